```python
import jax, jax.numpy as jnp
from jax import lax
import numpy as np

D_MODEL = 1024
BATCH = 4
SEQ = 8192
DEPTH = 4

GRID_W = 64
CTX_LEN = 256
EPS = 1e-6
SSD_HEADS = 8
SSD_HEAD_DIM = 64
SSD_INNER = SSD_HEADS * SSD_HEAD_DIM
SSD_GROUPS = 2
SSD_STATE = 64
SSD_CONV = 5
SSD_CHUNK = 128
SSD_XBC = SSD_INNER + 2 * SSD_GROUPS * SSD_STATE
POOL_WINDOWS = (2, 4, 8, 16)
POOL_WIDTH = 512
POOL_GROUP = POOL_WIDTH // len(POOL_WINDOWS)
RET_HEADS = 8
RET_QK_DIM = 64
RET_V_DIM = 64
RET_QK_WIDTH = RET_HEADS * RET_QK_DIM
RET_WIDTH = RET_HEADS * RET_V_DIM
RET_CHUNK = 128
ROPE_BASE = 10000.0
N_BRANCH = 3
BRANCH_WIDTH = 512
COL_Z = 0
COL_XBC = COL_Z + SSD_INNER
COL_DT = COL_XBC + SSD_XBC
COL_POOL = COL_DT + 2 * SSD_HEADS
COL_Q = COL_POOL + POOL_WIDTH
COL_K = COL_Q + RET_QK_WIDTH
COL_V = COL_K + RET_QK_WIDTH
COL_G = COL_V + RET_WIDTH
COL_GATE = COL_G + RET_WIDTH
IN_COLS = COL_GATE + N_BRANCH * D_MODEL
D_FF = 2816
N_EXPERTS = 8
TOP_K = 2
EXPERT_FF = 3584
MOE_BLOCK = 256
N_DENSE = (DEPTH + 1) // 2
N_MOE = DEPTH // 2

kernel_name = 'hybrid_ssd_pool_retention_moe_dit'


def rmsnorm(x, g):
    xf = x.astype(jnp.float32)
    y = xf * lax.rsqrt(jnp.mean(xf * xf, axis=-1, keepdims=True) + EPS)
    return (y * g.astype(jnp.float32)).astype(x.dtype)


def adaln(cvec, w, b):
    m = jax.nn.silu(cvec) @ w + b
    return jnp.split(m[:, None, :], 6, axis=-1)


def modulate(h, shift, scale):
    return h * (1 + scale) + shift


def flip(a):
    return jnp.flip(a, axis=1)


def centred_dwconv(u, w, bias):
    ch = u.shape[-1]
    pad = w.shape[0] // 2
    y = lax.conv_general_dilated(u, w[:, None, :].astype(u.dtype), window_strides=(1,),
                                 padding=[(pad, pad)], dimension_numbers=('NWC', 'WIO', 'NWC'),
                                 feature_group_count=ch)
    return y + bias


def ssd_chunk_scan(xdt, da, bm, cm, s0):
    b, l, h, p = xdt.shape
    n = bm.shape[-1]
    q = SSD_CHUNK
    nc = l // q
    X = xdt.reshape(b, nc, q, h, p)
    Bc = bm.reshape(b, nc, q, h, n)
    Cc = cm.reshape(b, nc, q, h, n)
    A = da.astype(jnp.float32).reshape(b, nc, q, h).transpose(0, 3, 1, 2)
    acs = jnp.cumsum(A, axis=-1)
    causal = jnp.tril(jnp.ones((q, q), dtype=bool))
    seg = acs[..., :, None] - acs[..., None, :]
    Lm = jnp.exp(jnp.where(causal, seg, -jnp.inf))
    scores = jnp.einsum('bclhn,bcshn->bhcls', Cc, Bc) * Lm
    y_diag = jnp.einsum('bhcls,bcshp->bclhp', scores, X)
    decay_states = jnp.exp(acs[..., -1:] - acs)
    states = jnp.einsum('bclhn,bhcl,bclhp->bchpn', Bc, decay_states, X)
    chunk_decay = jnp.exp(acs[..., -1])

    def step(s, inp):
        st, dec = inp
        return dec[:, :, None, None] * s + st, s

    s_fin, prev = lax.scan(step, s0, (states.transpose(1, 0, 2, 3, 4), chunk_decay.transpose(2, 0, 1)))
    prev = prev.transpose(1, 0, 2, 3, 4)
    y_off = jnp.einsum('bclhn,bchpn,bhcl->bclhp', Cc, prev, jnp.exp(acs))
    return (y_diag + y_off).reshape(b, l, h, p), s_fin


def retention_chunk_scan(q, k, v, log_gamma, s0):
    b, l, h, dk = q.shape
    dv = v.shape[-1]
    Q = RET_CHUNK
    nc = l // Q
    qc = q.reshape(b, nc, Q, h, dk)
    kc = k.reshape(b, nc, Q, h, dk)
    vc = v.reshape(b, nc, Q, h, dv)
    idx = jnp.arange(Q, dtype=jnp.float32)
    diff = idx[:, None] - idx[None, :]
    dmat = jnp.where(diff >= 0, jnp.exp(log_gamma[:, None, None] * jnp.maximum(diff, 0.0)), 0.0)
    inner = jnp.einsum('bcihd,bcmhd->bchim', qc, kc) * dmat
    y_in = jnp.einsum('bchim,bcmhe->bcihe', inner, vc)
    k_dec = jnp.exp(log_gamma[:, None] * (Q - 1 - idx)[None, :])
    kv = jnp.einsum('bcmhd,hm,bcmhe->bchde', kc, k_dec, vc)
    chunk_dec = jnp.exp(log_gamma * Q)

    def step(s, kv_c):
        return chunk_dec[None, :, None, None] * s + kv_c, s

    s_fin, prev = lax.scan(step, s0, kv.transpose(1, 0, 2, 3, 4))
    prev = prev.transpose(1, 0, 2, 3, 4)
    q_dec = jnp.exp(log_gamma[:, None] * (idx + 1)[None, :])
    y_x = jnp.einsum('bcihd,bchde,hi->bcihe', qc, prev, q_dec)
    return (y_in + y_x).reshape(b, l, h, dv), s_fin


def axial_rope(x, row, col):
    n_pair = x.shape[-1] // 2
    n_axis = n_pair // 2
    inv = ROPE_BASE ** (-jnp.arange(n_axis, dtype=jnp.float32) / n_axis)
    ang = jnp.concatenate([row[:, None] * inv, col[:, None] * inv], axis=-1)[None, :, None, :]
    cos, sin = jnp.cos(ang), jnp.sin(ang)
    x1, x2 = x[..., :n_pair], x[..., n_pair:]
    return jnp.concatenate([x1 * cos - x2 * sin, x1 * sin + x2 * cos], axis=-1)


def head_groupnorm(y):
    y = y.astype(jnp.float32)
    mu = jnp.mean(y, axis=-1, keepdims=True)
    var = jnp.mean(jnp.square(y - mu), axis=-1, keepdims=True)
    return (y - mu) * lax.rsqrt(var + EPS)


def ssd_branch(pc, pl, conv_w, conv_b, dt_bias, a_log, d_skip, norm_g, need_ctx):
    f32 = jnp.float32
    gn = SSD_GROUPS * SSD_STATE
    rep = SSD_HEADS // SSD_GROUPS

    def prep(p):
        b, l = p.shape[:2]
        z = p[..., COL_Z:COL_Z + SSD_INNER]
        xbc = jax.nn.silu(centred_dwconv(p[..., COL_XBC:COL_XBC + SSD_XBC], conv_w, conv_b))
        xs = xbc[..., :SSD_INNER].reshape(b, l, SSD_HEADS, SSD_HEAD_DIM)
        bm = jnp.repeat(xbc[..., SSD_INNER:SSD_INNER + gn].reshape(b, l, SSD_GROUPS, SSD_STATE), rep, axis=2)
        cm = jnp.repeat(xbc[..., SSD_INNER + gn:].reshape(b, l, SSD_GROUPS, SSD_STATE), rep, axis=2)
        dt_raw = p[..., COL_DT:COL_DT + 2 * SSD_HEADS].astype(f32)
        return z, xs, bm, cm, dt_raw

    def direction(xs, dt_raw, d):
        dt = jax.nn.softplus(dt_raw[..., d * SSD_HEADS:(d + 1) * SSD_HEADS] + dt_bias[d].astype(f32))
        da = dt * (-jnp.exp(a_log[d].astype(f32)))
        return xs * dt[..., None], da

    zc, xc, bc, cc, dtc = prep(pc)
    zl, xl, bl, cl, dtl = prep(pl)
    s0 = jnp.zeros((pc.shape[0], SSD_HEADS, SSD_HEAD_DIM, SSD_STATE), f32)
    u_c, a_c = direction(xc, dtc, 0)
    u_l, a_l = direction(xl, dtl, 0)
    yc_f, sc_f = ssd_chunk_scan(u_c, a_c, bc, cc, s0)
    yl_f, _ = ssd_chunk_scan(u_l, a_l, bl, cl, sc_f)
    u_c, a_c = direction(xc, dtc, 1)
    u_l, a_l = direction(xl, dtl, 1)
    yc_b, sc_b = ssd_chunk_scan(flip(u_c), flip(a_c), flip(bc), flip(cc), s0)
    yl_b, _ = ssd_chunk_scan(flip(u_l), flip(a_l), flip(bl), flip(cl), sc_b)

    def finish(y_f, y_b, xs, z):
        b, l = xs.shape[:2]
        y = y_f + flip(y_b) + d_skip.astype(f32)[:, None] * xs
        return rmsnorm(y.reshape(b, l, SSD_INNER) * jax.nn.silu(z.astype(f32)), norm_g)

    out_l = finish(yl_f, yl_b, xl, zl)
    out_c = finish(yc_f, yc_b, xc, zc) if need_ctx else None
    return out_c, out_l


def centred_window_mean(u, w):
    b, l, ch = u.shape
    left = w // 2
    right = w - 1 - left
    cs = jnp.concatenate([jnp.zeros((b, 1, ch), jnp.float32), jnp.cumsum(u.astype(jnp.float32), axis=1)], axis=1)
    t = jnp.arange(l)
    lo = jnp.maximum(t - left, 0)
    hi = jnp.minimum(t + right, l - 1) + 1
    s = jnp.take(cs, hi, axis=1) - jnp.take(cs, lo, axis=1)
    return s / (hi - lo).astype(jnp.float32)[None, :, None]


def pool_branch(p, pool_w, pool_scale):
    u = p[..., COL_POOL:COL_POOL + POOL_WIDTH]
    outs = []
    for gi, w in enumerate(POOL_WINDOWS):
        ug = u[..., gi * POOL_GROUP:(gi + 1) * POOL_GROUP]
        mixed = centred_window_mean(ug, w) - ug.astype(jnp.float32)
        outs.append(mixed @ pool_w[gi])
    return jnp.concatenate(outs, axis=-1) * pool_scale


def retention_branch(pc, pl, decay_logit, row, col, need_ctx):
    f32 = jnp.float32
    log_gamma = jax.nn.log_sigmoid(decay_logit.astype(f32))

    def split(p):
        b, l = p.shape[:2]
        q = p[..., COL_Q:COL_Q + RET_QK_WIDTH].reshape(b, l, RET_HEADS, RET_QK_DIM)
        k = p[..., COL_K:COL_K + RET_QK_WIDTH].reshape(b, l, RET_HEADS, RET_QK_DIM) * (RET_QK_DIM ** -0.5)
        v = p[..., COL_V:COL_V + RET_WIDTH].reshape(b, l, RET_HEADS, RET_V_DIM)
        g = p[..., COL_G:COL_G + RET_WIDTH]
        return q, k, v, g

    qc, kc, vc, gc = split(pc)
    ql, kl, vl, gl = split(pl)
    ql = axial_rope(ql, row, col)
    kl = axial_rope(kl, row, col)
    s0 = jnp.zeros((pc.shape[0], RET_HEADS, RET_QK_DIM, RET_V_DIM), f32)
    yc_f, sf = retention_chunk_scan(qc, kc, vc, log_gamma[0], s0)
    yl_f, _ = retention_chunk_scan(ql, kl, vl, log_gamma[0], sf)
    yc_b, sb = retention_chunk_scan(flip(qc), flip(kc), flip(vc), log_gamma[1], s0)
    yl_b, _ = retention_chunk_scan(flip(ql), flip(kl), flip(vl), log_gamma[1], sb)

    def finish(y, g):
        b, l = y.shape[:2]
        return head_groupnorm(y).reshape(b, l, RET_WIDTH) * jax.nn.silu(g.astype(f32))

    out_l = finish(yl_f + flip(yl_b), gl)
    out_c = finish(yc_f + flip(yc_b), gc) if need_ctx else None
    return out_c, out_l


def merge_branches(p, branches, w_branch, w_out):
    b, l = p.shape[:2]
    gates = jax.nn.sigmoid(p[..., COL_GATE:COL_GATE + N_BRANCH * D_MODEL].astype(jnp.float32))
    gates = gates.reshape(b, l, N_BRANCH, D_MODEL)
    acc = 0.0
    for i, br in enumerate(branches):
        acc = acc + gates[:, :, i, :] * (br @ w_branch[i]).astype(jnp.float32)
    return acc.astype(p.dtype) @ w_out


def token_mixer(hc, hl, w_in, conv_w, conv_b, dt_bias, a_log, d_skip, ssd_norm_g, pool_w, pool_scale,
                ret_decay_logit, w_branch, w_out, row, col, need_ctx):
    pc = hc @ w_in
    pl = hl @ w_in
    sc, sl = ssd_branch(pc, pl, conv_w, conv_b, dt_bias, a_log, d_skip, ssd_norm_g, need_ctx)
    rc, rl = retention_branch(pc, pl, ret_decay_logit, row, col, need_ctx)
    out_l = merge_branches(pl, (sl, pool_branch(pl, pool_w, pool_scale), rl), w_branch, w_out)
    out_c = merge_branches(pc, (sc, pool_branch(pc, pool_w, pool_scale), rc), w_branch, w_out) if need_ctx else None
    return out_c, out_l


def swiglu(h, w13, w2):
    g, u = jnp.split(h @ w13, 2, axis=-1)
    return (jax.nn.silu(g) * u) @ w2


def moe_swiglu(h, w_router, w13, w2):
    n, d = h.shape
    logits = (h @ w_router).astype(jnp.float32)
    top_v, top_i = lax.top_k(logits, TOP_K)
    top_w = jax.nn.softmax(top_v, axis=-1)
    n_slots = n * TOP_K
    e_flat = top_i.reshape(-1)
    w_flat = top_w.reshape(-1)
    tok_flat = jnp.repeat(jnp.arange(n, dtype=jnp.int32), TOP_K)
    order = jnp.argsort(e_flat)
    e_sorted = e_flat[order]
    tok_sorted = tok_flat[order]
    w_sorted = w_flat[order]
    counts = jnp.bincount(e_flat, length=N_EXPERTS)
    starts = jnp.cumsum(counts) - counts
    padded = (counts + MOE_BLOCK - 1) // MOE_BLOCK * MOE_BLOCK
    pad_ends = jnp.cumsum(padded)
    pad_starts = pad_ends - padded
    dest = pad_starts[e_sorted] + jnp.arange(n_slots) - starts[e_sorted]
    n_blocks = -(-(n_slots + N_EXPERTS * (MOE_BLOCK - 1)) // MOE_BLOCK)
    cap = n_blocks * MOE_BLOCK
    buf_tok = jnp.zeros((cap,), jnp.int32).at[dest].set(tok_sorted)
    buf_w = jnp.zeros((cap,), jnp.float32).at[dest].set(w_sorted)
    block_e = jnp.minimum(jnp.searchsorted(pad_ends, jnp.arange(n_blocks) * MOE_BLOCK, side='right'), N_EXPERTS - 1)
    xb = h[buf_tok].reshape(n_blocks, MOE_BLOCK, d)

    def run_block(args):
        xblk, e = args
        g, u = jnp.split(xblk @ w13[e], 2, axis=-1)
        return (jax.nn.silu(g) * u) @ w2[e]

    yb = lax.map(run_block, (xb, block_e)).reshape(cap, d).astype(jnp.float32)
    y = jnp.zeros((n, d), jnp.float32).at[buf_tok].add(yb * buf_w[:, None])
    return y.astype(h.dtype)


def setup_inputs(seed: int = 0) -> dict:
    key = jax.random.key(seed)
    ks = jax.random.split(key, 32)
    nrm = jax.random.normal
    D = D_MODEL
    dt = jnp.exp(jax.random.uniform(ks[10], (DEPTH, 2, SSD_HEADS)) * (np.log(0.1) - np.log(0.001)) + np.log(0.001))
    gamma_exp = 5.0 + jnp.arange(RET_HEADS, dtype=jnp.float32)
    ret_logit = jnp.log(2.0 ** gamma_exp - 1.0)
    return {
        'x': nrm(ks[0], (BATCH, SEQ, D), jnp.float32),
        'c': nrm(ks[1], (BATCH, D), jnp.float32),
        'ctx': nrm(ks[2], (BATCH, CTX_LEN, D), jnp.float32),
        'c_ctx': nrm(ks[3], (D,), jnp.float32),
        'ada_w': nrm(ks[4], (DEPTH, D, 6 * D), jnp.float32) * (0.5 * D ** -0.5),
        'ada_b': nrm(ks[5], (DEPTH, 6 * D), jnp.float32) * 0.02,
        'norm1_g': 1.0 + 0.02 * nrm(ks[6], (DEPTH, D), jnp.float32),
        'norm2_g': 1.0 + 0.02 * nrm(ks[7], (DEPTH, D), jnp.float32),
        'w_in': nrm(ks[8], (DEPTH, D, IN_COLS), jnp.float32) * D ** -0.5,
        'ssd_conv_w': nrm(ks[9], (DEPTH, SSD_CONV, SSD_XBC), jnp.float32) * SSD_CONV ** -0.5,
        'ssd_conv_b': nrm(ks[11], (DEPTH, SSD_XBC), jnp.float32) * 0.02,
        'ssd_dt_bias': dt + jnp.log(-jnp.expm1(-dt)),
        'ssd_a_log': jnp.log(jax.random.uniform(ks[12], (DEPTH, 2, SSD_HEADS), minval=1.0, maxval=16.0)),
        'ssd_d': 1.0 + 0.02 * nrm(ks[13], (DEPTH, SSD_HEADS), jnp.float32),
        'ssd_norm_g': 1.0 + 0.02 * nrm(ks[14], (DEPTH, SSD_INNER), jnp.float32),
        'pool_w': nrm(ks[15], (DEPTH, len(POOL_WINDOWS), POOL_GROUP, POOL_GROUP), jnp.float32) * POOL_GROUP ** -0.5,
        'pool_scale': 1.0 + 0.02 * nrm(ks[16], (DEPTH, POOL_WIDTH), jnp.float32),
        'ret_decay_logit': ret_logit[None, None, :] + 0.01 * nrm(ks[17], (DEPTH, 2, RET_HEADS), jnp.float32),
        'w_branch': nrm(ks[18], (DEPTH, N_BRANCH, BRANCH_WIDTH, D), jnp.float32) * BRANCH_WIDTH ** -0.5,
        'w_out': nrm(ks[19], (DEPTH, D, D), jnp.float32) * D ** -0.5,
        'ffn_w13': nrm(ks[20], (N_DENSE, D, 2 * D_FF), jnp.float32) * D ** -0.5,
        'ffn_w2': nrm(ks[21], (N_DENSE, D_FF, D), jnp.float32) * D_FF ** -0.5,
        'moe_router': nrm(ks[22], (N_MOE, D, N_EXPERTS), jnp.float32) * D ** -0.5,
        'moe_w13': nrm(ks[23], (N_MOE, N_EXPERTS, D, 2 * EXPERT_FF), jnp.float32) * D ** -0.5,
        'moe_w2': nrm(ks[24], (N_MOE, N_EXPERTS, EXPERT_FF, D), jnp.float32) * EXPERT_FF ** -0.5,
        'final_norm_g': 1.0 + 0.02 * nrm(ks[25], (D,), jnp.float32),
    }


def reference(x, c, ctx, c_ctx, ada_w, ada_b, norm1_g, norm2_g, w_in, ssd_conv_w, ssd_conv_b, ssd_dt_bias,
              ssd_a_log, ssd_d, ssd_norm_g, pool_w, pool_scale, ret_decay_logit, w_branch, w_out,
              ffn_w13, ffn_w2, moe_router, moe_w13, moe_w2, final_norm_g):
    f32 = jnp.float32
    b, seq = x.shape[0], x.shape[1]
    rows = seq // GRID_W
    row = jnp.repeat(jnp.arange(rows, dtype=f32), GRID_W)
    col = (jnp.arange(rows * GRID_W) % GRID_W).astype(f32)
    h_ctx = ctx
    for layer in range(DEPTH):
        need_ctx = layer < DEPTH - 1
        sh1, sc1, g1, sh2, sc2, g2 = adaln(c, ada_w[layer], ada_b[layer])
        csh1, csc1, cg1, csh2, csc2, cg2 = adaln(c_ctx[None, :], ada_w[layer], ada_b[layer])
        hl = modulate(rmsnorm(x, norm1_g[layer]), sh1, sc1)
        hc = modulate(rmsnorm(h_ctx, norm1_g[layer]), csh1, csc1)
        mc, ml = token_mixer(hc, hl, w_in[layer], ssd_conv_w[layer], ssd_conv_b[layer], ssd_dt_bias[layer],
                             ssd_a_log[layer], ssd_d[layer], ssd_norm_g[layer], pool_w[layer], pool_scale[layer],
                             ret_decay_logit[layer], w_branch[layer], w_out[layer], row, col, need_ctx)
        x = x + (g1 * ml).astype(x.dtype)
        if need_ctx:
            h_ctx = h_ctx + (cg1 * mc).astype(h_ctx.dtype)
        tok = modulate(rmsnorm(x, norm2_g[layer]), sh2, sc2).reshape(-1, D_MODEL)
        if need_ctx:
            fc = modulate(rmsnorm(h_ctx, norm2_g[layer]), csh2, csc2)
            tok = jnp.concatenate([fc.reshape(-1, D_MODEL), tok], axis=0)
        if layer % 2 == 0:
            y = swiglu(tok, ffn_w13[layer // 2], ffn_w2[layer // 2])
        else:
            y = moe_swiglu(tok, moe_router[layer // 2], moe_w13[layer // 2], moe_w2[layer // 2])
        n_ctx_tok = tok.shape[0] - b * seq
        x = x + (g2 * y[n_ctx_tok:].reshape(x.shape)).astype(x.dtype)
        if need_ctx:
            h_ctx = h_ctx + (cg2 * y[:n_ctx_tok].reshape(h_ctx.shape)).astype(h_ctx.dtype)
    return rmsnorm(x, final_norm_g)
```

```python
import functools

import numpy as np
import jax
import jax.numpy as jnp
from jax import lax
from jax.experimental import pallas as pl
from jax.experimental.pallas import tpu as pltpu

F32 = jnp.float32
BF16 = jnp.bfloat16
HIGHEST = lax.Precision.HIGHEST

D_MODEL = 1024
GRID_W = 64
EPS = 1e-6
CHUNK = 128
HALO = 8
SSD_HEADS = 8
SSD_HEAD_DIM = 64
SSD_INNER = 512
SSD_STATE = 64
SSD_GROUPS = 2
SSD_CONV = 5
SSD_XBC = 768
POOL_WINDOWS = (2, 4, 8, 16)
POOL_WIDTH = 512
POOL_GROUP = 128
RET_HEADS = 8
RET_DIM = 64
RET_WIDTH = 512
ROPE_BASE = 10000.0
N_BRANCH = 3
N_EXPERTS = 8
TOP_K = 2
LANES = 128

COL_Z = 0
COL_XBC = 512
COL_DT = 1280
COL_POOL = 1296
COL_Q = 1808
COL_G_END = 3856
COL_GATE = 3856
IN_COLS = 6928
P_GATE = 0
P_Q = 3072
P_K = 3584
P_V = 4096
P_G = 4608
P_POOL = 5120
P_Z = 5632
P_XBC = 6144
P_DT = 6912
P_COLS = 7040

VMEM_LIMIT = 56 * 1024 * 1024


def _silu(v):
    return v * jax.nn.sigmoid(v)


def _softplus(v):
    return jnp.maximum(v, 0.0) + jnp.log1p(jnp.exp(-jnp.abs(v)))


def _cparams(sem):
    return pltpu.CompilerParams(dimension_semantics=sem, vmem_limit_bytes=VMEM_LIMIT)


def _ada_kernel(c_ref, w_ref, b_ref, o_ref):
    cv = c_ref[...]
    o_ref[...] = jnp.dot(_silu(cv), w_ref[...], precision=HIGHEST,
                         preferred_element_type=F32) + b_ref[...]


def _ada_all(cvec, ada_w, ada_b):
    depth, d, n6 = ada_w.shape
    tn = 1536
    return pl.pallas_call(
        _ada_kernel,
        out_shape=jax.ShapeDtypeStruct((depth, 8, n6), F32),
        grid=(depth, n6 // tn),
        in_specs=[
            pl.BlockSpec((8, d), lambda l, j: (0, 0)),
            pl.BlockSpec((None, d, tn), lambda l, j: (l, 0, j)),
            pl.BlockSpec((None, 1, tn), lambda l, j: (l, 0, j)),
        ],
        out_specs=pl.BlockSpec((None, 8, tn), lambda l, j: (l, 0, j)),
        compiler_params=_cparams(("arbitrary", "arbitrary")),
        name="ada_mods",
    )(cvec, ada_w, ada_b.reshape(depth, 1, n6))


class _Layout:
    def __init__(self, b, ctx_len, seq):
        self.b, self.ctx, self.seq = b, ctx_len, seq
        self.n_ctx_rows = b * ctx_len
        self.n = b * (ctx_len + seq)
        self.cch = ctx_len // CHUNK
        self.lch = seq // CHUNK
        self.nch = self.cch + self.lch

    def mod_row(self, tile, tm):
        ctx_tiles = self.n_ctx_rows // tm
        per_b = self.seq // tm
        return jnp.where(tile < ctx_tiles, self.b, (tile - ctx_tiles) // per_b)

    def chunk_block(self, bi, c):
        return jnp.where(c < self.cch, bi * self.cch + c,
                         self.b * self.cch + bi * self.lch + (c - self.cch))

    def fwd_chunk(self, s):
        return s

    def bwd_chunk(self, s):
        return jnp.where(s < self.cch, self.cch - 1 - s, self.nch - 1 - (s - self.cch))


def _row_tile(lay, cap):
    tm = cap
    while lay.n_ctx_rows % tm or lay.seq % tm:
        tm //= 2
    return tm


def _in_kernel(x_ref, m_ref, g_ref, w_ref, o_ref, h_ref):
    @pl.when(pl.program_id(1) == 0)
    def _():
        x = x_ref[...]
        y = x * lax.rsqrt(jnp.mean(x * x, axis=-1, keepdims=True) + EPS) * g_ref[...]
        h_ref[...] = (y * (1.0 + m_ref[1]) + m_ref[0]).astype(BF16)

    o_ref[...] = jnp.dot(h_ref[...], w_ref[...], preferred_element_type=F32)


def _in_proj(lay, x, mods, g, w):
    n, d = x.shape
    tm = _row_tile(lay, 1024)
    tn = 1408
    return pl.pallas_call(
        _in_kernel,
        out_shape=jax.ShapeDtypeStruct((n, P_COLS), F32),
        grid=(n // tm, P_COLS // tn),
        in_specs=[
            pl.BlockSpec((tm, d), lambda i, j: (i, 0)),
            pl.BlockSpec((None, 6, 1, d), lambda i, j: (lay.mod_row(i, tm), 0, 0, 0)),
            pl.BlockSpec((1, d), lambda i, j: (0, 0)),
            pl.BlockSpec((d, tn), lambda i, j: (0, j)),
        ],
        out_specs=pl.BlockSpec((tm, tn), lambda i, j: (i, j)),
        scratch_shapes=[pltpu.VMEM((tm, d), BF16)],
        compiler_params=_cparams(("arbitrary", "arbitrary")),
        name="in_proj",
    )(x, mods, g, w)


def _tri_consts():
    i = np.arange(CHUNK)
    fwd = (i[None, :] <= i[:, None]).astype(np.float32)
    bwd = (i[None, :] >= i[:, None]).astype(np.float32)
    return np.stack([fwd, bwd])


def _expand_consts(heads, width):
    e = np.zeros((2, LANES, heads * width), np.float32)
    for d in range(2):
        for h in range(heads):
            e[d, d * heads + h, h * width:(h + 1) * width] = 1.0
    return e


def _ssd_direction(d, c, lay, xm_ref, xp_ref, xn_ref, dt_ref, tri_ref, exp_ref, cw_ref, cb_ref,
                   dtb_ref, alog_ref, dskip_ref, st_ref, o_ref, xw_ref):
    is_start = jnp.logical_or(c == 0, c == lay.cch)
    is_end = jnp.logical_or(c == lay.cch - 1, c == lay.nch - 1)
    xw_ref[0:HALO, :] = jnp.where(is_start, 0.0, xp_ref[...])
    xw_ref[HALO:HALO + CHUNK, :] = xm_ref[...]
    xw_ref[HALO + CHUNK:, :] = jnp.where(is_end, 0.0, xn_ref[...])
    acc = cb_ref[...] + cw_ref[0:1, :] * xw_ref[HALO - 2:HALO - 2 + CHUNK, :]
    for k in range(1, SSD_CONV):
        acc = acc + cw_ref[k:k + 1, :] * xw_ref[HALO - 2 + k:HALO - 2 + k + CHUNK, :]
    xbc = _silu(acc)
    xs = xbc[:, :SSD_INNER]
    bm = xbc[:, SSD_INNER:SSD_INNER + LANES]
    cm = xbc[:, SSD_INNER + LANES:]
    bt = bm.T.astype(BF16)
    cmb = cm.astype(BF16)

    dt_all = _softplus(dt_ref[...] + dtb_ref[...])
    a_all = dt_all * (-jnp.exp(alog_ref[...]))
    acs = jnp.dot(tri_ref[d], a_all, precision=HIGHEST, preferred_element_type=F32)
    acs_t = acs.T
    e = exp_ref[d]
    acs_x = jnp.dot(acs, e, precision=HIGHEST, preferred_element_type=F32)
    dt_x = jnp.dot(dt_all, e, precision=HIGHEST, preferred_element_type=F32)
    last = CHUNK - 1 if d == 0 else 0
    tot_x = acs_x[last:last + 1, :]
    u = xs * dt_x
    ud = (u * jnp.exp(tot_x - acs_x)).astype(BF16)
    ub = u.astype(BF16)
    off_x = jnp.exp(acs_x)
    cd_x = jnp.exp(tot_x)

    li = lax.broadcasted_iota(jnp.int32, (CHUNK, CHUNK), 0)
    si = lax.broadcasted_iota(jnp.int32, (CHUNK, CHUNK), 1)
    mask = (si <= li) if d == 0 else (si >= li)
    hpg = SSD_HEADS // SSD_GROUPS
    gw = hpg * SSD_HEAD_DIM
    for g in range(SSD_GROUPS):
        bt_g = bt[g * SSD_STATE:(g + 1) * SSD_STATE, :]
        c_g = cmb[:, g * SSD_STATE:(g + 1) * SSD_STATE]
        cb = jnp.dot(c_g, bt_g, preferred_element_type=F32)
        prev = st_ref[d, g]
        y_off = jnp.dot(c_g, prev.astype(BF16), preferred_element_type=F32)
        pieces = []
        for hh in range(hpg):
            h = g * hpg + hh
            col = acs[:, d * SSD_HEADS + h:d * SSD_HEADS + h + 1]
            row = acs_t[d * SSD_HEADS + h:d * SSD_HEADS + h + 1, :]
            lm = jnp.exp(jnp.where(mask, col - row, -jnp.inf))
            sc = (cb * lm).astype(BF16)
            pieces.append(jnp.dot(sc, ub[:, h * SSD_HEAD_DIM:(h + 1) * SSD_HEAD_DIM],
                                  preferred_element_type=F32))
        y_g = jnp.concatenate(pieces, axis=-1) + y_off * off_x[:, g * gw:(g + 1) * gw]
        if d == 0:
            y_g = y_g + dskip_ref[:, g * gw:(g + 1) * gw] * xs[:, g * gw:(g + 1) * gw]
        o_ref[:, g * gw:(g + 1) * gw] = y_g
        st_ref[d, g] = prev * cd_x[:, g * gw:(g + 1) * gw] + jnp.dot(
            bt_g, ud[:, g * gw:(g + 1) * gw], preferred_element_type=F32)


def _ssd_kernel(lay, xm_f, xp_f, xn_f, dt_f, xm_b, xp_b, xn_b, dt_b, tri_ref, exp_ref, cw_ref, cb_ref,
                dtb_ref, alog_ref, dskip_ref, of_ref, ob_ref, st_ref, xw_ref):
    s = pl.program_id(1)

    @pl.when(s == 0)
    def _():
        st_ref[...] = jnp.zeros_like(st_ref)

    _ssd_direction(0, lay.fwd_chunk(s), lay, xm_f, xp_f, xn_f, dt_f, tri_ref, exp_ref, cw_ref, cb_ref,
                   dtb_ref, alog_ref, dskip_ref, st_ref, of_ref, xw_ref)
    _ssd_direction(1, lay.bwd_chunk(s), lay, xm_b, xp_b, xn_b, dt_b, tri_ref, exp_ref, cw_ref, cb_ref,
                   dtb_ref, alog_ref, dskip_ref, st_ref, ob_ref, xw_ref)


def _halo_specs(lay, width, col_block, chunk_of):
    n8 = lay.n // HALO
    per = CHUNK // HALO

    def main(bi, s):
        return (lay.chunk_block(bi, chunk_of(s)), col_block)

    def prev(bi, s):
        return (jnp.maximum(lay.chunk_block(bi, chunk_of(s)) * per - 1, 0), col_block)

    def nxt(bi, s):
        return (jnp.minimum(lay.chunk_block(bi, chunk_of(s)) * per + per, n8 - 1), col_block)

    return [pl.BlockSpec((CHUNK, width), main), pl.BlockSpec((HALO, width), prev),
            pl.BlockSpec((HALO, width), nxt)]


def _ssd_scan(lay, p, conv_w, conv_b, dt_bias, a_log, d_skip):
    n = lay.n
    pad = LANES - 2 * SSD_HEADS
    dtb = jnp.pad(dt_bias.reshape(1, -1), ((0, 0), (0, pad)))
    alog = jnp.pad(a_log.reshape(1, -1), ((0, 0), (0, pad)))
    dskip = jnp.repeat(d_skip, SSD_HEAD_DIM).reshape(1, SSD_INNER)
    tri = jnp.asarray(_tri_consts())
    expand = jnp.asarray(_expand_consts(SSD_HEADS, SSD_HEAD_DIM))
    xbc_blk = P_XBC // SSD_XBC
    dt_blk = P_DT // LANES

    def dt_spec(chunk_of):
        return pl.BlockSpec((CHUNK, LANES), lambda bi, s: (lay.chunk_block(bi, chunk_of(s)), dt_blk))

    def out_spec(chunk_of):
        return pl.BlockSpec((CHUNK, SSD_INNER), lambda bi, s: (lay.chunk_block(bi, chunk_of(s)), 0))

    const2 = lambda bi, s: (0, 0)
    const3 = lambda bi, s: (0, 0, 0)
    in_specs = (
        _halo_specs(lay, SSD_XBC, xbc_blk, lay.fwd_chunk) + [dt_spec(lay.fwd_chunk)]
        + _halo_specs(lay, SSD_XBC, xbc_blk, lay.bwd_chunk) + [dt_spec(lay.bwd_chunk)]
        + [pl.BlockSpec((2, CHUNK, CHUNK), const3),
           pl.BlockSpec((2, LANES, SSD_INNER), const3),
           pl.BlockSpec((SSD_CONV, SSD_XBC), const2),
           pl.BlockSpec((1, SSD_XBC), const2),
           pl.BlockSpec((1, LANES), const2),
           pl.BlockSpec((1, LANES), const2),
           pl.BlockSpec((1, SSD_INNER), const2)])
    return pl.pallas_call(
        functools.partial(_ssd_kernel, lay),
        out_shape=(jax.ShapeDtypeStruct((n, SSD_INNER), F32), jax.ShapeDtypeStruct((n, SSD_INNER), F32)),
        grid=(lay.b, lay.nch),
        in_specs=in_specs,
        out_specs=(out_spec(lay.fwd_chunk), out_spec(lay.bwd_chunk)),
        scratch_shapes=[pltpu.VMEM((2, SSD_GROUPS, SSD_STATE, SSD_INNER // SSD_GROUPS), F32),
                        pltpu.VMEM((CHUNK + 2 * HALO, SSD_XBC), F32)],
        compiler_params=_cparams(("arbitrary", "arbitrary")),
        name="ssd_scan",
    )(p, p, p, p, p, p, p, p, tri, expand, conv_w, conv_b.reshape(1, -1), dtb, alog, dskip)


def _rope_tables(lay):
    n_axis = RET_DIM // 4
    t = np.arange(lay.seq)
    inv = ROPE_BASE ** (-np.arange(n_axis, dtype=np.float32) / n_axis)
    row = (t // GRID_W).astype(np.float32)
    colp = (t % GRID_W).astype(np.float32)
    ang = jnp.concatenate([jnp.asarray(row)[:, None] * inv, jnp.asarray(colp)[:, None] * inv], axis=-1)
    cos, sin = jnp.cos(ang), jnp.sin(ang)
    cos_l = jnp.concatenate([cos, cos, cos, cos], axis=-1)
    sin_l = jnp.concatenate([-sin, sin, -sin, sin], axis=-1)
    cos_t = jnp.concatenate([jnp.ones((lay.ctx, LANES), F32), cos_l], axis=0)
    sin_t = jnp.concatenate([jnp.zeros((lay.ctx, LANES), F32), sin_l], axis=0)
    return cos_t, sin_t


def _rope(xv, cos, sin):
    lane = lax.broadcasted_iota(jnp.int32, (CHUNK, LANES), 1)
    first_half = (lane % RET_DIM) < (RET_DIM // 2)
    out = []
    for j in range(RET_WIDTH // LANES):
        v = xv[:, j * LANES:(j + 1) * LANES]
        swapped = jnp.where(first_half, pltpu.roll(v, LANES - RET_DIM // 2, 1),
                            pltpu.roll(v, RET_DIM // 2, 1))
        out.append(v * cos + swapped * sin)
    return jnp.concatenate(out, axis=-1)


def _log_sigmoid(v):
    return -_softplus(-v)


def _ret_direction(d, q_ref, k_ref, v_ref, cos_ref, sin_ref, lgx_ref, lgp_ref, st_ref, o_ref):
    cos, sin = cos_ref[...], sin_ref[...]
    q = _rope(q_ref[...], cos, sin)
    k = _rope(k_ref[...], cos, sin) * (RET_DIM ** -0.5)
    v = v_ref[...]
    lg_x = _log_sigmoid(lgx_ref[d])
    lg_p = _log_sigmoid(lgp_ref[d])
    idx = lax.broadcasted_iota(jnp.int32, (CHUNK, 1), 0).astype(F32)
    if d == 0:
        k_pow, q_pow = (CHUNK - 1) - idx, idx + 1.0
    else:
        k_pow, q_pow = idx, CHUNK - idx
    vk = (v * jnp.exp(lg_x * k_pow)).astype(BF16)
    q_dec = jnp.exp(lg_x * q_pow)
    cd = jnp.exp(lg_x * float(CHUNK))
    qb, kb, vb = q.astype(BF16), k.astype(BF16), v.astype(BF16)
    kt = k.T.astype(BF16)
    ii = lax.broadcasted_iota(jnp.int32, (CHUNK, CHUNK), 0)
    mi = lax.broadcasted_iota(jnp.int32, (CHUNK, CHUNK), 1)
    diff = (ii - mi) if d == 0 else (mi - ii)
    dpos = jnp.maximum(diff, 0).astype(F32)
    pieces = []
    for h in range(RET_HEADS):
        sl = slice(h * RET_DIM, (h + 1) * RET_DIM)
        dmat = jnp.where(diff >= 0, jnp.exp(lg_p[:, h:h + 1] * dpos), 0.0)
        s = lax.dot_general(qb[:, sl], kb[:, sl], (((1,), (1,)), ((), ())), preferred_element_type=F32)
        y_in = jnp.dot((s * dmat).astype(BF16), vb[:, sl], preferred_element_type=F32)
        prev = st_ref[d, h]
        y_x = jnp.dot(qb[:, sl], prev.astype(BF16), preferred_element_type=F32)
        pieces.append(y_in + y_x * q_dec[:, sl])
        st_ref[d, h] = prev * cd[:, sl] + jnp.dot(kt[sl, :], vk[:, sl], preferred_element_type=F32)
    o_ref[...] = jnp.concatenate(pieces, axis=-1)


def _ret_kernel(qf, kf, vf, cosf, sinf, qb, kb, vb, cosb, sinb, lgx_ref, lgp_ref, of_ref, ob_ref, st_ref):
    @pl.when(pl.program_id(1) == 0)
    def _():
        st_ref[...] = jnp.zeros_like(st_ref)

    _ret_direction(0, qf, kf, vf, cosf, sinf, lgx_ref, lgp_ref, st_ref, of_ref)
    _ret_direction(1, qb, kb, vb, cosb, sinb, lgx_ref, lgp_ref, st_ref, ob_ref)


def _ret_scan(lay, p, decay_logit):
    n = lay.n
    cos_t, sin_t = _rope_tables(lay)
    lgx = jnp.repeat(decay_logit, RET_DIM, axis=-1).reshape(2, 1, RET_WIDTH)
    lgp = jnp.pad(decay_logit, ((0, 0), (0, LANES - RET_HEADS))).reshape(2, 1, LANES)

    def specs(chunk_of):
        def blk(cb):
            return pl.BlockSpec((CHUNK, RET_WIDTH), lambda bi, s: (lay.chunk_block(bi, chunk_of(s)), cb))
        tab = pl.BlockSpec((CHUNK, LANES), lambda bi, s: (chunk_of(s), 0))
        return [blk(P_Q // RET_WIDTH), blk(P_K // RET_WIDTH), blk(P_V // RET_WIDTH), tab, tab]

    def out_spec(chunk_of):
        return pl.BlockSpec((CHUNK, RET_WIDTH), lambda bi, s: (lay.chunk_block(bi, chunk_of(s)), 0))

    const3 = lambda bi, s: (0, 0, 0)
    return pl.pallas_call(
        _ret_kernel,
        out_shape=(jax.ShapeDtypeStruct((n, RET_WIDTH), F32), jax.ShapeDtypeStruct((n, RET_WIDTH), F32)),
        grid=(lay.b, lay.nch),
        in_specs=specs(lay.fwd_chunk) + specs(lay.bwd_chunk)
        + [pl.BlockSpec((2, 1, RET_WIDTH), const3), pl.BlockSpec((2, 1, LANES), const3)],
        out_specs=(out_spec(lay.fwd_chunk), out_spec(lay.bwd_chunk)),
        scratch_shapes=[pltpu.VMEM((2, RET_HEADS, RET_DIM, RET_DIM), F32)],
        compiler_params=_cparams(("arbitrary", "arbitrary")),
        name="ret_scan",
    )(p, p, p, cos_t, sin_t, p, p, p, cos_t, sin_t, lgx, lgp)


def _merge_kernel(lay, tm, emit_h2, x_ref, m_ref, gate_ref, z_ref, g_ref, um_ref, up_ref, un_ref,
                  sf_ref, sb_ref, rf_ref, rb_ref, sng_ref, pw_ref, ps_ref, wb_ref, wo_ref, n2g_ref,
                  wr_ref, *rest):
    if emit_h2:
        xo_ref, h2_ref, lg_ref, uw_ref = rest
    else:
        xo_ref, uw_ref = rest
    i = pl.program_id(0)
    ctx_tiles = lay.n_ctx_rows // tm
    per_ctx = lay.ctx // tm
    per_lat = lay.seq // tm
    in_ctx = i < ctx_tiles
    t_in_seg = jnp.where(in_ctx, i % per_ctx, (i - ctx_tiles) % per_lat)
    seg_tiles = jnp.where(in_ctx, per_ctx, per_lat)
    seg_len = jnp.where(in_ctx, lay.ctx, lay.seq)
    pos = t_in_seg * tm + lax.broadcasted_iota(jnp.int32, (tm, 1), 0)

    ys = (sf_ref[...] + sb_ref[...]) * _silu(z_ref[...])
    s_br = ys * lax.rsqrt(jnp.mean(ys * ys, axis=-1, keepdims=True) + EPS) * sng_ref[...]

    uw_ref[0:HALO, :] = jnp.where(t_in_seg == 0, 0.0, up_ref[...])
    uw_ref[HALO:HALO + tm, :] = um_ref[...]
    uw_ref[HALO + tm:, :] = jnp.where(t_in_seg == seg_tiles - 1, 0.0, un_ref[...])
    pooled = []
    for gi, w in enumerate(POOL_WINDOWS):
        left = w // 2
        right = w - 1 - left
        ls = slice(gi * POOL_GROUP, (gi + 1) * POOL_GROUP)
        tot = uw_ref[HALO - left:HALO - left + tm, ls]
        for o in range(-left + 1, right + 1):
            tot = tot + uw_ref[HALO + o:HALO + o + tm, ls]
        cnt = (jnp.minimum(pos + right, seg_len - 1) + 1 - jnp.maximum(pos - left, 0)).astype(F32)
        mixed = tot / cnt - uw_ref[HALO:HALO + tm, ls]
        pooled.append(jnp.dot(mixed.astype(BF16), pw_ref[gi], preferred_element_type=F32))
    p_br = jnp.concatenate(pooled, axis=-1) * ps_ref[...]

    yr = rf_ref[...] + rb_ref[...]
    normed = []
    for h in range(RET_HEADS):
        yh = yr[:, h * RET_DIM:(h + 1) * RET_DIM]
        mu = jnp.mean(yh, axis=-1, keepdims=True)
        dv = yh - mu
        var = jnp.mean(dv * dv, axis=-1, keepdims=True)
        normed.append(dv * lax.rsqrt(var + EPS))
    r_br = jnp.concatenate(normed, axis=-1) * _silu(g_ref[...])

    acc = None
    for bi, br in enumerate((s_br, p_br, r_br)):
        gate = jax.nn.sigmoid(gate_ref[:, bi * D_MODEL:(bi + 1) * D_MODEL])
        term = gate * jnp.dot(br.astype(BF16), wb_ref[bi], preferred_element_type=F32)
        acc = term if acc is None else acc + term
    mix = jnp.dot(acc.astype(BF16), wo_ref[...], preferred_element_type=F32)
    xn = x_ref[...] + m_ref[2] * mix
    xo_ref[...] = xn
    if emit_h2:
        y = xn * lax.rsqrt(jnp.mean(xn * xn, axis=-1, keepdims=True) + EPS) * n2g_ref[...]
        h2 = y * (1.0 + m_ref[4]) + m_ref[3]
        h2_ref[...] = h2
        lg_ref[...] = jnp.dot(h2, wr_ref[...], precision=HIGHEST, preferred_element_type=F32)


def _merge(lay, x, mods, p, ssd_f, ssd_b, ret_f, ret_b, ssd_norm_g, pool_w, pool_scale, w_branch, w_out,
           norm2_g, w_router_pad, emit_h2):
    n, d = x.shape
    tm = _row_tile(lay, 256)
    n8 = n // HALO
    per = tm // HALO
    pool_blk = P_POOL // POOL_WIDTH
    row = lambda i: (i, 0)
    const2 = lambda i: (0, 0)
    const3 = lambda i: (0, 0, 0)
    in_specs = [
        pl.BlockSpec((tm, d), row),
        pl.BlockSpec((None, 6, 1, d), lambda i: (lay.mod_row(i, tm), 0, 0, 0)),
        pl.BlockSpec((tm, N_BRANCH * d), lambda i: (i, P_GATE // (N_BRANCH * d))),
        pl.BlockSpec((tm, SSD_INNER), lambda i: (i, P_Z // SSD_INNER)),
        pl.BlockSpec((tm, RET_WIDTH), lambda i: (i, P_G // RET_WIDTH)),
        pl.BlockSpec((tm, POOL_WIDTH), lambda i: (i, pool_blk)),
        pl.BlockSpec((HALO, POOL_WIDTH), lambda i: (jnp.maximum(i * per - 1, 0), pool_blk)),
        pl.BlockSpec((HALO, POOL_WIDTH), lambda i: (jnp.minimum(i * per + per, n8 - 1), pool_blk)),
        pl.BlockSpec((tm, SSD_INNER), row),
        pl.BlockSpec((tm, SSD_INNER), row),
        pl.BlockSpec((tm, RET_WIDTH), row),
        pl.BlockSpec((tm, RET_WIDTH), row),
        pl.BlockSpec((1, SSD_INNER), const2),
        pl.BlockSpec((len(POOL_WINDOWS), POOL_GROUP, POOL_GROUP), const3),
        pl.BlockSpec((1, POOL_WIDTH), const2),
        pl.BlockSpec((N_BRANCH, SSD_INNER, d), const3),
        pl.BlockSpec((d, d), const2),
        pl.BlockSpec((1, d), const2),
        pl.BlockSpec((d, LANES), const2),
    ]
    out_shape = [jax.ShapeDtypeStruct((n, d), F32)]
    out_specs = [pl.BlockSpec((tm, d), row)]
    if emit_h2:
        out_shape += [jax.ShapeDtypeStruct((n, d), F32), jax.ShapeDtypeStruct((n, LANES), F32)]
        out_specs += [pl.BlockSpec((tm, d), row), pl.BlockSpec((tm, LANES), row)]
    return pl.pallas_call(
        functools.partial(_merge_kernel, lay, tm, emit_h2),
        out_shape=tuple(out_shape),
        grid=(n // tm,),
        in_specs=in_specs,
        out_specs=tuple(out_specs),
        scratch_shapes=[pltpu.VMEM((tm + 2 * HALO, POOL_WIDTH), F32)],
        compiler_params=_cparams(("arbitrary",)),
        name="merge_h2" if emit_h2 else "merge",
    )(x, mods, p, p, p, p, p, p, ssd_f, ssd_b, ret_f, ret_b, ssd_norm_g.reshape(1, -1),
      pool_w.astype(BF16), pool_scale.reshape(1, -1), w_branch.astype(BF16), w_out.astype(BF16),
      norm2_g.reshape(1, -1), w_router_pad)


def _ffn_kernel(x_ref, m_ref, g_ref, wg_ref, wu_ref, w2_ref, o_ref, h_ref, acc_ref):
    j = pl.program_id(1)

    @pl.when(j == 0)
    def _():
        x = x_ref[...]
        y = x * lax.rsqrt(jnp.mean(x * x, axis=-1, keepdims=True) + EPS) * g_ref[...]
        h_ref[...] = (y * (1.0 + m_ref[4]) + m_ref[3]).astype(BF16)
        acc_ref[...] = jnp.zeros_like(acc_ref)

    h = h_ref[...]
    gt = jnp.dot(h, wg_ref[...], preferred_element_type=F32)
    up = jnp.dot(h, wu_ref[...], preferred_element_type=F32)
    acc_ref[...] += jnp.dot((_silu(gt) * up).astype(BF16), w2_ref[...], preferred_element_type=F32)

    @pl.when(j == pl.num_programs(1) - 1)
    def _():
        o_ref[...] = x_ref[...] + m_ref[5] * acc_ref[...]


def _ffn_dense(lay, x, mods, g, w13, w2):
    n, d = x.shape
    ff = w2.shape[0]
    tm = _row_tile(lay, 1024)
    tf = 1408
    nf = ff // tf
    return pl.pallas_call(
        _ffn_kernel,
        out_shape=jax.ShapeDtypeStruct((n, d), F32),
        grid=(n // tm, nf),
        in_specs=[
            pl.BlockSpec((tm, d), lambda i, j: (i, 0)),
            pl.BlockSpec((None, 6, 1, d), lambda i, j: (lay.mod_row(i, tm), 0, 0, 0)),
            pl.BlockSpec((1, d), lambda i, j: (0, 0)),
            pl.BlockSpec((d, tf), lambda i, j: (0, j)),
            pl.BlockSpec((d, tf), lambda i, j: (0, j + nf)),
            pl.BlockSpec((tf, d), lambda i, j: (j, 0)),
        ],
        out_specs=pl.BlockSpec((tm, d), lambda i, j: (i, 0)),
        scratch_shapes=[pltpu.VMEM((tm, d), BF16), pltpu.VMEM((tm, d), F32)],
        compiler_params=_cparams(("arbitrary", "arbitrary")),
        name="ffn_dense",
    )(x, mods, g, w13, w13, w2)


MOE_BM = 512


def _moe_kernel(be_ref, nb_ref, x_ref, wt_ref, wg_ref, wu_ref, w2_ref, o_ref, acc_ref):
    i, j = pl.program_id(0), pl.program_id(1)

    @pl.when(i < nb_ref[0])
    def _():
        @pl.when(j == 0)
        def _():
            acc_ref[...] = jnp.zeros_like(acc_ref)

        h = x_ref[...].astype(BF16)
        gt = jnp.dot(h, wg_ref[...], preferred_element_type=F32)
        up = jnp.dot(h, wu_ref[...], preferred_element_type=F32)
        acc_ref[...] += jnp.dot((_silu(gt) * up).astype(BF16), w2_ref[...], preferred_element_type=F32)

    @pl.when(j == pl.num_programs(1) - 1)
    def _():
        o_ref[...] = jnp.where(i < nb_ref[0], acc_ref[...] * wt_ref[...], 0.0)


def _moe_blocks(xb, slot_w, block_e, n_used, w13, w2):
    cap, d = xb.shape
    ne, ff, _ = w2.shape
    tf = 896
    nf = ff // tf
    n_blocks = cap // MOE_BM
    grid_spec = pltpu.PrefetchScalarGridSpec(
        num_scalar_prefetch=2,
        grid=(n_blocks, nf),
        in_specs=[
            pl.BlockSpec((MOE_BM, d), lambda i, j, be, nb: (i, 0)),
            pl.BlockSpec((MOE_BM, 1), lambda i, j, be, nb: (i, 0)),
            pl.BlockSpec((None, d, tf), lambda i, j, be, nb: (be[i], 0, j)),
            pl.BlockSpec((None, d, tf), lambda i, j, be, nb: (be[i], 0, j + nf)),
            pl.BlockSpec((None, tf, d), lambda i, j, be, nb: (be[i], j, 0)),
        ],
        out_specs=pl.BlockSpec((MOE_BM, d), lambda i, j, be, nb: (i, 0)),
        scratch_shapes=[pltpu.VMEM((MOE_BM, d), F32)],
    )
    return pl.pallas_call(
        _moe_kernel,
        out_shape=jax.ShapeDtypeStruct((cap, d), F32),
        grid_spec=grid_spec,
        compiler_params=_cparams(("arbitrary", "arbitrary")),
        name="moe_experts",
    )(block_e, n_used, xb, slot_w, w13, w13, w2)


def _route(logits):
    n = logits.shape[0]
    top_v, top_i = lax.top_k(logits, TOP_K)
    top_w = jax.nn.softmax(top_v, axis=-1)
    e_flat = top_i.reshape(-1).astype(jnp.int32)
    w_flat = top_w.reshape(-1)
    n_slots = n * TOP_K
    onehot = (e_flat[:, None] == jnp.arange(N_EXPERTS, dtype=jnp.int32)[None, :]).astype(jnp.int32)
    rank = jnp.take_along_axis(jnp.cumsum(onehot, axis=0) - onehot, e_flat[:, None], axis=1)[:, 0]
    counts = jnp.sum(onehot, axis=0)
    padded = (counts + MOE_BM - 1) // MOE_BM * MOE_BM
    pad_ends = jnp.cumsum(padded)
    pad_starts = pad_ends - padded
    dest = pad_starts[e_flat] + rank
    n_blocks = -(-(n_slots + N_EXPERTS * (MOE_BM - 1)) // MOE_BM)
    cap = n_blocks * MOE_BM
    tok_flat = jnp.repeat(jnp.arange(n, dtype=jnp.int32), TOP_K)
    buf_tok = jnp.zeros((cap,), jnp.int32).at[dest].set(tok_flat)
    buf_w = jnp.zeros((cap,), F32).at[dest].set(w_flat)
    block_e = jnp.minimum(jnp.searchsorted(pad_ends, jnp.arange(n_blocks, dtype=jnp.int32) * MOE_BM,
                                           side='right'), N_EXPERTS - 1).astype(jnp.int32)
    n_used = (pad_ends[-1] // MOE_BM).astype(jnp.int32).reshape(1)
    return buf_tok, buf_w, block_e, n_used, dest.reshape(n, TOP_K)


def _combine_kernel(x_ref, m_ref, a_ref, b_ref, o_ref):
    o_ref[...] = x_ref[...] + m_ref[5] * (a_ref[...] + b_ref[...])


def _combine(lay, x, mods, ya, yb):
    n, d = x.shape
    tm = _row_tile(lay, 1024)
    row = lambda i: (i, 0)
    return pl.pallas_call(
        _combine_kernel,
        out_shape=jax.ShapeDtypeStruct((n, d), F32),
        grid=(n // tm,),
        in_specs=[pl.BlockSpec((tm, d), row),
                  pl.BlockSpec((None, 6, 1, d), lambda i: (lay.mod_row(i, tm), 0, 0, 0)),
                  pl.BlockSpec((tm, d), row), pl.BlockSpec((tm, d), row)],
        out_specs=pl.BlockSpec((tm, d), row),
        compiler_params=_cparams(("arbitrary",)),
        name="moe_combine",
    )(x, mods, ya, yb)


def _final_kernel(x_ref, g_ref, o_ref):
    x = x_ref[...]
    o_ref[...] = x * lax.rsqrt(jnp.mean(x * x, axis=-1, keepdims=True) + EPS) * g_ref[...]


def _final_norm(lay, x, g):
    n, d = x.shape
    tm = _row_tile(lay, 1024)
    off = lay.n_ctx_rows // tm
    n_lat = lay.b * lay.seq
    return pl.pallas_call(
        _final_kernel,
        out_shape=jax.ShapeDtypeStruct((n_lat, d), F32),
        grid=(n_lat // tm,),
        in_specs=[pl.BlockSpec((tm, d), lambda i: (i + off, 0)), pl.BlockSpec((1, d), lambda i: (0, 0))],
        out_specs=pl.BlockSpec((tm, d), lambda i: (i, 0)),
        compiler_params=_cparams(("arbitrary",)),
        name="final_norm",
    )(x, g.reshape(1, -1))


def _permute_w_in(w):
    d = w.shape[0]
    parts = [w[:, COL_GATE:COL_GATE + N_BRANCH * D_MODEL], w[:, COL_Q:COL_G_END],
             w[:, COL_POOL:COL_POOL + POOL_WIDTH], w[:, COL_Z:COL_Z + SSD_INNER],
             w[:, COL_XBC:COL_XBC + SSD_XBC], w[:, COL_DT:COL_DT + 2 * SSD_HEADS],
             jnp.zeros((d, LANES - 2 * SSD_HEADS), w.dtype)]
    return jnp.concatenate(parts, axis=1).astype(BF16)


def kernel(x, c, ctx, c_ctx, ada_w, ada_b, norm1_g, norm2_g, w_in, ssd_conv_w, ssd_conv_b, ssd_dt_bias,
           ssd_a_log, ssd_d, ssd_norm_g, pool_w, pool_scale, ret_decay_logit, w_branch, w_out,
           ffn_w13, ffn_w2, moe_router, moe_w13, moe_w2, final_norm_g):
    b, seq, d = x.shape
    ctx_len = ctx.shape[1]
    depth = w_in.shape[0]
    lay = _Layout(b, ctx_len, seq)
    n = lay.n

    cvec = jnp.concatenate([c, c_ctx[None, :], jnp.zeros((8 - b - 1, d), F32)], axis=0)
    mods_all = _ada_all(cvec, ada_w, ada_b).reshape(depth, 8, 6, 1, d)
    xa = jnp.concatenate([ctx.reshape(-1, d), x.reshape(-1, d)], axis=0)

    for layer in range(depth):
        mods = mods_all[layer]
        p = _in_proj(lay, xa, mods, norm1_g[layer].reshape(1, -1), _permute_w_in(w_in[layer]))
        ssd_f, ssd_b = _ssd_scan(lay, p, ssd_conv_w[layer], ssd_conv_b[layer], ssd_dt_bias[layer],
                                 ssd_a_log[layer], ssd_d[layer])
        ret_f, ret_b = _ret_scan(lay, p, ret_decay_logit[layer])
        is_moe = layer % 2 == 1
        if is_moe:
            w_r = jnp.pad(moe_router[layer // 2], ((0, 0), (0, LANES - N_EXPERTS)))
        else:
            w_r = jnp.zeros((d, LANES), F32)
        outs = _merge(lay, xa, mods, p, ssd_f, ssd_b, ret_f, ret_b, ssd_norm_g[layer], pool_w[layer],
                      pool_scale[layer], w_branch[layer], w_out[layer], norm2_g[layer], w_r, is_moe)
        if not is_moe:
            (xa,) = outs
            xa = _ffn_dense(lay, xa, mods, norm2_g[layer].reshape(1, -1),
                            ffn_w13[layer // 2].astype(BF16), ffn_w2[layer // 2].astype(BF16))
        else:
            xa, h2, logits = outs
            buf_tok, buf_w, block_e, n_used, dest = _route(logits[:, :N_EXPERTS])
            xb = jnp.take(h2, buf_tok, axis=0)
            yb = _moe_blocks(xb, buf_w[:, None], block_e, n_used,
                             moe_w13[layer // 2].astype(BF16), moe_w2[layer // 2].astype(BF16))
            xa = _combine(lay, xa, mods, jnp.take(yb, dest[:, 0], axis=0), jnp.take(yb, dest[:, 1], axis=0))
    return _final_norm(lay, xa, final_norm_g).reshape(b, seq, d)
```

```python
import functools

import numpy as np
import jax
import jax.numpy as jnp
from jax import lax
from jax.experimental import pallas as pl
from jax.experimental.pallas import tpu as pltpu

F32 = jnp.float32
BF16 = jnp.bfloat16
HIGHEST = lax.Precision.HIGHEST

D_MODEL = 1024
GRID_W = 64
EPS = 1e-6
CHUNK = 128
HALO = 8
SSD_HEADS = 8
SSD_HEAD_DIM = 64
SSD_INNER = 512
SSD_STATE = 64
SSD_GROUPS = 2
SSD_CONV = 5
SSD_XBC = 768
POOL_WINDOWS = (2, 4, 8, 16)
POOL_WIDTH = 512
POOL_GROUP = 128
RET_HEADS = 8
RET_DIM = 64
RET_WIDTH = 512
ROPE_BASE = 10000.0
N_BRANCH = 3
N_EXPERTS = 8
TOP_K = 2
LANES = 128
HEAD_GROUP = 4
GROUP_W = HEAD_GROUP * 64

COL_Z = 0
COL_XBC = 512
COL_DT = 1280
COL_POOL = 1296
COL_Q = 1808
COL_G_END = 3856
COL_GATE = 3856
IN_COLS = 6928
P_GATE = 0
P_Q = 3072
P_K = 3584
P_V = 4096
P_G = 4608
P_POOL = 5120
P_Z = 5632
P_XBC = 6144
P_DT = 6912
P_COLS = 7040

VMEM_LIMIT = 56 * 1024 * 1024


def _sigmoid(v):
    return 0.5 * jnp.tanh(0.5 * v) + 0.5


def _silu(v):
    return v * _sigmoid(v)


def _softplus(v):
    return jnp.maximum(v, 0.0) + jnp.log1p(jnp.exp(-jnp.abs(v)))


def _log_sigmoid(v):
    return -_softplus(-v)


def _cparams(sem):
    return pltpu.CompilerParams(dimension_semantics=sem, vmem_limit_bytes=VMEM_LIMIT)


def _split3(x):
    hi = x.astype(BF16)
    r = x - hi.astype(F32)
    mid = r.astype(BF16)
    lo = (r - mid.astype(F32)).astype(BF16)
    return hi, mid, lo


def _dot_sel_right(x, m):
    return sum(jnp.dot(part, m, preferred_element_type=F32) for part in _split3(x))


def _dot_sel_left(m, x):
    return sum(jnp.dot(m, part, preferred_element_type=F32) for part in _split3(x))


def _block_mask(rows, cols, row_blk, col_blk):
    r = lax.broadcasted_iota(jnp.int32, (rows, cols), 0) // row_blk
    c = lax.broadcasted_iota(jnp.int32, (rows, cols), 1) // col_blk
    return r == c


def _ada_kernel(c_ref, w_ref, b_ref, o_ref):
    cv = c_ref[...]
    o_ref[...] = jnp.dot(_silu(cv), w_ref[...], precision=HIGHEST,
                         preferred_element_type=F32) + b_ref[...]


def _ada_all(cvec, ada_w, ada_b):
    depth, d, n6 = ada_w.shape
    tn = 1536
    return pl.pallas_call(
        _ada_kernel,
        out_shape=jax.ShapeDtypeStruct((depth, 8, n6), F32),
        grid=(depth, n6 // tn),
        in_specs=[
            pl.BlockSpec((8, d), lambda l, j: (0, 0)),
            pl.BlockSpec((None, d, tn), lambda l, j: (l, 0, j)),
            pl.BlockSpec((None, 1, tn), lambda l, j: (l, 0, j)),
        ],
        out_specs=pl.BlockSpec((None, 8, tn), lambda l, j: (l, 0, j)),
        compiler_params=_cparams(("arbitrary", "arbitrary")),
        name="ada_mods",
    )(cvec, ada_w, ada_b.reshape(depth, 1, n6))


class _Layout:
    def __init__(self, b, ctx_len, seq):
        self.b, self.ctx, self.seq = b, ctx_len, seq
        self.n_ctx_rows = b * ctx_len
        self.n = b * (ctx_len + seq)
        self.cch = ctx_len // CHUNK
        self.lch = seq // CHUNK
        self.nch = self.cch + self.lch

    def mod_row(self, tile, tm):
        ctx_tiles = self.n_ctx_rows // tm
        per_b = self.seq // tm
        return jnp.where(tile < ctx_tiles, self.b, (tile - ctx_tiles) // per_b)

    def chunk_block(self, bi, c):
        return jnp.where(c < self.cch, bi * self.cch + c,
                         self.b * self.cch + bi * self.lch + (c - self.cch))

    def fwd_chunk(self, s):
        return s

    def bwd_chunk(self, s):
        return jnp.where(s < self.cch, self.cch - 1 - s, self.nch - 1 - (s - self.cch))


def _row_tile(lay, cap):
    tm = cap
    while lay.n_ctx_rows % tm or lay.seq % tm:
        tm //= 2
    return tm


def _in_kernel(x_ref, m_ref, g_ref, w_ref, o_ref, h_ref):
    @pl.when(pl.program_id(1) == 0)
    def _():
        x = x_ref[...]
        y = x * lax.rsqrt(jnp.mean(x * x, axis=-1, keepdims=True) + EPS) * g_ref[...]
        h_ref[...] = (y * (1.0 + m_ref[1]) + m_ref[0]).astype(BF16)

    o_ref[...] = jnp.dot(h_ref[...], w_ref[...], preferred_element_type=F32)


def _in_proj(lay, x, mods, g, w):
    n, d = x.shape
    tm = _row_tile(lay, 1024)
    tn = 1408
    return pl.pallas_call(
        _in_kernel,
        out_shape=jax.ShapeDtypeStruct((n, P_COLS), F32),
        grid=(n // tm, P_COLS // tn),
        in_specs=[
            pl.BlockSpec((tm, d), lambda i, j: (i, 0)),
            pl.BlockSpec((None, 6, 1, d), lambda i, j: (lay.mod_row(i, tm), 0, 0, 0)),
            pl.BlockSpec((1, d), lambda i, j: (0, 0)),
            pl.BlockSpec((d, tn), lambda i, j: (0, j)),
        ],
        out_specs=pl.BlockSpec((tm, tn), lambda i, j: (i, j)),
        scratch_shapes=[pltpu.VMEM((tm, d), BF16)],
        compiler_params=_cparams(("arbitrary", "arbitrary")),
        name="in_proj",
    )(x, mods, g, w)


def _tri_consts():
    i = np.arange(CHUNK)
    fwd = (i[None, :] <= i[:, None]).astype(np.float32)
    bwd = (i[None, :] >= i[:, None]).astype(np.float32)
    return np.stack([fwd, bwd])


def _expand_consts(heads, width):
    e = np.zeros((2, LANES, heads * width), np.float32)
    for d in range(2):
        for h in range(heads):
            e[d, d * heads + h, h * width:(h + 1) * width] = 1.0
    return e


def _ssd_direction(d, c, lay, xm_ref, xp_ref, xn_ref, dt_ref, tri_ref, exp_ref, cw_ref, cb_ref,
                   dtb_ref, alog_ref, alogx_ref, dskip_ref, st_ref, o_ref, xw_ref):
    is_start = jnp.logical_or(c == 0, c == lay.cch)
    is_end = jnp.logical_or(c == lay.cch - 1, c == lay.nch - 1)
    xw_ref[0:HALO, :] = jnp.where(is_start, 0.0, xp_ref[...])
    xw_ref[HALO:HALO + CHUNK, :] = xm_ref[...]
    xw_ref[HALO + CHUNK:, :] = jnp.where(is_end, 0.0, xn_ref[...])
    acc = cb_ref[...] + cw_ref[0:1, :] * xw_ref[HALO - 2:HALO - 2 + CHUNK, :]
    for k in range(1, SSD_CONV):
        acc = acc + cw_ref[k:k + 1, :] * xw_ref[HALO - 2 + k:HALO - 2 + k + CHUNK, :]
    xbc = _silu(acc)
    xs = xbc[:, :SSD_INNER]
    bm = xbc[:, SSD_INNER:SSD_INNER + LANES]
    cm = xbc[:, SSD_INNER + LANES:]
    bt = bm.T.astype(BF16)
    top = lax.broadcasted_iota(jnp.int32, (CHUNK, LANES), 0) < SSD_STATE
    zero = jnp.zeros_like(bt)
    bt_bd = jnp.concatenate([jnp.where(top, bt, zero), jnp.where(top, zero, bt)], axis=1)
    cb_all = jnp.dot(cm.astype(BF16), bt_bd, preferred_element_type=F32)

    tri = tri_ref[d]
    dt_all = _softplus(dt_ref[...] + dtb_ref[...])
    acs = _dot_sel_left(tri, dt_all * (-jnp.exp(alog_ref[...])))
    acs_t = acs.T
    dt_x = _dot_sel_right(dt_all, exp_ref[d])
    acs_x = _dot_sel_left(tri, dt_x * (-jnp.exp(alogx_ref[d])))
    last = CHUNK - 1 if d == 0 else 0
    tot_x = acs_x[last:last + 1, :]
    u = xs * dt_x
    ud = (u * jnp.exp(tot_x - acs_x)).astype(BF16)
    ub = u.astype(BF16)
    off_x = jnp.exp(acs_x)
    cd_x = jnp.exp(tot_x)

    li = lax.broadcasted_iota(jnp.int32, (CHUNK, CHUNK), 0)
    si = lax.broadcasted_iota(jnp.int32, (CHUNK, CHUNK), 1)
    mask = (si <= li) if d == 0 else (si >= li)
    lane = lax.broadcasted_iota(jnp.int32, (CHUNK, LANES), 1)
    cm_sw = pltpu.roll(cm, SSD_STATE, 1)
    u_mask = _block_mask(HEAD_GROUP * CHUNK, GROUP_W, CHUNK, SSD_HEAD_DIM)
    s_mask = _block_mask(GROUP_W, GROUP_W, SSD_STATE, SSD_HEAD_DIM)
    for g in range(SSD_GROUPS):
        gl = slice(g * GROUP_W, (g + 1) * GROUP_W)
        cb = cb_all[:, g * CHUNK:(g + 1) * CHUNK]
        parts = []
        for hh in range(HEAD_GROUP):
            h = g * HEAD_GROUP + hh
            col = acs_x[:, h * SSD_HEAD_DIM:h * SSD_HEAD_DIM + 1]
            row = acs_t[d * SSD_HEADS + h:d * SSD_HEADS + h + 1, :]
            lm = jnp.exp(jnp.where(mask, col - row, -jnp.inf))
            parts.append((cb * lm).astype(BF16))
        in_g = (lane < SSD_STATE) if g == 0 else (lane >= SSD_STATE)
        c_rep = jnp.where(in_g, cm, cm_sw)
        c_off = jnp.concatenate([c_rep, c_rep], axis=1) * off_x[:, gl]
        parts.append(c_off.astype(BF16))
        lhs = jnp.concatenate(parts, axis=1)
        ub_g = ub[:, gl]
        u_bd = jnp.where(u_mask, jnp.concatenate([ub_g] * HEAD_GROUP, axis=0), jnp.zeros((), BF16))
        st = st_ref[d, g]
        rhs = jnp.concatenate([u_bd, st.astype(BF16)], axis=0)
        y_g = jnp.dot(lhs, rhs, preferred_element_type=F32)
        if d == 0:
            y_g = y_g + dskip_ref[:, gl] * xs[:, gl]
        o_ref[:, gl] = y_g
        bt_g = bt[g * SSD_STATE:(g + 1) * SSD_STATE, :]
        upd = jnp.dot(jnp.concatenate([bt_g] * HEAD_GROUP, axis=0), ud[:, gl], preferred_element_type=F32)
        st_ref[d, g] = st * cd_x[:, gl] + jnp.where(s_mask, upd, 0.0)


def _ssd_kernel(lay, xm_f, xp_f, xn_f, dt_f, xm_b, xp_b, xn_b, dt_b, tri_ref, exp_ref, cw_ref, cb_ref,
                dtb_ref, alog_ref, alogx_ref, dskip_ref, of_ref, ob_ref, st_ref, xw_ref):
    s = pl.program_id(1)

    @pl.when(s == 0)
    def _():
        st_ref[...] = jnp.zeros_like(st_ref)

    _ssd_direction(0, lay.fwd_chunk(s), lay, xm_f, xp_f, xn_f, dt_f, tri_ref, exp_ref, cw_ref, cb_ref,
                   dtb_ref, alog_ref, alogx_ref, dskip_ref, st_ref, of_ref, xw_ref)
    _ssd_direction(1, lay.bwd_chunk(s), lay, xm_b, xp_b, xn_b, dt_b, tri_ref, exp_ref, cw_ref, cb_ref,
                   dtb_ref, alog_ref, alogx_ref, dskip_ref, st_ref, ob_ref, xw_ref)


def _halo_specs(lay, width, col_block, chunk_of):
    n8 = lay.n // HALO
    per = CHUNK // HALO

    def main(bi, s):
        return (lay.chunk_block(bi, chunk_of(s)), col_block)

    def prev(bi, s):
        return (jnp.maximum(lay.chunk_block(bi, chunk_of(s)) * per - 1, 0), col_block)

    def nxt(bi, s):
        return (jnp.minimum(lay.chunk_block(bi, chunk_of(s)) * per + per, n8 - 1), col_block)

    return [pl.BlockSpec((CHUNK, width), main), pl.BlockSpec((HALO, width), prev),
            pl.BlockSpec((HALO, width), nxt)]


def _ssd_scan(lay, p, conv_w, conv_b, dt_bias, a_log, d_skip):
    n = lay.n
    pad = LANES - 2 * SSD_HEADS
    dtb = jnp.pad(dt_bias.reshape(1, -1), ((0, 0), (0, pad)))
    alog = jnp.pad(a_log.reshape(1, -1), ((0, 0), (0, pad)))
    alogx = jnp.repeat(a_log, SSD_HEAD_DIM, axis=-1).reshape(2, 1, SSD_INNER)
    dskip = jnp.repeat(d_skip, SSD_HEAD_DIM).reshape(1, SSD_INNER)
    tri = jnp.asarray(_tri_consts(), BF16)
    expand = jnp.asarray(_expand_consts(SSD_HEADS, SSD_HEAD_DIM), BF16)
    xbc_blk = P_XBC // SSD_XBC
    dt_blk = P_DT // LANES

    def dt_spec(chunk_of):
        return pl.BlockSpec((CHUNK, LANES), lambda bi, s: (lay.chunk_block(bi, chunk_of(s)), dt_blk))

    def out_spec(chunk_of):
        return pl.BlockSpec((CHUNK, SSD_INNER), lambda bi, s: (lay.chunk_block(bi, chunk_of(s)), 0))

    const2 = lambda bi, s: (0, 0)
    const3 = lambda bi, s: (0, 0, 0)
    in_specs = (
        _halo_specs(lay, SSD_XBC, xbc_blk, lay.fwd_chunk) + [dt_spec(lay.fwd_chunk)]
        + _halo_specs(lay, SSD_XBC, xbc_blk, lay.bwd_chunk) + [dt_spec(lay.bwd_chunk)]
        + [pl.BlockSpec((2, CHUNK, CHUNK), const3),
           pl.BlockSpec((2, LANES, SSD_INNER), const3),
           pl.BlockSpec((SSD_CONV, SSD_XBC), const2),
           pl.BlockSpec((1, SSD_XBC), const2),
           pl.BlockSpec((1, LANES), const2),
           pl.BlockSpec((1, LANES), const2),
           pl.BlockSpec((2, 1, SSD_INNER), const3),
           pl.BlockSpec((1, SSD_INNER), const2)])
    return pl.pallas_call(
        functools.partial(_ssd_kernel, lay),
        out_shape=(jax.ShapeDtypeStruct((n, SSD_INNER), F32), jax.ShapeDtypeStruct((n, SSD_INNER), F32)),
        grid=(lay.b, lay.nch),
        in_specs=in_specs,
        out_specs=(out_spec(lay.fwd_chunk), out_spec(lay.bwd_chunk)),
        scratch_shapes=[pltpu.VMEM((2, SSD_GROUPS, GROUP_W, GROUP_W), F32),
                        pltpu.VMEM((CHUNK + 2 * HALO, SSD_XBC), F32)],
        compiler_params=_cparams(("arbitrary", "arbitrary")),
        name="ssd_scan",
    )(p, p, p, p, p, p, p, p, tri, expand, conv_w, conv_b.reshape(1, -1), dtb, alog, alogx, dskip)


def _rope_tables(lay):
    n_axis = RET_DIM // 4
    t = np.arange(lay.seq)
    inv = ROPE_BASE ** (-np.arange(n_axis, dtype=np.float32) / n_axis)
    row = (t // GRID_W).astype(np.float32)
    colp = (t % GRID_W).astype(np.float32)
    ang = jnp.concatenate([jnp.asarray(row)[:, None] * inv, jnp.asarray(colp)[:, None] * inv], axis=-1)
    cos, sin = jnp.cos(ang), jnp.sin(ang)
    cos_l = jnp.concatenate([cos, cos, cos, cos], axis=-1)
    sin_l = jnp.concatenate([-sin, sin, -sin, sin], axis=-1)
    cos_t = jnp.concatenate([jnp.ones((lay.ctx, LANES), F32), cos_l], axis=0)
    sin_t = jnp.concatenate([jnp.zeros((lay.ctx, LANES), F32), sin_l], axis=0)
    return cos_t, sin_t


def _rope(xv, cos, sin):
    lane = lax.broadcasted_iota(jnp.int32, (CHUNK, LANES), 1)
    first_half = (lane % RET_DIM) < (RET_DIM // 2)
    out = []
    for j in range(RET_WIDTH // LANES):
        v = xv[:, j * LANES:(j + 1) * LANES]
        swapped = jnp.where(first_half, pltpu.roll(v, LANES - RET_DIM // 2, 1),
                            pltpu.roll(v, RET_DIM // 2, 1))
        out.append(v * cos + swapped * sin)
    return jnp.concatenate(out, axis=-1)


def _ret_tables(lgx_ref, lgp_ref, kdec_ref, qdec_ref, cd_ref, dmat_ref):
    idx = lax.broadcasted_iota(jnp.int32, (CHUNK, 1), 0).astype(F32)
    ii = lax.broadcasted_iota(jnp.int32, (CHUNK, CHUNK), 0)
    mi = lax.broadcasted_iota(jnp.int32, (CHUNK, CHUNK), 1)
    for d in range(2):
        lg_x = _log_sigmoid(lgx_ref[d])
        lg_p = _log_sigmoid(lgp_ref[d])
        if d == 0:
            k_pow, q_pow, diff = (CHUNK - 1) - idx, idx + 1.0, ii - mi
        else:
            k_pow, q_pow, diff = idx, CHUNK - idx, mi - ii
        kdec_ref[d] = jnp.exp(lg_x * k_pow)
        qdec_ref[d] = jnp.exp(lg_x * q_pow)
        cd_ref[d] = jnp.exp(lg_x * float(CHUNK))
        dpos = jnp.maximum(diff, 0).astype(F32)
        for h in range(RET_HEADS):
            dmat_ref[d, :, h * CHUNK:(h + 1) * CHUNK] = jnp.where(
                diff >= 0, jnp.exp(lg_p[:, h:h + 1] * dpos), 0.0)


def _ret_direction(d, q_ref, k_ref, v_ref, cos_ref, sin_ref, kdec_ref, qdec_ref, cd_ref, dmat_ref,
                   st_ref, o_ref):
    cos, sin = cos_ref[...], sin_ref[...]
    q = _rope(q_ref[...], cos, sin)
    k = _rope(k_ref[...], cos, sin) * (RET_DIM ** -0.5)
    v = v_ref[...]
    vk = (v * kdec_ref[d]).astype(BF16)
    qd = (q * qdec_ref[d]).astype(BF16)
    cd = cd_ref[d]
    qb, vb = q.astype(BF16), v.astype(BF16)
    kt = k.T.astype(BF16)
    k_mask = _block_mask(GROUP_W, HEAD_GROUP * CHUNK, RET_DIM, CHUNK)
    v_mask = _block_mask(HEAD_GROUP * CHUNK, GROUP_W, CHUNK, RET_DIM)
    s_mask = _block_mask(GROUP_W, GROUP_W, RET_DIM, RET_DIM)
    zero = jnp.zeros((), BF16)
    for g in range(RET_HEADS // HEAD_GROUP):
        gl = slice(g * GROUP_W, (g + 1) * GROUP_W)
        sl = slice(g * HEAD_GROUP * CHUNK, (g + 1) * HEAD_GROUP * CHUNK)
        kt_g = kt[gl, :]
        k_bd = jnp.where(k_mask, jnp.concatenate([kt_g] * HEAD_GROUP, axis=1), zero)
        s_all = jnp.dot(qb[:, gl], k_bd, preferred_element_type=F32)
        inner = (s_all * dmat_ref[d, :, sl]).astype(BF16)
        lhs = jnp.concatenate([inner, qd[:, gl]], axis=1)
        v_bd = jnp.where(v_mask, jnp.concatenate([vb[:, gl]] * HEAD_GROUP, axis=0), zero)
        st = st_ref[d, g]
        rhs = jnp.concatenate([v_bd, st.astype(BF16)], axis=0)
        o_ref[:, gl] = jnp.dot(lhs, rhs, preferred_element_type=F32)
        upd = jnp.dot(kt_g, vk[:, gl], preferred_element_type=F32)
        st_ref[d, g] = st * cd[:, gl] + jnp.where(s_mask, upd, 0.0)


def _ret_kernel(qf, kf, vf, cosf, sinf, qb, kb, vb, cosb, sinb, lgx_ref, lgp_ref, of_ref, ob_ref, st_ref,
                kdec_ref, qdec_ref, cd_ref, dmat_ref):
    @pl.when(jnp.logical_and(pl.program_id(0) == 0, pl.program_id(1) == 0))
    def _():
        _ret_tables(lgx_ref, lgp_ref, kdec_ref, qdec_ref, cd_ref, dmat_ref)

    @pl.when(pl.program_id(1) == 0)
    def _():
        st_ref[...] = jnp.zeros_like(st_ref)

    _ret_direction(0, qf, kf, vf, cosf, sinf, kdec_ref, qdec_ref, cd_ref, dmat_ref, st_ref, of_ref)
    _ret_direction(1, qb, kb, vb, cosb, sinb, kdec_ref, qdec_ref, cd_ref, dmat_ref, st_ref, ob_ref)


def _ret_scan(lay, p, decay_logit):
    n = lay.n
    cos_t, sin_t = _rope_tables(lay)
    lgx = jnp.repeat(decay_logit, RET_DIM, axis=-1).reshape(2, 1, RET_WIDTH)
    lgp = jnp.pad(decay_logit, ((0, 0), (0, LANES - RET_HEADS))).reshape(2, 1, LANES)

    def specs(chunk_of):
        def blk(cb):
            return pl.BlockSpec((CHUNK, RET_WIDTH), lambda bi, s: (lay.chunk_block(bi, chunk_of(s)), cb))
        tab = pl.BlockSpec((CHUNK, LANES), lambda bi, s: (chunk_of(s), 0))
        return [blk(P_Q // RET_WIDTH), blk(P_K // RET_WIDTH), blk(P_V // RET_WIDTH), tab, tab]

    def out_spec(chunk_of):
        return pl.BlockSpec((CHUNK, RET_WIDTH), lambda bi, s: (lay.chunk_block(bi, chunk_of(s)), 0))

    const3 = lambda bi, s: (0, 0, 0)
    return pl.pallas_call(
        _ret_kernel,
        out_shape=(jax.ShapeDtypeStruct((n, RET_WIDTH), F32), jax.ShapeDtypeStruct((n, RET_WIDTH), F32)),
        grid=(lay.b, lay.nch),
        in_specs=specs(lay.fwd_chunk) + specs(lay.bwd_chunk)
        + [pl.BlockSpec((2, 1, RET_WIDTH), const3), pl.BlockSpec((2, 1, LANES), const3)],
        out_specs=(out_spec(lay.fwd_chunk), out_spec(lay.bwd_chunk)),
        scratch_shapes=[pltpu.VMEM((2, RET_HEADS // HEAD_GROUP, GROUP_W, GROUP_W), F32),
                        pltpu.VMEM((2, CHUNK, RET_WIDTH), F32),
                        pltpu.VMEM((2, CHUNK, RET_WIDTH), F32),
                        pltpu.VMEM((2, 1, RET_WIDTH), F32),
                        pltpu.VMEM((2, CHUNK, RET_HEADS * CHUNK), F32)],
        compiler_params=_cparams(("arbitrary", "arbitrary")),
        name="ret_scan",
    )(p, p, p, cos_t, sin_t, p, p, p, cos_t, sin_t, lgx, lgp)


def _merge_kernel(lay, tm, emit_h2, x_ref, m_ref, gate_ref, z_ref, g_ref, um_ref, up_ref, un_ref,
                  sf_ref, sb_ref, rf_ref, rb_ref, sng_ref, pw_ref, ps_ref, wb_ref, wo_ref, n2g_ref,
                  wrh_ref, wrl_ref, *rest):
    if emit_h2:
        xo_ref, h2_ref, lg_ref, uw_ref = rest
    else:
        xo_ref, uw_ref = rest
    i = pl.program_id(0)
    ctx_tiles = lay.n_ctx_rows // tm
    per_ctx = lay.ctx // tm
    per_lat = lay.seq // tm
    in_ctx = i < ctx_tiles
    t_in_seg = jnp.where(in_ctx, i % per_ctx, (i - ctx_tiles) % per_lat)
    seg_tiles = jnp.where(in_ctx, per_ctx, per_lat)
    seg_len = jnp.where(in_ctx, lay.ctx, lay.seq)
    pos = t_in_seg * tm + lax.broadcasted_iota(jnp.int32, (tm, 1), 0)

    ys = (sf_ref[...] + sb_ref[...]) * _silu(z_ref[...])
    s_br = ys * lax.rsqrt(jnp.mean(ys * ys, axis=-1, keepdims=True) + EPS) * sng_ref[...]

    uw_ref[0:HALO, :] = jnp.where(t_in_seg == 0, 0.0, up_ref[...])
    uw_ref[HALO:HALO + tm, :] = um_ref[...]
    uw_ref[HALO + tm:, :] = jnp.where(t_in_seg == seg_tiles - 1, 0.0, un_ref[...])
    pooled = []
    for gi, w in enumerate(POOL_WINDOWS):
        left = w // 2
        right = w - 1 - left
        ls = slice(gi * POOL_GROUP, (gi + 1) * POOL_GROUP)
        tot = uw_ref[HALO - left:HALO - left + tm, ls]
        for o in range(-left + 1, right + 1):
            tot = tot + uw_ref[HALO + o:HALO + o + tm, ls]
        cnt = (jnp.minimum(pos + right, seg_len - 1) + 1 - jnp.maximum(pos - left, 0)).astype(F32)
        mixed = tot / cnt - uw_ref[HALO:HALO + tm, ls]
        pooled.append(jnp.dot(mixed.astype(BF16), pw_ref[gi], preferred_element_type=F32))
    p_br = jnp.concatenate(pooled, axis=-1) * ps_ref[...]

    yr = rf_ref[...] + rb_ref[...]
    normed = []
    for h in range(RET_HEADS):
        yh = yr[:, h * RET_DIM:(h + 1) * RET_DIM]
        mu = jnp.mean(yh, axis=-1, keepdims=True)
        dv = yh - mu
        var = jnp.mean(dv * dv, axis=-1, keepdims=True)
        normed.append(dv * lax.rsqrt(var + EPS))
    r_br = jnp.concatenate(normed, axis=-1) * _silu(g_ref[...])

    acc = None
    for bi, br in enumerate((s_br, p_br, r_br)):
        gate = _sigmoid(gate_ref[:, bi * D_MODEL:(bi + 1) * D_MODEL])
        term = gate * jnp.dot(br.astype(BF16), wb_ref[bi], preferred_element_type=F32)
        acc = term if acc is None else acc + term
    mix = jnp.dot(acc.astype(BF16), wo_ref[...], preferred_element_type=F32)
    xn = x_ref[...] + m_ref[2] * mix
    xo_ref[...] = xn
    if emit_h2:
        y = xn * lax.rsqrt(jnp.mean(xn * xn, axis=-1, keepdims=True) + EPS) * n2g_ref[...]
        h2 = y * (1.0 + m_ref[4]) + m_ref[3]
        h2_ref[...] = h2
        hh = h2.astype(BF16)
        hl = (h2 - hh.astype(F32)).astype(BF16)
        wh, wl = wrh_ref[...], wrl_ref[...]
        lg_ref[...] = (jnp.dot(hh, wh, preferred_element_type=F32)
                       + (jnp.dot(hh, wl, preferred_element_type=F32)
                          + jnp.dot(hl, wh, preferred_element_type=F32)))


def _merge(lay, x, mods, p, ssd_f, ssd_b, ret_f, ret_b, ssd_norm_g, pool_w, pool_scale, w_branch, w_out,
           norm2_g, w_router_pad, emit_h2):
    n, d = x.shape
    tm = _row_tile(lay, 256)
    n8 = n // HALO
    per = tm // HALO
    pool_blk = P_POOL // POOL_WIDTH
    wr_hi = w_router_pad.astype(BF16)
    wr_lo = (w_router_pad - wr_hi.astype(F32)).astype(BF16)
    row = lambda i: (i, 0)
    const2 = lambda i: (0, 0)
    const3 = lambda i: (0, 0, 0)
    in_specs = [
        pl.BlockSpec((tm, d), row),
        pl.BlockSpec((None, 6, 1, d), lambda i: (lay.mod_row(i, tm), 0, 0, 0)),
        pl.BlockSpec((tm, N_BRANCH * d), lambda i: (i, P_GATE // (N_BRANCH * d))),
        pl.BlockSpec((tm, SSD_INNER), lambda i: (i, P_Z // SSD_INNER)),
        pl.BlockSpec((tm, RET_WIDTH), lambda i: (i, P_G // RET_WIDTH)),
        pl.BlockSpec((tm, POOL_WIDTH), lambda i: (i, pool_blk)),
        pl.BlockSpec((HALO, POOL_WIDTH), lambda i: (jnp.maximum(i * per - 1, 0), pool_blk)),
        pl.BlockSpec((HALO, POOL_WIDTH), lambda i: (jnp.minimum(i * per + per, n8 - 1), pool_blk)),
        pl.BlockSpec((tm, SSD_INNER), row),
        pl.BlockSpec((tm, SSD_INNER), row),
        pl.BlockSpec((tm, RET_WIDTH), row),
        pl.BlockSpec((tm, RET_WIDTH), row),
        pl.BlockSpec((1, SSD_INNER), const2),
        pl.BlockSpec((len(POOL_WINDOWS), POOL_GROUP, POOL_GROUP), const3),
        pl.BlockSpec((1, POOL_WIDTH), const2),
        pl.BlockSpec((N_BRANCH, SSD_INNER, d), const3),
        pl.BlockSpec((d, d), const2),
        pl.BlockSpec((1, d), const2),
        pl.BlockSpec((d, LANES), const2),
        pl.BlockSpec((d, LANES), const2),
    ]
    out_shape = [jax.ShapeDtypeStruct((n, d), F32)]
    out_specs = [pl.BlockSpec((tm, d), row)]
    if emit_h2:
        out_shape += [jax.ShapeDtypeStruct((n, d), F32), jax.ShapeDtypeStruct((n, LANES), F32)]
        out_specs += [pl.BlockSpec((tm, d), row), pl.BlockSpec((tm, LANES), row)]
    return pl.pallas_call(
        functools.partial(_merge_kernel, lay, tm, emit_h2),
        out_shape=tuple(out_shape),
        grid=(n // tm,),
        in_specs=in_specs,
        out_specs=tuple(out_specs),
        scratch_shapes=[pltpu.VMEM((tm + 2 * HALO, POOL_WIDTH), F32)],
        compiler_params=_cparams(("arbitrary",)),
        name="merge_h2" if emit_h2 else "merge",
    )(x, mods, p, p, p, p, p, p, ssd_f, ssd_b, ret_f, ret_b, ssd_norm_g.reshape(1, -1),
      pool_w.astype(BF16), pool_scale.reshape(1, -1), w_branch.astype(BF16), w_out.astype(BF16),
      norm2_g.reshape(1, -1), wr_hi, wr_lo)


def _ffn_kernel(x_ref, m_ref, g_ref, wg_ref, wu_ref, w2_ref, o_ref, h_ref, acc_ref):
    j = pl.program_id(1)

    @pl.when(j == 0)
    def _():
        x = x_ref[...]
        y = x * lax.rsqrt(jnp.mean(x * x, axis=-1, keepdims=True) + EPS) * g_ref[...]
        h_ref[...] = (y * (1.0 + m_ref[4]) + m_ref[3]).astype(BF16)
        acc_ref[...] = jnp.zeros_like(acc_ref)

    h = h_ref[...]
    gt = jnp.dot(h, wg_ref[...], preferred_element_type=F32)
    up = jnp.dot(h, wu_ref[...], preferred_element_type=F32)
    acc_ref[...] += jnp.dot((_silu(gt) * up).astype(BF16), w2_ref[...], preferred_element_type=F32)

    @pl.when(j == pl.num_programs(1) - 1)
    def _():
        o_ref[...] = x_ref[...] + m_ref[5] * acc_ref[...]


def _ffn_dense(lay, x, mods, g, w13, w2):
    n, d = x.shape
    ff = w2.shape[0]
    tm = _row_tile(lay, 1024)
    tf = 1408
    nf = ff // tf
    return pl.pallas_call(
        _ffn_kernel,
        out_shape=jax.ShapeDtypeStruct((n, d), F32),
        grid=(n // tm, nf),
        in_specs=[
            pl.BlockSpec((tm, d), lambda i, j: (i, 0)),
            pl.BlockSpec((None, 6, 1, d), lambda i, j: (lay.mod_row(i, tm), 0, 0, 0)),
            pl.BlockSpec((1, d), lambda i, j: (0, 0)),
            pl.BlockSpec((d, tf), lambda i, j: (0, j)),
            pl.BlockSpec((d, tf), lambda i, j: (0, j + nf)),
            pl.BlockSpec((tf, d), lambda i, j: (j, 0)),
        ],
        out_specs=pl.BlockSpec((tm, d), lambda i, j: (i, 0)),
        scratch_shapes=[pltpu.VMEM((tm, d), BF16), pltpu.VMEM((tm, d), F32)],
        compiler_params=_cparams(("arbitrary", "arbitrary")),
        name="ffn_dense",
    )(x, mods, g, w13, w13, w2)


ROUTE_TM = 512
R_E0, R_E1, R_RANK0, R_RANK1, R_W0, R_W1 = range(6)


def _route_kernel(lg_ref, tri_ref, o_ref, cnt_ref, run_ref):
    i = pl.program_id(0)

    @pl.when(i == 0)
    def _():
        run_ref[...] = jnp.zeros_like(run_ref)

    tm = lg_ref.shape[0]
    lane = lax.broadcasted_iota(jnp.int32, (tm, LANES), 1).astype(F32)
    lg = jnp.where(lane < N_EXPERTS, lg_ref[...], -jnp.inf)
    m1 = jnp.max(lg, axis=-1, keepdims=True)
    i1 = jnp.min(jnp.where(lg == m1, lane, float(LANES)), axis=-1, keepdims=True)
    l2 = jnp.where(lane == i1, -jnp.inf, lg)
    m2 = jnp.max(l2, axis=-1, keepdims=True)
    i2 = jnp.min(jnp.where(l2 == m2, lane, float(LANES)), axis=-1, keepdims=True)
    e21 = jnp.exp(m2 - m1)
    w1 = 1.0 / (1.0 + e21)
    w2 = e21 * w1
    sel1, sel2 = lane == i1, lane == i2
    memb = jnp.where(sel1, 1.0, jnp.where(sel2, 1.0, 0.0))
    before = jnp.dot(tri_ref[...], memb.astype(BF16), preferred_element_type=F32) + run_ref[...]
    r1 = jnp.sum(jnp.where(sel1, before, 0.0), axis=-1, keepdims=True)
    r2 = jnp.sum(jnp.where(sel2, before, 0.0), axis=-1, keepdims=True)
    rec = jnp.zeros((tm, LANES), F32)
    for k, v in ((R_E0, i1), (R_E1, i2), (R_RANK0, r1), (R_RANK1, r2), (R_W0, w1), (R_W1, w2)):
        rec = jnp.where(lane == k, v, rec)
    o_ref[...] = rec
    run_ref[...] += jnp.sum(memb, axis=0, keepdims=True)
    cnt_ref[...] = jnp.broadcast_to(run_ref[...], cnt_ref.shape)


def _route(logits):
    n = logits.shape[0]
    tm = ROUTE_TM
    i = np.arange(tm)
    tri = jnp.asarray((i[None, :] < i[:, None]).astype(np.float32), BF16)
    return pl.pallas_call(
        _route_kernel,
        out_shape=(jax.ShapeDtypeStruct((n, LANES), F32), jax.ShapeDtypeStruct((8, LANES), F32)),
        grid=(n // tm,),
        in_specs=[pl.BlockSpec((tm, LANES), lambda i: (i, 0)), pl.BlockSpec((tm, tm), lambda i: (0, 0))],
        out_specs=(pl.BlockSpec((tm, LANES), lambda i: (i, 0)), pl.BlockSpec((8, LANES), lambda i: (0, 0))),
        scratch_shapes=[pltpu.VMEM((1, LANES), F32)],
        compiler_params=_cparams(("arbitrary",)),
        name="moe_route",
    )(logits, tri)


MOE_BM = 1024


def _slot_plan(rec, counts):
    n = rec.shape[0]
    cnt = counts[0, :N_EXPERTS].astype(jnp.int32)
    padded = (cnt + MOE_BM - 1) // MOE_BM * MOE_BM
    pad_ends = jnp.cumsum(padded)
    pad_starts = pad_ends - padded
    n_blocks = -(-(n * TOP_K + N_EXPERTS * (MOE_BM - 1)) // MOE_BM)
    n_used = pad_ends[-1] // MOE_BM
    blk = jnp.minimum(jnp.arange(n_blocks, dtype=jnp.int32), n_used - 1)
    block_e = jnp.minimum(jnp.searchsorted(pad_ends, blk * MOE_BM, side='right'), N_EXPERTS - 1)
    e = rec[:, R_E0:R_E1 + 1].astype(jnp.int32)
    rank = rec[:, R_RANK0:R_RANK1 + 1].astype(jnp.int32)
    start = jnp.zeros_like(e)
    for k in range(N_EXPERTS):
        start = jnp.where(e == k, pad_starts[k], start)
    dest = start + rank
    last_blk = jnp.where(cnt > 0, pad_ends // MOE_BM - 1, -1).astype(jnp.int32)
    return dest, block_e.astype(jnp.int32), n_used.astype(jnp.int32).reshape(1), last_blk, n_blocks


DISPATCH_TM = 512


def _dispatch_kernel(last_ref, nb_ref, dest_ref, h_ref, xb_ref, zero_ref, sem, zsem):
    i = pl.program_id(0)
    tm = DISPATCH_TM
    n_blocks = xb_ref.shape[0] // MOE_BM

    def clear_block(blk):
        start = pl.multiple_of(blk * MOE_BM, MOE_BM)
        cp = pltpu.make_async_copy(zero_ref, xb_ref.at[pl.ds(start, MOE_BM), :], zsem)
        cp.start()
        cp.wait()

    @pl.when(i == 0)
    def _():
        zero_ref[...] = jnp.zeros_like(zero_ref)
        for e in range(N_EXPERTS):
            @pl.when(last_ref[e] >= 0)
            def _():
                clear_block(last_ref[e])

            @pl.when(nb_ref[0] + e < n_blocks)
            def _():
                clear_block(nb_ref[0] + e)

    def row_copy(r, k):
        return pltpu.make_async_copy(h_ref.at[pl.ds(i * tm + r, 1), :],
                                     xb_ref.at[pl.ds(dest_ref[0, 2 * r + k], 1), :], sem)

    def issue(r, carry):
        row_copy(r, 0).start()
        row_copy(r, 1).start()
        return carry

    def drain(r, carry):
        row_copy(r, 0).wait()
        row_copy(r, 1).wait()
        return carry

    lax.fori_loop(0, tm, issue, 0)
    lax.fori_loop(0, tm, drain, 0)


def _dispatch(h2, dest, last_blk, n_used, n_blocks):
    n, d = h2.shape
    tm = DISPATCH_TM
    cap = n_blocks * MOE_BM
    grid_spec = pltpu.PrefetchScalarGridSpec(
        num_scalar_prefetch=2,
        grid=(n // tm,),
        in_specs=[pl.BlockSpec((None, 1, TOP_K * tm), lambda i, lb, nb: (i, 0, 0), memory_space=pltpu.SMEM),
                  pl.BlockSpec(memory_space=pl.ANY)],
        out_specs=pl.BlockSpec(memory_space=pl.ANY),
        scratch_shapes=[pltpu.VMEM((MOE_BM, d), F32), pltpu.SemaphoreType.DMA, pltpu.SemaphoreType.DMA],
    )
    return pl.pallas_call(
        _dispatch_kernel,
        out_shape=jax.ShapeDtypeStruct((cap, d), F32),
        grid_spec=grid_spec,
        compiler_params=_cparams(("arbitrary",)),
        name="moe_dispatch",
    )(last_blk, n_used, dest.reshape(n // tm, 1, TOP_K * tm), h2)


def _moe_kernel(be_ref, nb_ref, x_ref, wg_ref, wu_ref, w2_ref, o_ref, h_ref, acc_ref):
    i, j = pl.program_id(0), pl.program_id(1)

    @pl.when(i < nb_ref[0])
    def _():
        @pl.when(j == 0)
        def _():
            h_ref[...] = x_ref[...].astype(BF16)
            acc_ref[...] = jnp.zeros_like(acc_ref)

        h = h_ref[...]
        gt = jnp.dot(h, wg_ref[...].astype(BF16), preferred_element_type=F32)
        up = jnp.dot(h, wu_ref[...].astype(BF16), preferred_element_type=F32)
        acc_ref[...] += jnp.dot((_silu(gt) * up).astype(BF16), w2_ref[...].astype(BF16),
                                preferred_element_type=F32)

        @pl.when(j == pl.num_programs(1) - 1)
        def _():
            o_ref[...] = acc_ref[...]

    @pl.when(jnp.logical_and(i >= nb_ref[0], j == pl.num_programs(1) - 1))
    def _():
        o_ref[...] = jnp.zeros_like(o_ref)


def _moe_blocks(xb, block_e, n_used, w13, w2):
    cap, d = xb.shape
    ne, ff, _ = w2.shape
    tf = 512
    nf = ff // tf
    n_blocks = cap // MOE_BM

    def row(i, j, be, nb):
        return (jnp.minimum(i, nb[0] - 1), 0)

    def jj(i, j, nb):
        return jnp.where(i < nb[0], j, nf - 1)

    grid_spec = pltpu.PrefetchScalarGridSpec(
        num_scalar_prefetch=2,
        grid=(n_blocks, nf),
        in_specs=[
            pl.BlockSpec((MOE_BM, d), row),
            pl.BlockSpec((None, d, tf), lambda i, j, be, nb: (be[i], 0, jj(i, j, nb))),
            pl.BlockSpec((None, d, tf), lambda i, j, be, nb: (be[i], 0, jj(i, j, nb) + nf)),
            pl.BlockSpec((None, tf, d), lambda i, j, be, nb: (be[i], jj(i, j, nb), 0)),
        ],
        out_specs=pl.BlockSpec((MOE_BM, d), lambda i, j, be, nb: (i, 0)),
        scratch_shapes=[pltpu.VMEM((MOE_BM, d), BF16), pltpu.VMEM((MOE_BM, d), F32)],
    )
    return pl.pallas_call(
        _moe_kernel,
        out_shape=jax.ShapeDtypeStruct((cap, d), F32),
        grid_spec=grid_spec,
        compiler_params=_cparams(("arbitrary", "arbitrary")),
        name="moe_experts",
    )(block_e, n_used, xb, w13, w13, w2)


COMBINE_TM = 256


def _combine_kernel(dest_ref, x_ref, m_ref, rec_ref, yb_ref, o_ref, buf_ref, sem):
    tm = COMBINE_TM

    def row_copy(r, k):
        return pltpu.make_async_copy(yb_ref.at[pl.ds(dest_ref[0, 2 * r + k], 1), :],
                                     buf_ref.at[k, pl.ds(r, 1), :], sem)

    def issue(r, carry):
        row_copy(r, 0).start()
        row_copy(r, 1).start()
        return carry

    def drain(r, carry):
        row_copy(r, 0).wait()
        row_copy(r, 1).wait()
        return carry

    lax.fori_loop(0, tm, issue, 0)
    lax.fori_loop(0, tm, drain, 0)
    rec = rec_ref[...]
    y = rec[:, R_W0:R_W0 + 1] * buf_ref[0] + rec[:, R_W1:R_W1 + 1] * buf_ref[1]
    o_ref[...] = x_ref[...] + m_ref[5] * y


def _combine(lay, x, mods, rec, dest, yb):
    n, d = x.shape
    tm = COMBINE_TM
    row = lambda i: (i, 0)
    return pl.pallas_call(
        _combine_kernel,
        out_shape=jax.ShapeDtypeStruct((n, d), F32),
        grid=(n // tm,),
        in_specs=[pl.BlockSpec((None, 1, TOP_K * tm), lambda i: (i, 0, 0), memory_space=pltpu.SMEM),
                  pl.BlockSpec((tm, d), row),
                  pl.BlockSpec((None, 6, 1, d), lambda i: (lay.mod_row(i, tm), 0, 0, 0)),
                  pl.BlockSpec((tm, LANES), row),
                  pl.BlockSpec(memory_space=pl.ANY)],
        out_specs=pl.BlockSpec((tm, d), row),
        scratch_shapes=[pltpu.VMEM((TOP_K, tm, d), F32), pltpu.SemaphoreType.DMA],
        compiler_params=_cparams(("arbitrary",)),
        name="moe_combine",
    )(dest.reshape(n // tm, 1, TOP_K * tm), x, mods, rec, yb)


def _final_kernel(x_ref, g_ref, o_ref):
    x = x_ref[...]
    o_ref[...] = x * lax.rsqrt(jnp.mean(x * x, axis=-1, keepdims=True) + EPS) * g_ref[...]


def _final_norm(lay, x, g):
    n, d = x.shape
    tm = _row_tile(lay, 1024)
    off = lay.n_ctx_rows // tm
    n_lat = lay.b * lay.seq
    return pl.pallas_call(
        _final_kernel,
        out_shape=jax.ShapeDtypeStruct((n_lat, d), F32),
        grid=(n_lat // tm,),
        in_specs=[pl.BlockSpec((tm, d), lambda i: (i + off, 0)), pl.BlockSpec((1, d), lambda i: (0, 0))],
        out_specs=pl.BlockSpec((tm, d), lambda i: (i, 0)),
        compiler_params=_cparams(("arbitrary",)),
        name="final_norm",
    )(x, g.reshape(1, -1))


def _permute_w_in(w):
    d = w.shape[0]
    parts = [w[:, COL_GATE:COL_GATE + N_BRANCH * D_MODEL], w[:, COL_Q:COL_G_END],
             w[:, COL_POOL:COL_POOL + POOL_WIDTH], w[:, COL_Z:COL_Z + SSD_INNER],
             w[:, COL_XBC:COL_XBC + SSD_XBC], w[:, COL_DT:COL_DT + 2 * SSD_HEADS],
             jnp.zeros((d, LANES - 2 * SSD_HEADS), w.dtype)]
    return jnp.concatenate(parts, axis=1).astype(BF16)


def kernel(x, c, ctx, c_ctx, ada_w, ada_b, norm1_g, norm2_g, w_in, ssd_conv_w, ssd_conv_b, ssd_dt_bias,
           ssd_a_log, ssd_d, ssd_norm_g, pool_w, pool_scale, ret_decay_logit, w_branch, w_out,
           ffn_w13, ffn_w2, moe_router, moe_w13, moe_w2, final_norm_g):
    b, seq, d = x.shape
    ctx_len = ctx.shape[1]
    depth = w_in.shape[0]
    lay = _Layout(b, ctx_len, seq)

    cvec = jnp.concatenate([c, c_ctx[None, :], jnp.zeros((8 - b - 1, d), F32)], axis=0)
    mods_all = _ada_all(cvec, ada_w, ada_b).reshape(depth, 8, 6, 1, d)
    xa = jnp.concatenate([ctx.reshape(-1, d), x.reshape(-1, d)], axis=0)

    for layer in range(depth):
        mods = mods_all[layer]
        p = _in_proj(lay, xa, mods, norm1_g[layer].reshape(1, -1), _permute_w_in(w_in[layer]))
        ssd_f, ssd_b = _ssd_scan(lay, p, ssd_conv_w[layer], ssd_conv_b[layer], ssd_dt_bias[layer],
                                 ssd_a_log[layer], ssd_d[layer])
        ret_f, ret_b = _ret_scan(lay, p, ret_decay_logit[layer])
        is_moe = layer % 2 == 1
        if is_moe:
            w_r = jnp.pad(moe_router[layer // 2], ((0, 0), (0, LANES - N_EXPERTS)))
        else:
            w_r = jnp.zeros((d, LANES), F32)
        outs = _merge(lay, xa, mods, p, ssd_f, ssd_b, ret_f, ret_b, ssd_norm_g[layer], pool_w[layer],
                      pool_scale[layer], w_branch[layer], w_out[layer], norm2_g[layer], w_r, is_moe)
        if not is_moe:
            (xa,) = outs
            xa = _ffn_dense(lay, xa, mods, norm2_g[layer].reshape(1, -1),
                            ffn_w13[layer // 2].astype(BF16), ffn_w2[layer // 2].astype(BF16))
        else:
            xa, h2, logits = outs
            rec, counts = _route(logits)
            dest, block_e, n_used, last_blk, n_blocks = _slot_plan(rec, counts)
            xb = _dispatch(h2, dest, last_blk, n_used, n_blocks)
            yb = _moe_blocks(xb, block_e, n_used, moe_w13[layer // 2], moe_w2[layer // 2])
            xa = _combine(lay, xa, mods, rec, dest, yb)
    return _final_norm(lay, xa, final_norm_g).reshape(b, seq, d)
```

```python
import functools

import numpy as np
import jax
import jax.numpy as jnp
from jax import lax
from jax.experimental import pallas as pl
from jax.experimental.pallas import tpu as pltpu

F32 = jnp.float32
BF16 = jnp.bfloat16
HIGHEST = lax.Precision.HIGHEST

D_MODEL = 1024
GRID_W = 64
EPS = 1e-6
CHUNK = 128
HALO = 8
SSD_HEADS = 8
SSD_HEAD_DIM = 64
SSD_INNER = 512
SSD_STATE = 64
SSD_GROUPS = 2
SSD_CONV = 5
SSD_XBC = 768
POOL_WINDOWS = (2, 4, 8, 16)
POOL_WIDTH = 512
POOL_GROUP = 128
RET_HEADS = 8
RET_DIM = 64
RET_WIDTH = 512
ROPE_BASE = 10000.0
N_BRANCH = 3
N_EXPERTS = 8
TOP_K = 2
LANES = 128
HEAD_GROUP = 4
GROUP_W = HEAD_GROUP * 64

COL_Z = 0
COL_XBC = 512
COL_DT = 1280
COL_POOL = 1296
COL_Q = 1808
COL_G_END = 3856
COL_GATE = 3856
IN_COLS = 6928
P_GATE = 0
P_Q = 3072
P_K = 3584
P_V = 4096
P_G = 4608
P_POOL = 5120
P_Z = 5632
P_XBC = 6144
P_DT = 6912
P_COLS = 7040

VMEM_LIMIT = 56 * 1024 * 1024


def _sigmoid(v):
    return 0.5 * jnp.tanh(0.5 * v) + 0.5


def _silu(v):
    return v * _sigmoid(v)


def _softplus(v):
    return jnp.maximum(v, 0.0) + jnp.log1p(jnp.exp(-jnp.abs(v)))


def _log_sigmoid(v):
    return -_softplus(-v)


def _cparams(sem):
    return pltpu.CompilerParams(dimension_semantics=sem, vmem_limit_bytes=VMEM_LIMIT)


def _split3(x):
    hi = x.astype(BF16)
    r = x - hi.astype(F32)
    mid = r.astype(BF16)
    lo = (r - mid.astype(F32)).astype(BF16)
    return hi, mid, lo


def _dot_sel_right(x, m):
    return sum(jnp.dot(part, m, preferred_element_type=F32) for part in _split3(x))


def _dot_sel_left(m, x):
    return sum(jnp.dot(m, part, preferred_element_type=F32) for part in _split3(x))


def _block_mask(rows, cols, row_blk, col_blk):
    r = lax.broadcasted_iota(jnp.int32, (rows, cols), 0) // row_blk
    c = lax.broadcasted_iota(jnp.int32, (rows, cols), 1) // col_blk
    return r == c


def _ada_kernel(c_ref, w_ref, b_ref, o_ref):
    cv = c_ref[...]
    o_ref[...] = jnp.dot(_silu(cv), w_ref[...], precision=HIGHEST,
                         preferred_element_type=F32) + b_ref[...]


def _ada_all(cvec, ada_w, ada_b):
    depth, d, n6 = ada_w.shape
    tn = 1536
    return pl.pallas_call(
        _ada_kernel,
        out_shape=jax.ShapeDtypeStruct((depth, 8, n6), F32),
        grid=(depth, n6 // tn),
        in_specs=[
            pl.BlockSpec((8, d), lambda l, j: (0, 0)),
            pl.BlockSpec((None, d, tn), lambda l, j: (l, 0, j)),
            pl.BlockSpec((None, 1, tn), lambda l, j: (l, 0, j)),
        ],
        out_specs=pl.BlockSpec((None, 8, tn), lambda l, j: (l, 0, j)),
        compiler_params=_cparams(("arbitrary", "arbitrary")),
        name="ada_mods",
    )(cvec, ada_w, ada_b.reshape(depth, 1, n6))


class _Layout:
    def __init__(self, b, ctx_len, seq):
        self.b, self.ctx, self.seq = b, ctx_len, seq
        self.n_ctx_rows = b * ctx_len
        self.n = b * (ctx_len + seq)
        self.cch = ctx_len // CHUNK
        self.lch = seq // CHUNK
        self.nch = self.cch + self.lch

    def mod_row(self, tile, tm):
        ctx_tiles = self.n_ctx_rows // tm
        per_b = self.seq // tm
        return jnp.where(tile < ctx_tiles, self.b, (tile - ctx_tiles) // per_b)

    def chunk_block(self, bi, c):
        return jnp.where(c < self.cch, bi * self.cch + c,
                         self.b * self.cch + bi * self.lch + (c - self.cch))

    def fwd_chunk(self, s):
        return s

    def bwd_chunk(self, s):
        return jnp.where(s < self.cch, self.cch - 1 - s, self.nch - 1 - (s - self.cch))


def _row_tile(lay, cap):
    tm = cap
    while lay.n_ctx_rows % tm or lay.seq % tm:
        tm //= 2
    return tm


def _in_kernel(x_ref, m_ref, g_ref, w_ref, o_ref, h_ref):
    @pl.when(pl.program_id(1) == 0)
    def _():
        x = x_ref[...]
        y = x * lax.rsqrt(jnp.mean(x * x, axis=-1, keepdims=True) + EPS) * g_ref[...]
        h_ref[...] = (y * (1.0 + m_ref[1]) + m_ref[0]).astype(BF16)

    o_ref[...] = jnp.dot(h_ref[...], w_ref[...], preferred_element_type=F32)


def _in_proj(lay, x, mods, g, w):
    n, d = x.shape
    tm = _row_tile(lay, 1024)
    tn = 1408
    return pl.pallas_call(
        _in_kernel,
        out_shape=jax.ShapeDtypeStruct((n, P_COLS), F32),
        grid=(n // tm, P_COLS // tn),
        in_specs=[
            pl.BlockSpec((tm, d), lambda i, j: (i, 0)),
            pl.BlockSpec((None, 6, 1, d), lambda i, j: (lay.mod_row(i, tm), 0, 0, 0)),
            pl.BlockSpec((1, d), lambda i, j: (0, 0)),
            pl.BlockSpec((d, tn), lambda i, j: (0, j)),
        ],
        out_specs=pl.BlockSpec((tm, tn), lambda i, j: (i, j)),
        scratch_shapes=[pltpu.VMEM((tm, d), BF16)],
        compiler_params=_cparams(("arbitrary", "arbitrary")),
        name="in_proj",
    )(x, mods, g, w)


def _tri_consts():
    i = np.arange(CHUNK)
    fwd = (i[None, :] <= i[:, None]).astype(np.float32)
    bwd = (i[None, :] >= i[:, None]).astype(np.float32)
    return np.stack([fwd, bwd])


def _expand_consts(heads, width):
    e = np.zeros((2, LANES, heads * width), np.float32)
    for d in range(2):
        for h in range(heads):
            e[d, d * heads + h, h * width:(h + 1) * width] = 1.0
    return e


def _ssd_direction(d, c, lay, xm_ref, xp_ref, xn_ref, dt_ref, tri_ref, exp_ref, cw_ref, cb_ref,
                   dtb_ref, alog_ref, dskip_ref, st_ref, o_ref, xw_ref):
    is_start = jnp.logical_or(c == 0, c == lay.cch)
    is_end = jnp.logical_or(c == lay.cch - 1, c == lay.nch - 1)
    xw_ref[0:HALO, :] = jnp.where(is_start, 0.0, xp_ref[...])
    xw_ref[HALO:HALO + CHUNK, :] = xm_ref[...]
    xw_ref[HALO + CHUNK:, :] = jnp.where(is_end, 0.0, xn_ref[...])
    acc = cb_ref[...] + cw_ref[0:1, :] * xw_ref[HALO - 2:HALO - 2 + CHUNK, :]
    for k in range(1, SSD_CONV):
        acc = acc + cw_ref[k:k + 1, :] * xw_ref[HALO - 2 + k:HALO - 2 + k + CHUNK, :]
    xbc = _silu(acc)
    xs = xbc[:, :SSD_INNER]
    bm = xbc[:, SSD_INNER:SSD_INNER + LANES]
    cm = xbc[:, SSD_INNER + LANES:]
    bt = bm.T.astype(BF16)
    top = lax.broadcasted_iota(jnp.int32, (CHUNK, LANES), 0) < SSD_STATE
    zero = jnp.zeros_like(bt)
    bt_bd = jnp.concatenate([jnp.where(top, bt, zero), jnp.where(top, zero, bt)], axis=1)
    cb_all = jnp.dot(cm.astype(BF16), bt_bd, preferred_element_type=F32)

    tri = tri_ref[d]
    dt_all = _softplus(dt_ref[...] + dtb_ref[...])
    acs = _dot_sel_left(tri, dt_all * (-jnp.exp(alog_ref[...])))
    acs_t = acs.T
    dt_x = _dot_sel_right(dt_all, exp_ref[d])
    acs_x = _dot_sel_right(acs, exp_ref[d])
    last = CHUNK - 1 if d == 0 else 0
    tot_x = acs_x[last:last + 1, :]
    u = xs * dt_x
    ud = (u * jnp.exp(tot_x - acs_x)).astype(BF16)
    ub = u.astype(BF16)
    off_x = jnp.exp(acs_x)
    cd_x = jnp.exp(tot_x)

    li = lax.broadcasted_iota(jnp.int32, (CHUNK, CHUNK), 0)
    si = lax.broadcasted_iota(jnp.int32, (CHUNK, CHUNK), 1)
    mask = (si <= li) if d == 0 else (si >= li)
    lane = lax.broadcasted_iota(jnp.int32, (CHUNK, LANES), 1)
    cm_sw = pltpu.roll(cm, SSD_STATE, 1)
    u_mask = _block_mask(HEAD_GROUP * CHUNK, GROUP_W, CHUNK, SSD_HEAD_DIM)
    s_mask = _block_mask(GROUP_W, GROUP_W, SSD_STATE, SSD_HEAD_DIM)
    for g in range(SSD_GROUPS):
        gl = slice(g * GROUP_W, (g + 1) * GROUP_W)
        cb = cb_all[:, g * CHUNK:(g + 1) * CHUNK]
        parts = []
        for hh in range(HEAD_GROUP):
            h = g * HEAD_GROUP + hh
            col = acs_x[:, h * SSD_HEAD_DIM:h * SSD_HEAD_DIM + 1]
            row = acs_t[d * SSD_HEADS + h:d * SSD_HEADS + h + 1, :]
            lm = jnp.exp(jnp.where(mask, col - row, -jnp.inf))
            parts.append((cb * lm).astype(BF16))
        in_g = (lane < SSD_STATE) if g == 0 else (lane >= SSD_STATE)
        c_rep = jnp.where(in_g, cm, cm_sw)
        c_off = jnp.concatenate([c_rep, c_rep], axis=1) * off_x[:, gl]
        parts.append(c_off.astype(BF16))
        lhs = jnp.concatenate(parts, axis=1)
        ub_g = ub[:, gl]
        u_bd = jnp.where(u_mask, jnp.concatenate([ub_g] * HEAD_GROUP, axis=0), jnp.zeros((), BF16))
        st = st_ref[d, g]
        rhs = jnp.concatenate([u_bd, st.astype(BF16)], axis=0)
        y_g = jnp.dot(lhs, rhs, preferred_element_type=F32)
        if d == 0:
            y_g = y_g + dskip_ref[:, gl] * xs[:, gl]
        o_ref[:, gl] = y_g
        bt_g = bt[g * SSD_STATE:(g + 1) * SSD_STATE, :]
        upd = jnp.dot(jnp.concatenate([bt_g] * HEAD_GROUP, axis=0), ud[:, gl], preferred_element_type=F32)
        st_ref[d, g] = st * cd_x[:, gl] + jnp.where(s_mask, upd, 0.0)


def _ssd_kernel(lay, xm_f, xp_f, xn_f, dt_f, xm_b, xp_b, xn_b, dt_b, tri_ref, exp_ref, cw_ref, cb_ref,
                dtb_ref, alog_ref, dskip_ref, of_ref, ob_ref, st_ref, xw_ref):
    s = pl.program_id(1)

    @pl.when(s == 0)
    def _():
        st_ref[...] = jnp.zeros_like(st_ref)

    _ssd_direction(0, lay.fwd_chunk(s), lay, xm_f, xp_f, xn_f, dt_f, tri_ref, exp_ref, cw_ref, cb_ref,
                   dtb_ref, alog_ref, dskip_ref, st_ref, of_ref, xw_ref)
    _ssd_direction(1, lay.bwd_chunk(s), lay, xm_b, xp_b, xn_b, dt_b, tri_ref, exp_ref, cw_ref, cb_ref,
                   dtb_ref, alog_ref, dskip_ref, st_ref, ob_ref, xw_ref)


def _halo_specs(lay, width, col_block, chunk_of):
    n8 = lay.n // HALO
    per = CHUNK // HALO

    def main(bi, s):
        return (lay.chunk_block(bi, chunk_of(s)), col_block)

    def prev(bi, s):
        return (jnp.maximum(lay.chunk_block(bi, chunk_of(s)) * per - 1, 0), col_block)

    def nxt(bi, s):
        return (jnp.minimum(lay.chunk_block(bi, chunk_of(s)) * per + per, n8 - 1), col_block)

    return [pl.BlockSpec((CHUNK, width), main), pl.BlockSpec((HALO, width), prev),
            pl.BlockSpec((HALO, width), nxt)]


def _ssd_scan(lay, p, conv_w, conv_b, dt_bias, a_log, d_skip):
    n = lay.n
    pad = LANES - 2 * SSD_HEADS
    dtb = jnp.pad(dt_bias.reshape(1, -1), ((0, 0), (0, pad)))
    alog = jnp.pad(a_log.reshape(1, -1), ((0, 0), (0, pad)))
    dskip = jnp.repeat(d_skip, SSD_HEAD_DIM).reshape(1, SSD_INNER)
    tri = jnp.asarray(_tri_consts(), BF16)
    expand = jnp.asarray(_expand_consts(SSD_HEADS, SSD_HEAD_DIM), BF16)
    xbc_blk = P_XBC // SSD_XBC
    dt_blk = P_DT // LANES

    def dt_spec(chunk_of):
        return pl.BlockSpec((CHUNK, LANES), lambda bi, s: (lay.chunk_block(bi, chunk_of(s)), dt_blk))

    def out_spec(chunk_of):
        return pl.BlockSpec((CHUNK, SSD_INNER), lambda bi, s: (lay.chunk_block(bi, chunk_of(s)), 0))

    const2 = lambda bi, s: (0, 0)
    const3 = lambda bi, s: (0, 0, 0)
    in_specs = (
        _halo_specs(lay, SSD_XBC, xbc_blk, lay.fwd_chunk) + [dt_spec(lay.fwd_chunk)]
        + _halo_specs(lay, SSD_XBC, xbc_blk, lay.bwd_chunk) + [dt_spec(lay.bwd_chunk)]
        + [pl.BlockSpec((2, CHUNK, CHUNK), const3),
           pl.BlockSpec((2, LANES, SSD_INNER), const3),
           pl.BlockSpec((SSD_CONV, SSD_XBC), const2),
           pl.BlockSpec((1, SSD_XBC), const2),
           pl.BlockSpec((1, LANES), const2),
           pl.BlockSpec((1, LANES), const2),
           pl.BlockSpec((1, SSD_INNER), const2)])
    return pl.pallas_call(
        functools.partial(_ssd_kernel, lay),
        out_shape=(jax.ShapeDtypeStruct((n, SSD_INNER), F32), jax.ShapeDtypeStruct((n, SSD_INNER), F32)),
        grid=(lay.b, lay.nch),
        in_specs=in_specs,
        out_specs=(out_spec(lay.fwd_chunk), out_spec(lay.bwd_chunk)),
        scratch_shapes=[pltpu.VMEM((2, SSD_GROUPS, GROUP_W, GROUP_W), F32),
                        pltpu.VMEM((CHUNK + 2 * HALO, SSD_XBC), F32)],
        compiler_params=_cparams(("arbitrary", "arbitrary")),
        name="ssd_scan",
    )(p, p, p, p, p, p, p, p, tri, expand, conv_w, conv_b.reshape(1, -1), dtb, alog, dskip)


def _rope_tables(lay):
    n_axis = RET_DIM // 4
    t = np.arange(lay.seq)
    inv = ROPE_BASE ** (-np.arange(n_axis, dtype=np.float32) / n_axis)
    row = (t // GRID_W).astype(np.float32)
    colp = (t % GRID_W).astype(np.float32)
    ang = jnp.concatenate([jnp.asarray(row)[:, None] * inv, jnp.asarray(colp)[:, None] * inv], axis=-1)
    cos, sin = jnp.cos(ang), jnp.sin(ang)
    cos_l = jnp.concatenate([cos, cos, cos, cos], axis=-1)
    sin_l = jnp.concatenate([-sin, sin, -sin, sin], axis=-1)
    cos_t = jnp.concatenate([jnp.ones((lay.ctx, LANES), F32), cos_l], axis=0)
    sin_t = jnp.concatenate([jnp.zeros((lay.ctx, LANES), F32), sin_l], axis=0)
    return cos_t, sin_t


def _rope(xv, cos, sin):
    lane = lax.broadcasted_iota(jnp.int32, (CHUNK, LANES), 1)
    first_half = (lane % RET_DIM) < (RET_DIM // 2)
    out = []
    for j in range(RET_WIDTH // LANES):
        v = xv[:, j * LANES:(j + 1) * LANES]
        swapped = jnp.where(first_half, pltpu.roll(v, LANES - RET_DIM // 2, 1),
                            pltpu.roll(v, RET_DIM // 2, 1))
        out.append(v * cos + swapped * sin)
    return jnp.concatenate(out, axis=-1)


def _ret_tables(lgx_ref, lgp_ref, kdec_ref, qdec_ref, cd_ref, dmat_ref):
    idx = lax.broadcasted_iota(jnp.int32, (CHUNK, 1), 0).astype(F32)
    ii = lax.broadcasted_iota(jnp.int32, (CHUNK, CHUNK), 0)
    mi = lax.broadcasted_iota(jnp.int32, (CHUNK, CHUNK), 1)
    for d in range(2):
        lg_x = _log_sigmoid(lgx_ref[d])
        lg_p = _log_sigmoid(lgp_ref[d])
        if d == 0:
            k_pow, q_pow, diff = (CHUNK - 1) - idx, idx + 1.0, ii - mi
        else:
            k_pow, q_pow, diff = idx, CHUNK - idx, mi - ii
        kdec_ref[d] = jnp.exp(lg_x * k_pow)
        qdec_ref[d] = jnp.exp(lg_x * q_pow)
        cd_ref[d] = jnp.exp(lg_x * float(CHUNK))
        dpos = jnp.maximum(diff, 0).astype(F32)
        for h in range(RET_HEADS):
            dmat_ref[d, :, h * CHUNK:(h + 1) * CHUNK] = jnp.where(
                diff >= 0, jnp.exp(lg_p[:, h:h + 1] * dpos), 0.0)


def _ret_direction(d, q_ref, k_ref, v_ref, cos_ref, sin_ref, kdec_ref, qdec_ref, cd_ref, dmat_ref,
                   st_ref, o_ref):
    cos, sin = cos_ref[...], sin_ref[...]
    q = _rope(q_ref[...], cos, sin)
    k = _rope(k_ref[...], cos, sin) * (RET_DIM ** -0.5)
    v = v_ref[...]
    vk = (v * kdec_ref[d]).astype(BF16)
    qd = (q * qdec_ref[d]).astype(BF16)
    cd = cd_ref[d]
    qb, vb = q.astype(BF16), v.astype(BF16)
    kt = k.T.astype(BF16)
    k_mask = _block_mask(GROUP_W, HEAD_GROUP * CHUNK, RET_DIM, CHUNK)
    v_mask = _block_mask(HEAD_GROUP * CHUNK, GROUP_W, CHUNK, RET_DIM)
    s_mask = _block_mask(GROUP_W, GROUP_W, RET_DIM, RET_DIM)
    zero = jnp.zeros((), BF16)
    for g in range(RET_HEADS // HEAD_GROUP):
        gl = slice(g * GROUP_W, (g + 1) * GROUP_W)
        sl = slice(g * HEAD_GROUP * CHUNK, (g + 1) * HEAD_GROUP * CHUNK)
        kt_g = kt[gl, :]
        k_bd = jnp.where(k_mask, jnp.concatenate([kt_g] * HEAD_GROUP, axis=1), zero)
        s_all = jnp.dot(qb[:, gl], k_bd, preferred_element_type=F32)
        inner = (s_all * dmat_ref[d, :, sl]).astype(BF16)
        lhs = jnp.concatenate([inner, qd[:, gl]], axis=1)
        v_bd = jnp.where(v_mask, jnp.concatenate([vb[:, gl]] * HEAD_GROUP, axis=0), zero)
        st = st_ref[d, g]
        rhs = jnp.concatenate([v_bd, st.astype(BF16)], axis=0)
        o_ref[:, gl] = jnp.dot(lhs, rhs, preferred_element_type=F32)
        upd = jnp.dot(kt_g, vk[:, gl], preferred_element_type=F32)
        st_ref[d, g] = st * cd[:, gl] + jnp.where(s_mask, upd, 0.0)


def _ret_kernel(qf, kf, vf, cosf, sinf, qb, kb, vb, cosb, sinb, lgx_ref, lgp_ref, of_ref, ob_ref, st_ref,
                kdec_ref, qdec_ref, cd_ref, dmat_ref):
    @pl.when(jnp.logical_and(pl.program_id(0) == 0, pl.program_id(1) == 0))
    def _():
        _ret_tables(lgx_ref, lgp_ref, kdec_ref, qdec_ref, cd_ref, dmat_ref)

    @pl.when(pl.program_id(1) == 0)
    def _():
        st_ref[...] = jnp.zeros_like(st_ref)

    _ret_direction(0, qf, kf, vf, cosf, sinf, kdec_ref, qdec_ref, cd_ref, dmat_ref, st_ref, of_ref)
    _ret_direction(1, qb, kb, vb, cosb, sinb, kdec_ref, qdec_ref, cd_ref, dmat_ref, st_ref, ob_ref)


def _ret_scan(lay, p, decay_logit):
    n = lay.n
    cos_t, sin_t = _rope_tables(lay)
    lgx = jnp.repeat(decay_logit, RET_DIM, axis=-1).reshape(2, 1, RET_WIDTH)
    lgp = jnp.pad(decay_logit, ((0, 0), (0, LANES - RET_HEADS))).reshape(2, 1, LANES)

    def specs(chunk_of):
        def blk(cb):
            return pl.BlockSpec((CHUNK, RET_WIDTH), lambda bi, s: (lay.chunk_block(bi, chunk_of(s)), cb))
        tab = pl.BlockSpec((CHUNK, LANES), lambda bi, s: (chunk_of(s), 0))
        return [blk(P_Q // RET_WIDTH), blk(P_K // RET_WIDTH), blk(P_V // RET_WIDTH), tab, tab]

    def out_spec(chunk_of):
        return pl.BlockSpec((CHUNK, RET_WIDTH), lambda bi, s: (lay.chunk_block(bi, chunk_of(s)), 0))

    const3 = lambda bi, s: (0, 0, 0)
    return pl.pallas_call(
        _ret_kernel,
        out_shape=(jax.ShapeDtypeStruct((n, RET_WIDTH), F32), jax.ShapeDtypeStruct((n, RET_WIDTH), F32)),
        grid=(lay.b, lay.nch),
        in_specs=specs(lay.fwd_chunk) + specs(lay.bwd_chunk)
        + [pl.BlockSpec((2, 1, RET_WIDTH), const3), pl.BlockSpec((2, 1, LANES), const3)],
        out_specs=(out_spec(lay.fwd_chunk), out_spec(lay.bwd_chunk)),
        scratch_shapes=[pltpu.VMEM((2, RET_HEADS // HEAD_GROUP, GROUP_W, GROUP_W), F32),
                        pltpu.VMEM((2, CHUNK, RET_WIDTH), F32),
                        pltpu.VMEM((2, CHUNK, RET_WIDTH), F32),
                        pltpu.VMEM((2, 1, RET_WIDTH), F32),
                        pltpu.VMEM((2, CHUNK, RET_HEADS * CHUNK), F32)],
        compiler_params=_cparams(("arbitrary", "arbitrary")),
        name="ret_scan",
    )(p, p, p, cos_t, sin_t, p, p, p, cos_t, sin_t, lgx, lgp)


def _merge_kernel(lay, tm, emit_h2, x_ref, m_ref, gate_ref, z_ref, g_ref, um_ref, up_ref, un_ref,
                  sf_ref, sb_ref, rf_ref, rb_ref, sng_ref, pw_ref, ps_ref, wb_ref, wo_ref, n2g_ref,
                  wrh_ref, wrl_ref, *rest):
    if emit_h2:
        xo_ref, h2_ref, lg_ref, uw_ref = rest
    else:
        xo_ref, uw_ref = rest
    i = pl.program_id(0)
    ctx_tiles = lay.n_ctx_rows // tm
    per_ctx = lay.ctx // tm
    per_lat = lay.seq // tm
    in_ctx = i < ctx_tiles
    t_in_seg = jnp.where(in_ctx, i % per_ctx, (i - ctx_tiles) % per_lat)
    seg_tiles = jnp.where(in_ctx, per_ctx, per_lat)
    seg_len = jnp.where(in_ctx, lay.ctx, lay.seq)
    pos = t_in_seg * tm + lax.broadcasted_iota(jnp.int32, (tm, 1), 0)

    ys = (sf_ref[...] + sb_ref[...]) * _silu(z_ref[...])
    s_br = ys * lax.rsqrt(jnp.mean(ys * ys, axis=-1, keepdims=True) + EPS) * sng_ref[...]

    uw_ref[0:HALO, :] = jnp.where(t_in_seg == 0, 0.0, up_ref[...])
    uw_ref[HALO:HALO + tm, :] = um_ref[...]
    uw_ref[HALO + tm:, :] = jnp.where(t_in_seg == seg_tiles - 1, 0.0, un_ref[...])
    pooled = []
    for gi, w in enumerate(POOL_WINDOWS):
        left = w // 2
        right = w - 1 - left
        ls = slice(gi * POOL_GROUP, (gi + 1) * POOL_GROUP)
        tot = uw_ref[HALO - left:HALO - left + tm, ls]
        for o in range(-left + 1, right + 1):
            tot = tot + uw_ref[HALO + o:HALO + o + tm, ls]
        cnt = (jnp.minimum(pos + right, seg_len - 1) + 1 - jnp.maximum(pos - left, 0)).astype(F32)
        mixed = tot / cnt - uw_ref[HALO:HALO + tm, ls]
        pooled.append(jnp.dot(mixed.astype(BF16), pw_ref[gi], preferred_element_type=F32))
    p_br = jnp.concatenate(pooled, axis=-1) * ps_ref[...]

    yr = rf_ref[...] + rb_ref[...]
    normed = []
    for h in range(RET_HEADS):
        yh = yr[:, h * RET_DIM:(h + 1) * RET_DIM]
        mu = jnp.mean(yh, axis=-1, keepdims=True)
        dv = yh - mu
        var = jnp.mean(dv * dv, axis=-1, keepdims=True)
        normed.append(dv * lax.rsqrt(var + EPS))
    r_br = jnp.concatenate(normed, axis=-1) * _silu(g_ref[...])

    acc = None
    for bi, br in enumerate((s_br, p_br, r_br)):
        gate = _sigmoid(gate_ref[:, bi * D_MODEL:(bi + 1) * D_MODEL])
        term = gate * jnp.dot(br.astype(BF16), wb_ref[bi], preferred_element_type=F32)
        acc = term if acc is None else acc + term
    mix = jnp.dot(acc.astype(BF16), wo_ref[...], preferred_element_type=F32)
    xn = x_ref[...] + m_ref[2] * mix
    xo_ref[...] = xn
    if emit_h2:
        y = xn * lax.rsqrt(jnp.mean(xn * xn, axis=-1, keepdims=True) + EPS) * n2g_ref[...]
        h2 = y * (1.0 + m_ref[4]) + m_ref[3]
        h2_ref[...] = h2
        hh = h2.astype(BF16)
        hl = (h2 - hh.astype(F32)).astype(BF16)
        wh, wl = wrh_ref[...], wrl_ref[...]
        lg_ref[...] = (jnp.dot(hh, wh, preferred_element_type=F32)
                       + (jnp.dot(hh, wl, preferred_element_type=F32)
                          + jnp.dot(hl, wh, preferred_element_type=F32)))


def _merge(lay, x, mods, p, ssd_f, ssd_b, ret_f, ret_b, ssd_norm_g, pool_w, pool_scale, w_branch, w_out,
           norm2_g, w_router_pad, emit_h2):
    n, d = x.shape
    tm = _row_tile(lay, 256)
    n8 = n // HALO
    per = tm // HALO
    pool_blk = P_POOL // POOL_WIDTH
    wr_hi = w_router_pad.astype(BF16)
    wr_lo = (w_router_pad - wr_hi.astype(F32)).astype(BF16)
    row = lambda i: (i, 0)
    const2 = lambda i: (0, 0)
    const3 = lambda i: (0, 0, 0)
    in_specs = [
        pl.BlockSpec((tm, d), row),
        pl.BlockSpec((None, 6, 1, d), lambda i: (lay.mod_row(i, tm), 0, 0, 0)),
        pl.BlockSpec((tm, N_BRANCH * d), lambda i: (i, P_GATE // (N_BRANCH * d))),
        pl.BlockSpec((tm, SSD_INNER), lambda i: (i, P_Z // SSD_INNER)),
        pl.BlockSpec((tm, RET_WIDTH), lambda i: (i, P_G // RET_WIDTH)),
        pl.BlockSpec((tm, POOL_WIDTH), lambda i: (i, pool_blk)),
        pl.BlockSpec((HALO, POOL_WIDTH), lambda i: (jnp.maximum(i * per - 1, 0), pool_blk)),
        pl.BlockSpec((HALO, POOL_WIDTH), lambda i: (jnp.minimum(i * per + per, n8 - 1), pool_blk)),
        pl.BlockSpec((tm, SSD_INNER), row),
        pl.BlockSpec((tm, SSD_INNER), row),
        pl.BlockSpec((tm, RET_WIDTH), row),
        pl.BlockSpec((tm, RET_WIDTH), row),
        pl.BlockSpec((1, SSD_INNER), const2),
        pl.BlockSpec((len(POOL_WINDOWS), POOL_GROUP, POOL_GROUP), const3),
        pl.BlockSpec((1, POOL_WIDTH), const2),
        pl.BlockSpec((N_BRANCH, SSD_INNER, d), const3),
        pl.BlockSpec((d, d), const2),
        pl.BlockSpec((1, d), const2),
        pl.BlockSpec((d, LANES), const2),
        pl.BlockSpec((d, LANES), const2),
    ]
    out_shape = [jax.ShapeDtypeStruct((n, d), F32)]
    out_specs = [pl.BlockSpec((tm, d), row)]
    if emit_h2:
        out_shape += [jax.ShapeDtypeStruct((n, d), F32), jax.ShapeDtypeStruct((n, LANES), F32)]
        out_specs += [pl.BlockSpec((tm, d), row), pl.BlockSpec((tm, LANES), row)]
    return pl.pallas_call(
        functools.partial(_merge_kernel, lay, tm, emit_h2),
        out_shape=tuple(out_shape),
        grid=(n // tm,),
        in_specs=in_specs,
        out_specs=tuple(out_specs),
        scratch_shapes=[pltpu.VMEM((tm + 2 * HALO, POOL_WIDTH), F32)],
        compiler_params=_cparams(("arbitrary",)),
        name="merge_h2" if emit_h2 else "merge",
    )(x, mods, p, p, p, p, p, p, ssd_f, ssd_b, ret_f, ret_b, ssd_norm_g.reshape(1, -1),
      pool_w.astype(BF16), pool_scale.reshape(1, -1), w_branch.astype(BF16), w_out.astype(BF16),
      norm2_g.reshape(1, -1), wr_hi, wr_lo)


def _ffn_kernel(x_ref, m_ref, g_ref, wg_ref, wu_ref, w2_ref, o_ref, h_ref, acc_ref):
    j = pl.program_id(1)

    @pl.when(j == 0)
    def _():
        x = x_ref[...]
        y = x * lax.rsqrt(jnp.mean(x * x, axis=-1, keepdims=True) + EPS) * g_ref[...]
        h_ref[...] = (y * (1.0 + m_ref[4]) + m_ref[3]).astype(BF16)
        acc_ref[...] = jnp.zeros_like(acc_ref)

    h = h_ref[...]
    gt = jnp.dot(h, wg_ref[...], preferred_element_type=F32)
    up = jnp.dot(h, wu_ref[...], preferred_element_type=F32)
    acc_ref[...] += jnp.dot((_silu(gt) * up).astype(BF16), w2_ref[...], preferred_element_type=F32)

    @pl.when(j == pl.num_programs(1) - 1)
    def _():
        o_ref[...] = x_ref[...] + m_ref[5] * acc_ref[...]


def _ffn_dense(lay, x, mods, g, w13, w2):
    n, d = x.shape
    ff = w2.shape[0]
    tm = _row_tile(lay, 1024)
    tf = 1408
    nf = ff // tf
    return pl.pallas_call(
        _ffn_kernel,
        out_shape=jax.ShapeDtypeStruct((n, d), F32),
        grid=(n // tm, nf),
        in_specs=[
            pl.BlockSpec((tm, d), lambda i, j: (i, 0)),
            pl.BlockSpec((None, 6, 1, d), lambda i, j: (lay.mod_row(i, tm), 0, 0, 0)),
            pl.BlockSpec((1, d), lambda i, j: (0, 0)),
            pl.BlockSpec((d, tf), lambda i, j: (0, j)),
            pl.BlockSpec((d, tf), lambda i, j: (0, j + nf)),
            pl.BlockSpec((tf, d), lambda i, j: (j, 0)),
        ],
        out_specs=pl.BlockSpec((tm, d), lambda i, j: (i, 0)),
        scratch_shapes=[pltpu.VMEM((tm, d), BF16), pltpu.VMEM((tm, d), F32)],
        compiler_params=_cparams(("arbitrary", "arbitrary")),
        name="ffn_dense",
    )(x, mods, g, w13, w13, w2)


ROUTE_TM = 512
R_E0, R_E1, R_RANK0, R_RANK1, R_W0, R_W1 = range(6)


def _route_kernel(lg_ref, tri_ref, o_ref, cnt_ref, run_ref):
    i = pl.program_id(0)

    @pl.when(i == 0)
    def _():
        run_ref[...] = jnp.zeros_like(run_ref)

    tm = lg_ref.shape[0]
    lane = lax.broadcasted_iota(jnp.int32, (tm, LANES), 1).astype(F32)
    lg = jnp.where(lane < N_EXPERTS, lg_ref[...], -jnp.inf)
    m1 = jnp.max(lg, axis=-1, keepdims=True)
    i1 = jnp.min(jnp.where(lg == m1, lane, float(LANES)), axis=-1, keepdims=True)
    l2 = jnp.where(lane == i1, -jnp.inf, lg)
    m2 = jnp.max(l2, axis=-1, keepdims=True)
    i2 = jnp.min(jnp.where(l2 == m2, lane, float(LANES)), axis=-1, keepdims=True)
    e21 = jnp.exp(m2 - m1)
    w1 = 1.0 / (1.0 + e21)
    w2 = e21 * w1
    sel1, sel2 = lane == i1, lane == i2
    memb = jnp.where(sel1, 1.0, jnp.where(sel2, 1.0, 0.0))
    before = jnp.dot(tri_ref[...], memb.astype(BF16), preferred_element_type=F32) + run_ref[...]
    r1 = jnp.sum(jnp.where(sel1, before, 0.0), axis=-1, keepdims=True)
    r2 = jnp.sum(jnp.where(sel2, before, 0.0), axis=-1, keepdims=True)
    rec = jnp.zeros((tm, LANES), F32)
    for k, v in ((R_E0, i1), (R_E1, i2), (R_RANK0, r1), (R_RANK1, r2), (R_W0, w1), (R_W1, w2)):
        rec = jnp.where(lane == k, v, rec)
    o_ref[...] = rec
    run_ref[...] += jnp.sum(memb, axis=0, keepdims=True)
    cnt_ref[...] = jnp.broadcast_to(run_ref[...], cnt_ref.shape)


def _route(logits):
    n = logits.shape[0]
    tm = ROUTE_TM
    i = np.arange(tm)
    tri = jnp.asarray((i[None, :] < i[:, None]).astype(np.float32), BF16)
    return pl.pallas_call(
        _route_kernel,
        out_shape=(jax.ShapeDtypeStruct((n, LANES), F32), jax.ShapeDtypeStruct((8, LANES), F32)),
        grid=(n // tm,),
        in_specs=[pl.BlockSpec((tm, LANES), lambda i: (i, 0)), pl.BlockSpec((tm, tm), lambda i: (0, 0))],
        out_specs=(pl.BlockSpec((tm, LANES), lambda i: (i, 0)), pl.BlockSpec((8, LANES), lambda i: (0, 0))),
        scratch_shapes=[pltpu.VMEM((1, LANES), F32)],
        compiler_params=_cparams(("arbitrary",)),
        name="moe_route",
    )(logits, tri)


MOE_BM = 1024


def _slot_plan(rec, counts):
    n = rec.shape[0]
    cnt = counts[0, :N_EXPERTS].astype(jnp.int32)
    padded = (cnt + MOE_BM - 1) // MOE_BM * MOE_BM
    pad_ends = jnp.cumsum(padded)
    pad_starts = pad_ends - padded
    n_blocks = -(-(n * TOP_K + N_EXPERTS * (MOE_BM - 1)) // MOE_BM)
    n_used = pad_ends[-1] // MOE_BM
    blk = jnp.minimum(jnp.arange(n_blocks, dtype=jnp.int32), n_used - 1)
    block_e = jnp.minimum(jnp.searchsorted(pad_ends, blk * MOE_BM, side='right'), N_EXPERTS - 1)
    e = rec[:, R_E0:R_E1 + 1].astype(jnp.int32)
    rank = rec[:, R_RANK0:R_RANK1 + 1].astype(jnp.int32)
    start = jnp.zeros_like(e)
    for k in range(N_EXPERTS):
        start = jnp.where(e == k, pad_starts[k], start)
    dest = start + rank
    last_blk = jnp.where(cnt > 0, pad_ends // MOE_BM - 1, -1).astype(jnp.int32)
    return dest, block_e.astype(jnp.int32), n_used.astype(jnp.int32).reshape(1), last_blk, n_blocks


DISPATCH_TM = 256


def _dispatch_kernel(last_ref, nb_ref, dest_ref, h_ref, xb_ref, zero_ref, sem, zsem):
    i = pl.program_id(0)
    tm = DISPATCH_TM
    n_blocks = xb_ref.shape[0] // MOE_BM

    def clear_block(blk):
        start = pl.multiple_of(blk * MOE_BM, MOE_BM)
        cp = pltpu.make_async_copy(zero_ref, xb_ref.at[pl.ds(start, MOE_BM), :], zsem)
        cp.start()
        cp.wait()

    @pl.when(i == 0)
    def _():
        zero_ref[...] = jnp.zeros_like(zero_ref)
        for e in range(N_EXPERTS):
            @pl.when(last_ref[e] >= 0)
            def _():
                clear_block(last_ref[e])

            @pl.when(nb_ref[0] + e < n_blocks)
            def _():
                clear_block(nb_ref[0] + e)

    for r in range(tm):
        for k in range(TOP_K):
            pltpu.make_async_copy(h_ref.at[pl.ds(r, 1), :],
                                  xb_ref.at[pl.ds(dest_ref[0, TOP_K * r + k], 1), :], sem).start(priority=k)
    for k in range(TOP_K):
        pltpu.make_async_copy(h_ref, xb_ref.at[pl.ds(0, tm), :], sem).wait()


def _dispatch(h2, dest, last_blk, n_used, n_blocks):
    n, d = h2.shape
    tm = DISPATCH_TM
    cap = n_blocks * MOE_BM
    grid_spec = pltpu.PrefetchScalarGridSpec(
        num_scalar_prefetch=2,
        grid=(n // tm,),
        in_specs=[pl.BlockSpec((None, 1, TOP_K * tm), lambda i, lb, nb: (i, 0, 0), memory_space=pltpu.SMEM),
                  pl.BlockSpec((tm, d), lambda i, lb, nb: (i, 0))],
        out_specs=pl.BlockSpec(memory_space=pl.ANY),
        scratch_shapes=[pltpu.VMEM((MOE_BM, d), F32), pltpu.SemaphoreType.DMA, pltpu.SemaphoreType.DMA],
    )
    return pl.pallas_call(
        _dispatch_kernel,
        out_shape=jax.ShapeDtypeStruct((cap, d), F32),
        grid_spec=grid_spec,
        compiler_params=_cparams(("arbitrary",)),
        name="moe_dispatch",
    )(last_blk, n_used, dest.reshape(n // tm, 1, TOP_K * tm), h2)


def _moe_kernel(be_ref, nb_ref, x_ref, wg_ref, wu_ref, w2_ref, o_ref, h_ref, acc_ref):
    i, j = pl.program_id(0), pl.program_id(1)

    @pl.when(i < nb_ref[0])
    def _():
        @pl.when(j == 0)
        def _():
            h_ref[...] = x_ref[...].astype(BF16)
            acc_ref[...] = jnp.zeros_like(acc_ref)

        h = h_ref[...]
        gt = jnp.dot(h, wg_ref[...].astype(BF16), preferred_element_type=F32)
        up = jnp.dot(h, wu_ref[...].astype(BF16), preferred_element_type=F32)
        acc_ref[...] += jnp.dot((_silu(gt) * up).astype(BF16), w2_ref[...].astype(BF16),
                                preferred_element_type=F32)

        @pl.when(j == pl.num_programs(1) - 1)
        def _():
            o_ref[...] = acc_ref[...]

    @pl.when(jnp.logical_and(i >= nb_ref[0], j == pl.num_programs(1) - 1))
    def _():
        o_ref[...] = jnp.zeros_like(o_ref)


def _moe_blocks(xb, block_e, n_used, w13, w2, li):
    cap, d = xb.shape
    _, ne, ff, _ = w2.shape
    tf = 512
    nf = ff // tf
    n_blocks = cap // MOE_BM

    def row(i, j, be, nb):
        return (jnp.minimum(i, nb[0] - 1), 0)

    def jj(i, j, nb):
        return jnp.where(i < nb[0], j, nf - 1)

    grid_spec = pltpu.PrefetchScalarGridSpec(
        num_scalar_prefetch=2,
        grid=(n_blocks, nf),
        in_specs=[
            pl.BlockSpec((MOE_BM, d), row),
            pl.BlockSpec((None, None, d, tf), lambda i, j, be, nb: (li, be[i], 0, jj(i, j, nb))),
            pl.BlockSpec((None, None, d, tf), lambda i, j, be, nb: (li, be[i], 0, jj(i, j, nb) + nf)),
            pl.BlockSpec((None, None, tf, d), lambda i, j, be, nb: (li, be[i], jj(i, j, nb), 0)),
        ],
        out_specs=pl.BlockSpec((MOE_BM, d), lambda i, j, be, nb: (i, 0)),
        scratch_shapes=[pltpu.VMEM((MOE_BM, d), BF16), pltpu.VMEM((MOE_BM, d), F32)],
    )
    return pl.pallas_call(
        _moe_kernel,
        out_shape=jax.ShapeDtypeStruct((cap, d), F32),
        grid_spec=grid_spec,
        compiler_params=_cparams(("arbitrary", "arbitrary")),
        name="moe_experts",
    )(block_e, n_used, xb, w13, w13, w2)


COMBINE_TM = 256


def _combine_kernel(dest_ref, x_ref, m_ref, rec_ref, yb_ref, o_ref, buf_ref, sem):
    tm = COMBINE_TM

    for r in range(tm):
        for k in range(TOP_K):
            pltpu.make_async_copy(yb_ref.at[pl.ds(dest_ref[0, TOP_K * r + k], 1), :],
                                  buf_ref.at[k, pl.ds(r, 1), :], sem).start(priority=k)
    for k in range(TOP_K):
        pltpu.make_async_copy(yb_ref.at[pl.ds(0, tm), :], buf_ref.at[k], sem).wait()
    rec = rec_ref[...]
    y = rec[:, R_W0:R_W0 + 1] * buf_ref[0] + rec[:, R_W1:R_W1 + 1] * buf_ref[1]
    o_ref[...] = x_ref[...] + m_ref[5] * y


def _combine(lay, x, mods, rec, dest, yb):
    n, d = x.shape
    tm = COMBINE_TM
    row = lambda i: (i, 0)
    return pl.pallas_call(
        _combine_kernel,
        out_shape=jax.ShapeDtypeStruct((n, d), F32),
        grid=(n // tm,),
        in_specs=[pl.BlockSpec((None, 1, TOP_K * tm), lambda i: (i, 0, 0), memory_space=pltpu.SMEM),
                  pl.BlockSpec((tm, d), row),
                  pl.BlockSpec((None, 6, 1, d), lambda i: (lay.mod_row(i, tm), 0, 0, 0)),
                  pl.BlockSpec((tm, LANES), row),
                  pl.BlockSpec(memory_space=pl.ANY)],
        out_specs=pl.BlockSpec((tm, d), row),
        scratch_shapes=[pltpu.VMEM((TOP_K, tm, d), F32), pltpu.SemaphoreType.DMA],
        compiler_params=_cparams(("arbitrary",)),
        name="moe_combine",
    )(dest.reshape(n // tm, 1, TOP_K * tm), x, mods, rec, yb)


def _final_kernel(x_ref, g_ref, o_ref):
    x = x_ref[...]
    o_ref[...] = x * lax.rsqrt(jnp.mean(x * x, axis=-1, keepdims=True) + EPS) * g_ref[...]


def _final_norm(lay, x, g):
    n, d = x.shape
    tm = _row_tile(lay, 1024)
    off = lay.n_ctx_rows // tm
    n_lat = lay.b * lay.seq
    return pl.pallas_call(
        _final_kernel,
        out_shape=jax.ShapeDtypeStruct((n_lat, d), F32),
        grid=(n_lat // tm,),
        in_specs=[pl.BlockSpec((tm, d), lambda i: (i + off, 0)), pl.BlockSpec((1, d), lambda i: (0, 0))],
        out_specs=pl.BlockSpec((tm, d), lambda i: (i, 0)),
        compiler_params=_cparams(("arbitrary",)),
        name="final_norm",
    )(x, g.reshape(1, -1))


def _permute_w_in(w):
    d = w.shape[0]
    parts = [w[:, COL_GATE:COL_GATE + N_BRANCH * D_MODEL], w[:, COL_Q:COL_G_END],
             w[:, COL_POOL:COL_POOL + POOL_WIDTH], w[:, COL_Z:COL_Z + SSD_INNER],
             w[:, COL_XBC:COL_XBC + SSD_XBC], w[:, COL_DT:COL_DT + 2 * SSD_HEADS],
             jnp.zeros((d, LANES - 2 * SSD_HEADS), w.dtype)]
    return jnp.concatenate(parts, axis=1).astype(BF16)


def kernel(x, c, ctx, c_ctx, ada_w, ada_b, norm1_g, norm2_g, w_in, ssd_conv_w, ssd_conv_b, ssd_dt_bias,
           ssd_a_log, ssd_d, ssd_norm_g, pool_w, pool_scale, ret_decay_logit, w_branch, w_out,
           ffn_w13, ffn_w2, moe_router, moe_w13, moe_w2, final_norm_g):
    b, seq, d = x.shape
    ctx_len = ctx.shape[1]
    depth = w_in.shape[0]
    lay = _Layout(b, ctx_len, seq)

    cvec = jnp.concatenate([c, c_ctx[None, :], jnp.zeros((8 - b - 1, d), F32)], axis=0)
    mods_all = _ada_all(cvec, ada_w, ada_b).reshape(depth, 8, 6, 1, d)
    xa = jnp.concatenate([ctx.reshape(-1, d), x.reshape(-1, d)], axis=0)

    for layer in range(depth):
        mods = mods_all[layer]
        p = _in_proj(lay, xa, mods, norm1_g[layer].reshape(1, -1), _permute_w_in(w_in[layer]))
        ssd_f, ssd_b = _ssd_scan(lay, p, ssd_conv_w[layer], ssd_conv_b[layer], ssd_dt_bias[layer],
                                 ssd_a_log[layer], ssd_d[layer])
        ret_f, ret_b = _ret_scan(lay, p, ret_decay_logit[layer])
        is_moe = layer % 2 == 1
        if is_moe:
            w_r = jnp.pad(moe_router[layer // 2], ((0, 0), (0, LANES - N_EXPERTS)))
        else:
            w_r = jnp.zeros((d, LANES), F32)
        outs = _merge(lay, xa, mods, p, ssd_f, ssd_b, ret_f, ret_b, ssd_norm_g[layer], pool_w[layer],
                      pool_scale[layer], w_branch[layer], w_out[layer], norm2_g[layer], w_r, is_moe)
        if not is_moe:
            (xa,) = outs
            xa = _ffn_dense(lay, xa, mods, norm2_g[layer].reshape(1, -1),
                            ffn_w13[layer // 2].astype(BF16), ffn_w2[layer // 2].astype(BF16))
        else:
            xa, h2, logits = outs
            rec, counts = _route(logits)
            dest, block_e, n_used, last_blk, n_blocks = _slot_plan(rec, counts)
            xb = _dispatch(h2, dest, last_blk, n_used, n_blocks)
            yb = _moe_blocks(xb, block_e, n_used, moe_w13, moe_w2, layer // 2)
            xa = _combine(lay, xa, mods, rec, dest, yb)
    return _final_norm(lay, xa, final_norm_g).reshape(b, seq, d)
```

```python
import functools

import numpy as np
import jax
import jax.numpy as jnp
from jax import lax
from jax.experimental import pallas as pl
from jax.experimental.pallas import tpu as pltpu

F32 = jnp.float32
BF16 = jnp.bfloat16
HIGHEST = lax.Precision.HIGHEST

D_MODEL = 1024
GRID_W = 64
EPS = 1e-6
CHUNK = 128
HALO = 16
SSD_HEADS = 8
SSD_HEAD_DIM = 64
SSD_INNER = 512
SSD_STATE = 64
SSD_GROUPS = 2
SSD_CONV = 5
SSD_XBC = 768
POOL_WINDOWS = (2, 4, 8, 16)
POOL_WIDTH = 512
POOL_GROUP = 128
RET_HEADS = 8
RET_DIM = 64
RET_WIDTH = 512
ROPE_BASE = 10000.0
N_BRANCH = 3
N_EXPERTS = 8
TOP_K = 2
LANES = 128
HEAD_GROUP = 4
GROUP_W = HEAD_GROUP * 64

COL_Z = 0
COL_XBC = 512
COL_DT = 1280
COL_POOL = 1296
COL_Q = 1808
COL_G_END = 3856
COL_GATE = 3856
IN_COLS = 6928
P_GATE = 0
P_Q = 3072
P_K = 3584
P_V = 4096
P_G = 4608
P_POOL = 5120
P_Z = 5632
P_XBC = 6144
P_COLS = 6912

VMEM_LIMIT = 56 * 1024 * 1024


def _sigmoid(v):
    return 0.5 * jnp.tanh(0.5 * v) + 0.5


def _silu(v):
    return v * _sigmoid(v)


def _softplus(v):
    return jnp.maximum(v, 0.0) + jnp.log1p(jnp.exp(-jnp.abs(v)))


def _log_sigmoid(v):
    return -_softplus(-v)


def _cparams(sem):
    return pltpu.CompilerParams(dimension_semantics=sem, vmem_limit_bytes=VMEM_LIMIT)


def _split3(x):
    hi = x.astype(BF16)
    r = x - hi.astype(F32)
    mid = r.astype(BF16)
    lo = (r - mid.astype(F32)).astype(BF16)
    return hi, mid, lo


def _dot_sel_right(x, m):
    return sum(jnp.dot(part, m, preferred_element_type=F32) for part in _split3(x))


def _dot_sel_left(m, x):
    return sum(jnp.dot(m, part, preferred_element_type=F32) for part in _split3(x))


def _block_mask(rows, cols, row_blk, col_blk):
    r = lax.broadcasted_iota(jnp.int32, (rows, cols), 0) // row_blk
    c = lax.broadcasted_iota(jnp.int32, (rows, cols), 1) // col_blk
    return r == c


def _ada_kernel(c_ref, w_ref, b_ref, o_ref):
    cv = c_ref[...]
    o_ref[...] = jnp.dot(_silu(cv), w_ref[...], precision=HIGHEST,
                         preferred_element_type=F32) + b_ref[...]


def _ada_all(cvec, ada_w, ada_b):
    depth, d, n6 = ada_w.shape
    tn = 1536
    return pl.pallas_call(
        _ada_kernel,
        out_shape=jax.ShapeDtypeStruct((depth, 8, n6), F32),
        grid=(depth, n6 // tn),
        in_specs=[
            pl.BlockSpec((8, d), lambda l, j: (0, 0)),
            pl.BlockSpec((None, d, tn), lambda l, j: (l, 0, j)),
            pl.BlockSpec((None, 1, tn), lambda l, j: (l, 0, j)),
        ],
        out_specs=pl.BlockSpec((None, 8, tn), lambda l, j: (l, 0, j)),
        compiler_params=_cparams(("arbitrary", "arbitrary")),
        name="ada_mods",
    )(cvec, ada_w, ada_b.reshape(depth, 1, n6))


class _Layout:
    def __init__(self, b, ctx_len, seq):
        self.b, self.ctx, self.seq = b, ctx_len, seq
        self.n_ctx_rows = b * ctx_len
        self.n = b * (ctx_len + seq)
        self.cch = ctx_len // CHUNK
        self.lch = seq // CHUNK
        self.nch = self.cch + self.lch

    def mod_row(self, tile, tm):
        ctx_tiles = self.n_ctx_rows // tm
        per_b = self.seq // tm
        return jnp.where(tile < ctx_tiles, self.b, (tile - ctx_tiles) // per_b)

    def chunk_block(self, bi, c):
        return jnp.where(c < self.cch, bi * self.cch + c,
                         self.b * self.cch + bi * self.lch + (c - self.cch))

    def fwd_chunk(self, s):
        return s

    def bwd_chunk(self, s):
        return jnp.where(s < self.cch, self.cch - 1 - s, self.nch - 1 - (s - self.cch))


def _row_tile(lay, cap):
    tm = cap
    while lay.n_ctx_rows % tm or lay.seq % tm:
        tm //= 2
    return tm


def _in_kernel(x_ref, m_ref, g_ref, w_ref, wdt_ref, o_ref, dt_ref, h_ref):
    @pl.when(pl.program_id(1) == 0)
    def _():
        x = x_ref[...]
        y = x * lax.rsqrt(jnp.mean(x * x, axis=-1, keepdims=True) + EPS) * g_ref[...]
        h_ref[...] = (y * (1.0 + m_ref[1]) + m_ref[0]).astype(BF16)
        dt_ref[...] = jnp.dot(h_ref[...], wdt_ref[...], preferred_element_type=F32)

    o_ref[...] = jnp.dot(h_ref[...], w_ref[...], preferred_element_type=F32).astype(BF16)


def _in_proj(lay, x, mods, g, w, w_dt):
    n, d = x.shape
    tm = _row_tile(lay, 1024)
    tn = 2304
    return pl.pallas_call(
        _in_kernel,
        out_shape=(jax.ShapeDtypeStruct((n, P_COLS), BF16), jax.ShapeDtypeStruct((n, LANES), F32)),
        grid=(n // tm, P_COLS // tn),
        in_specs=[
            pl.BlockSpec((tm, d), lambda i, j: (i, 0)),
            pl.BlockSpec((None, 6, 1, d), lambda i, j: (lay.mod_row(i, tm), 0, 0, 0)),
            pl.BlockSpec((1, d), lambda i, j: (0, 0)),
            pl.BlockSpec((d, tn), lambda i, j: (0, j)),
            pl.BlockSpec((d, LANES), lambda i, j: (0, 0)),
        ],
        out_specs=(pl.BlockSpec((tm, tn), lambda i, j: (i, j)), pl.BlockSpec((tm, LANES), lambda i, j: (i, 0))),
        scratch_shapes=[pltpu.VMEM((tm, d), BF16)],
        compiler_params=_cparams(("arbitrary", "arbitrary")),
        name="in_proj",
    )(x, mods, g, w, w_dt)


def _tri_consts():
    i = np.arange(CHUNK)
    fwd = (i[None, :] <= i[:, None]).astype(np.float32)
    bwd = (i[None, :] >= i[:, None]).astype(np.float32)
    return np.stack([fwd, bwd])


def _expand_consts(heads, width):
    e = np.zeros((2, LANES, heads * width), np.float32)
    for d in range(2):
        for h in range(heads):
            e[d, d * heads + h, h * width:(h + 1) * width] = 1.0
    return e


def _ssd_direction(d, c, lay, xm_ref, xp_ref, xn_ref, dt_ref, tri_ref, exp_ref, cw_ref, cb_ref,
                   dtb_ref, alog_ref, dskip_ref, st_ref, o_ref, xw_ref):
    is_start = jnp.logical_or(c == 0, c == lay.cch)
    is_end = jnp.logical_or(c == lay.cch - 1, c == lay.nch - 1)
    xw_ref[0:HALO, :] = jnp.where(is_start, 0.0, xp_ref[...].astype(F32))
    xw_ref[HALO:HALO + CHUNK, :] = xm_ref[...].astype(F32)
    xw_ref[HALO + CHUNK:, :] = jnp.where(is_end, 0.0, xn_ref[...].astype(F32))
    acc = cb_ref[...] + cw_ref[0:1, :] * xw_ref[HALO - 2:HALO - 2 + CHUNK, :]
    for k in range(1, SSD_CONV):
        acc = acc + cw_ref[k:k + 1, :] * xw_ref[HALO - 2 + k:HALO - 2 + k + CHUNK, :]
    xbc = _silu(acc)
    xs = xbc[:, :SSD_INNER]
    bm = xbc[:, SSD_INNER:SSD_INNER + LANES]
    cm = xbc[:, SSD_INNER + LANES:]
    bt = bm.T.astype(BF16)
    top = lax.broadcasted_iota(jnp.int32, (CHUNK, LANES), 0) < SSD_STATE
    zero = jnp.zeros_like(bt)
    bt_bd = jnp.concatenate([jnp.where(top, bt, zero), jnp.where(top, zero, bt)], axis=1)
    cb_all = jnp.dot(cm.astype(BF16), bt_bd, preferred_element_type=F32)

    tri = tri_ref[d]
    dt_all = _softplus(dt_ref[...] + dtb_ref[...])
    acs = _dot_sel_left(tri, dt_all * (-jnp.exp(alog_ref[...])))
    acs_t = acs.T
    dt_x = _dot_sel_right(dt_all, exp_ref[d])
    acs_x = _dot_sel_right(acs, exp_ref[d])
    last = CHUNK - 1 if d == 0 else 0
    tot_x = acs_x[last:last + 1, :]
    u = xs * dt_x
    ud = (u * jnp.exp(tot_x - acs_x)).astype(BF16)
    ub = u.astype(BF16)
    off_x = jnp.exp(acs_x)
    cd_x = jnp.exp(tot_x)

    li = lax.broadcasted_iota(jnp.int32, (CHUNK, CHUNK), 0)
    si = lax.broadcasted_iota(jnp.int32, (CHUNK, CHUNK), 1)
    mask = (si <= li) if d == 0 else (si >= li)
    lane = lax.broadcasted_iota(jnp.int32, (CHUNK, LANES), 1)
    cm_sw = pltpu.roll(cm, SSD_STATE, 1)
    u_mask = _block_mask(HEAD_GROUP * CHUNK, GROUP_W, CHUNK, SSD_HEAD_DIM)
    s_mask = _block_mask(GROUP_W, GROUP_W, SSD_STATE, SSD_HEAD_DIM)
    for g in range(SSD_GROUPS):
        gl = slice(g * GROUP_W, (g + 1) * GROUP_W)
        cb = cb_all[:, g * CHUNK:(g + 1) * CHUNK]
        parts = []
        for hh in range(HEAD_GROUP):
            h = g * HEAD_GROUP + hh
            col = acs_x[:, h * SSD_HEAD_DIM:h * SSD_HEAD_DIM + 1]
            row = acs_t[d * SSD_HEADS + h:d * SSD_HEADS + h + 1, :]
            lm = jnp.exp(jnp.where(mask, col - row, -jnp.inf))
            parts.append((cb * lm).astype(BF16))
        in_g = (lane < SSD_STATE) if g == 0 else (lane >= SSD_STATE)
        c_rep = jnp.where(in_g, cm, cm_sw)
        c_off = jnp.concatenate([c_rep, c_rep], axis=1) * off_x[:, gl]
        parts.append(c_off.astype(BF16))
        lhs = jnp.concatenate(parts, axis=1)
        ub_g = ub[:, gl]
        u_bd = jnp.where(u_mask, jnp.concatenate([ub_g] * HEAD_GROUP, axis=0), jnp.zeros((), BF16))
        st = st_ref[d, g]
        rhs = jnp.concatenate([u_bd, st.astype(BF16)], axis=0)
        y_g = jnp.dot(lhs, rhs, preferred_element_type=F32)
        if d == 0:
            y_g = y_g + dskip_ref[:, gl] * xs[:, gl]
        o_ref[:, gl] = y_g
        bt_g = bt[g * SSD_STATE:(g + 1) * SSD_STATE, :]
        upd = jnp.dot(jnp.concatenate([bt_g] * HEAD_GROUP, axis=0), ud[:, gl], preferred_element_type=F32)
        st_ref[d, g] = st * cd_x[:, gl] + jnp.where(s_mask, upd, 0.0)


def _ssd_kernel(lay, xm_f, xp_f, xn_f, dt_f, xm_b, xp_b, xn_b, dt_b, tri_ref, exp_ref, cw_ref, cb_ref,
                dtb_ref, alog_ref, dskip_ref, of_ref, ob_ref, st_ref, xw_ref):
    s = pl.program_id(1)

    @pl.when(s == 0)
    def _():
        st_ref[...] = jnp.zeros_like(st_ref)

    _ssd_direction(0, lay.fwd_chunk(s), lay, xm_f, xp_f, xn_f, dt_f, tri_ref, exp_ref, cw_ref, cb_ref,
                   dtb_ref, alog_ref, dskip_ref, st_ref, of_ref, xw_ref)
    _ssd_direction(1, lay.bwd_chunk(s), lay, xm_b, xp_b, xn_b, dt_b, tri_ref, exp_ref, cw_ref, cb_ref,
                   dtb_ref, alog_ref, dskip_ref, st_ref, ob_ref, xw_ref)


def _halo_specs(lay, width, col_block, chunk_of):
    n8 = lay.n // HALO
    per = CHUNK // HALO

    def main(bi, s):
        return (lay.chunk_block(bi, chunk_of(s)), col_block)

    def prev(bi, s):
        return (jnp.maximum(lay.chunk_block(bi, chunk_of(s)) * per - 1, 0), col_block)

    def nxt(bi, s):
        return (jnp.minimum(lay.chunk_block(bi, chunk_of(s)) * per + per, n8 - 1), col_block)

    return [pl.BlockSpec((CHUNK, width), main), pl.BlockSpec((HALO, width), prev),
            pl.BlockSpec((HALO, width), nxt)]


def _ssd_scan(lay, p, dt_raw, conv_w, conv_b, dt_bias, a_log, d_skip):
    n = lay.n
    pad = LANES - 2 * SSD_HEADS
    dtb = jnp.pad(dt_bias.reshape(1, -1), ((0, 0), (0, pad)))
    alog = jnp.pad(a_log.reshape(1, -1), ((0, 0), (0, pad)))
    dskip = jnp.repeat(d_skip, SSD_HEAD_DIM).reshape(1, SSD_INNER)
    tri = jnp.asarray(_tri_consts(), BF16)
    expand = jnp.asarray(_expand_consts(SSD_HEADS, SSD_HEAD_DIM), BF16)
    xbc_blk = P_XBC // SSD_XBC

    def dt_spec(chunk_of):
        return pl.BlockSpec((CHUNK, LANES), lambda bi, s: (lay.chunk_block(bi, chunk_of(s)), 0))

    def out_spec(chunk_of):
        return pl.BlockSpec((CHUNK, SSD_INNER), lambda bi, s: (lay.chunk_block(bi, chunk_of(s)), 0))

    const2 = lambda bi, s: (0, 0)
    const3 = lambda bi, s: (0, 0, 0)
    in_specs = (
        _halo_specs(lay, SSD_XBC, xbc_blk, lay.fwd_chunk) + [dt_spec(lay.fwd_chunk)]
        + _halo_specs(lay, SSD_XBC, xbc_blk, lay.bwd_chunk) + [dt_spec(lay.bwd_chunk)]
        + [pl.BlockSpec((2, CHUNK, CHUNK), const3),
           pl.BlockSpec((2, LANES, SSD_INNER), const3),
           pl.BlockSpec((SSD_CONV, SSD_XBC), const2),
           pl.BlockSpec((1, SSD_XBC), const2),
           pl.BlockSpec((1, LANES), const2),
           pl.BlockSpec((1, LANES), const2),
           pl.BlockSpec((1, SSD_INNER), const2)])
    return pl.pallas_call(
        functools.partial(_ssd_kernel, lay),
        out_shape=(jax.ShapeDtypeStruct((n, SSD_INNER), F32), jax.ShapeDtypeStruct((n, SSD_INNER), F32)),
        grid=(lay.b, lay.nch),
        in_specs=in_specs,
        out_specs=(out_spec(lay.fwd_chunk), out_spec(lay.bwd_chunk)),
        scratch_shapes=[pltpu.VMEM((2, SSD_GROUPS, GROUP_W, GROUP_W), F32),
                        pltpu.VMEM((CHUNK + 2 * HALO, SSD_XBC), F32)],
        compiler_params=_cparams(("arbitrary", "arbitrary")),
        name="ssd_scan",
    )(p, p, p, dt_raw, p, p, p, dt_raw, tri, expand, conv_w, conv_b.reshape(1, -1), dtb, alog, dskip)


def _rope_tables(lay):
    n_axis = RET_DIM // 4
    t = np.arange(lay.seq)
    inv = ROPE_BASE ** (-np.arange(n_axis, dtype=np.float32) / n_axis)
    row = (t // GRID_W).astype(np.float32)
    colp = (t % GRID_W).astype(np.float32)
    ang = jnp.concatenate([jnp.asarray(row)[:, None] * inv, jnp.asarray(colp)[:, None] * inv], axis=-1)
    cos, sin = jnp.cos(ang), jnp.sin(ang)
    cos_l = jnp.concatenate([cos, cos, cos, cos], axis=-1)
    sin_l = jnp.concatenate([-sin, sin, -sin, sin], axis=-1)
    cos_t = jnp.concatenate([jnp.ones((lay.ctx, LANES), F32), cos_l], axis=0)
    sin_t = jnp.concatenate([jnp.zeros((lay.ctx, LANES), F32), sin_l], axis=0)
    return cos_t, sin_t


def _rope(xv, cos, sin):
    lane = lax.broadcasted_iota(jnp.int32, (CHUNK, LANES), 1)
    first_half = (lane % RET_DIM) < (RET_DIM // 2)
    out = []
    for j in range(RET_WIDTH // LANES):
        v = xv[:, j * LANES:(j + 1) * LANES]
        swapped = jnp.where(first_half, pltpu.roll(v, LANES - RET_DIM // 2, 1),
                            pltpu.roll(v, RET_DIM // 2, 1))
        out.append(v * cos + swapped * sin)
    return jnp.concatenate(out, axis=-1)


def _ret_tables(lgx_ref, lgp_ref, kdec_ref, qdec_ref, cd_ref, dmat_ref):
    idx = lax.broadcasted_iota(jnp.int32, (CHUNK, 1), 0).astype(F32)
    ii = lax.broadcasted_iota(jnp.int32, (CHUNK, CHUNK), 0)
    mi = lax.broadcasted_iota(jnp.int32, (CHUNK, CHUNK), 1)
    for d in range(2):
        lg_x = _log_sigmoid(lgx_ref[d])
        lg_p = _log_sigmoid(lgp_ref[d])
        if d == 0:
            k_pow, q_pow, diff = (CHUNK - 1) - idx, idx + 1.0, ii - mi
        else:
            k_pow, q_pow, diff = idx, CHUNK - idx, mi - ii
        kdec_ref[d] = jnp.exp(lg_x * k_pow)
        qdec_ref[d] = jnp.exp(lg_x * q_pow)
        cd_ref[d] = jnp.exp(lg_x * float(CHUNK))
        dpos = jnp.maximum(diff, 0).astype(F32)
        for h in range(RET_HEADS):
            dmat_ref[d, :, h * CHUNK:(h + 1) * CHUNK] = jnp.where(
                diff >= 0, jnp.exp(lg_p[:, h:h + 1] * dpos), 0.0)


def _ret_direction(d, q_ref, k_ref, v_ref, cos_ref, sin_ref, kdec_ref, qdec_ref, cd_ref, dmat_ref,
                   st_ref, o_ref):
    cos, sin = cos_ref[...], sin_ref[...]
    q = _rope(q_ref[...].astype(F32), cos, sin)
    k = _rope(k_ref[...].astype(F32), cos, sin) * (RET_DIM ** -0.5)
    v = v_ref[...].astype(F32)
    vk = (v * kdec_ref[d]).astype(BF16)
    qd = (q * qdec_ref[d]).astype(BF16)
    cd = cd_ref[d]
    qb, vb = q.astype(BF16), v.astype(BF16)
    kt = k.T.astype(BF16)
    k_mask = _block_mask(GROUP_W, HEAD_GROUP * CHUNK, RET_DIM, CHUNK)
    v_mask = _block_mask(HEAD_GROUP * CHUNK, GROUP_W, CHUNK, RET_DIM)
    s_mask = _block_mask(GROUP_W, GROUP_W, RET_DIM, RET_DIM)
    zero = jnp.zeros((), BF16)
    for g in range(RET_HEADS // HEAD_GROUP):
        gl = slice(g * GROUP_W, (g + 1) * GROUP_W)
        sl = slice(g * HEAD_GROUP * CHUNK, (g + 1) * HEAD_GROUP * CHUNK)
        kt_g = kt[gl, :]
        k_bd = jnp.where(k_mask, jnp.concatenate([kt_g] * HEAD_GROUP, axis=1), zero)
        s_all = jnp.dot(qb[:, gl], k_bd, preferred_element_type=F32)
        inner = (s_all * dmat_ref[d, :, sl]).astype(BF16)
        lhs = jnp.concatenate([inner, qd[:, gl]], axis=1)
        v_bd = jnp.where(v_mask, jnp.concatenate([vb[:, gl]] * HEAD_GROUP, axis=0), zero)
        st = st_ref[d, g]
        rhs = jnp.concatenate([v_bd, st.astype(BF16)], axis=0)
        o_ref[:, gl] = jnp.dot(lhs, rhs, preferred_element_type=F32)
        upd = jnp.dot(kt_g, vk[:, gl], preferred_element_type=F32)
        st_ref[d, g] = st * cd[:, gl] + jnp.where(s_mask, upd, 0.0)


def _ret_kernel(qf, kf, vf, cosf, sinf, qb, kb, vb, cosb, sinb, lgx_ref, lgp_ref, of_ref, ob_ref, st_ref,
                kdec_ref, qdec_ref, cd_ref, dmat_ref):
    @pl.when(jnp.logical_and(pl.program_id(0) == 0, pl.program_id(1) == 0))
    def _():
        _ret_tables(lgx_ref, lgp_ref, kdec_ref, qdec_ref, cd_ref, dmat_ref)

    @pl.when(pl.program_id(1) == 0)
    def _():
        st_ref[...] = jnp.zeros_like(st_ref)

    _ret_direction(0, qf, kf, vf, cosf, sinf, kdec_ref, qdec_ref, cd_ref, dmat_ref, st_ref, of_ref)
    _ret_direction(1, qb, kb, vb, cosb, sinb, kdec_ref, qdec_ref, cd_ref, dmat_ref, st_ref, ob_ref)


def _ret_scan(lay, p, decay_logit):
    n = lay.n
    cos_t, sin_t = _rope_tables(lay)
    lgx = jnp.repeat(decay_logit, RET_DIM, axis=-1).reshape(2, 1, RET_WIDTH)
    lgp = jnp.pad(decay_logit, ((0, 0), (0, LANES - RET_HEADS))).reshape(2, 1, LANES)

    def specs(chunk_of):
        def blk(cb):
            return pl.BlockSpec((CHUNK, RET_WIDTH), lambda bi, s: (lay.chunk_block(bi, chunk_of(s)), cb))
        tab = pl.BlockSpec((CHUNK, LANES), lambda bi, s: (chunk_of(s), 0))
        return [blk(P_Q // RET_WIDTH), blk(P_K // RET_WIDTH), blk(P_V // RET_WIDTH), tab, tab]

    def out_spec(chunk_of):
        return pl.BlockSpec((CHUNK, RET_WIDTH), lambda bi, s: (lay.chunk_block(bi, chunk_of(s)), 0))

    const3 = lambda bi, s: (0, 0, 0)
    return pl.pallas_call(
        _ret_kernel,
        out_shape=(jax.ShapeDtypeStruct((n, RET_WIDTH), F32), jax.ShapeDtypeStruct((n, RET_WIDTH), F32)),
        grid=(lay.b, lay.nch),
        in_specs=specs(lay.fwd_chunk) + specs(lay.bwd_chunk)
        + [pl.BlockSpec((2, 1, RET_WIDTH), const3), pl.BlockSpec((2, 1, LANES), const3)],
        out_specs=(out_spec(lay.fwd_chunk), out_spec(lay.bwd_chunk)),
        scratch_shapes=[pltpu.VMEM((2, RET_HEADS // HEAD_GROUP, GROUP_W, GROUP_W), F32),
                        pltpu.VMEM((2, CHUNK, RET_WIDTH), F32),
                        pltpu.VMEM((2, CHUNK, RET_WIDTH), F32),
                        pltpu.VMEM((2, 1, RET_WIDTH), F32),
                        pltpu.VMEM((2, CHUNK, RET_HEADS * CHUNK), F32)],
        compiler_params=_cparams(("arbitrary", "arbitrary")),
        name="ret_scan",
    )(p, p, p, cos_t, sin_t, p, p, p, cos_t, sin_t, lgx, lgp)


def _merge_kernel(lay, tm, emit_h2, x_ref, m_ref, gate_ref, z_ref, g_ref, um_ref, up_ref, un_ref,
                  sf_ref, sb_ref, rf_ref, rb_ref, sng_ref, pw_ref, ps_ref, wb_ref, wo_ref, n2g_ref,
                  wrh_ref, wrl_ref, *rest):
    if emit_h2:
        xo_ref, h2_ref, lg_ref, uw_ref = rest
    else:
        xo_ref, uw_ref = rest
    i = pl.program_id(0)
    ctx_tiles = lay.n_ctx_rows // tm
    per_ctx = lay.ctx // tm
    per_lat = lay.seq // tm
    in_ctx = i < ctx_tiles
    t_in_seg = jnp.where(in_ctx, i % per_ctx, (i - ctx_tiles) % per_lat)
    seg_tiles = jnp.where(in_ctx, per_ctx, per_lat)
    seg_len = jnp.where(in_ctx, lay.ctx, lay.seq)
    pos = t_in_seg * tm + lax.broadcasted_iota(jnp.int32, (tm, 1), 0)

    ys = (sf_ref[...] + sb_ref[...]) * _silu(z_ref[...].astype(F32))
    s_br = ys * lax.rsqrt(jnp.mean(ys * ys, axis=-1, keepdims=True) + EPS) * sng_ref[...]

    uw_ref[0:HALO, :] = jnp.where(t_in_seg == 0, 0.0, up_ref[...].astype(F32))
    uw_ref[HALO:HALO + tm, :] = um_ref[...].astype(F32)
    uw_ref[HALO + tm:, :] = jnp.where(t_in_seg == seg_tiles - 1, 0.0, un_ref[...].astype(F32))
    pooled = []
    for gi, w in enumerate(POOL_WINDOWS):
        left = w // 2
        right = w - 1 - left
        ls = slice(gi * POOL_GROUP, (gi + 1) * POOL_GROUP)
        tot = uw_ref[HALO - left:HALO - left + tm, ls]
        for o in range(-left + 1, right + 1):
            tot = tot + uw_ref[HALO + o:HALO + o + tm, ls]
        cnt = (jnp.minimum(pos + right, seg_len - 1) + 1 - jnp.maximum(pos - left, 0)).astype(F32)
        mixed = tot / cnt - uw_ref[HALO:HALO + tm, ls]
        pooled.append(jnp.dot(mixed.astype(BF16), pw_ref[gi], preferred_element_type=F32))
    p_br = jnp.concatenate(pooled, axis=-1) * ps_ref[...]

    yr = rf_ref[...] + rb_ref[...]
    normed = []
    for h in range(RET_HEADS):
        yh = yr[:, h * RET_DIM:(h + 1) * RET_DIM]
        mu = jnp.mean(yh, axis=-1, keepdims=True)
        dv = yh - mu
        var = jnp.mean(dv * dv, axis=-1, keepdims=True)
        normed.append(dv * lax.rsqrt(var + EPS))
    r_br = jnp.concatenate(normed, axis=-1) * _silu(g_ref[...].astype(F32))

    acc = None
    for bi, br in enumerate((s_br, p_br, r_br)):
        gate = _sigmoid(gate_ref[:, bi * D_MODEL:(bi + 1) * D_MODEL].astype(F32))
        term = gate * jnp.dot(br.astype(BF16), wb_ref[bi], preferred_element_type=F32)
        acc = term if acc is None else acc + term
    mix = jnp.dot(acc.astype(BF16), wo_ref[...], preferred_element_type=F32)
    xn = x_ref[...] + m_ref[2] * mix
    xo_ref[...] = xn
    if emit_h2:
        y = xn * lax.rsqrt(jnp.mean(xn * xn, axis=-1, keepdims=True) + EPS) * n2g_ref[...]
        h2 = y * (1.0 + m_ref[4]) + m_ref[3]
        h2_ref[...] = h2
        hh = h2.astype(BF16)
        hl = (h2 - hh.astype(F32)).astype(BF16)
        wh, wl = wrh_ref[...], wrl_ref[...]
        lg_ref[...] = (jnp.dot(hh, wh, preferred_element_type=F32)
                       + (jnp.dot(hh, wl, preferred_element_type=F32)
                          + jnp.dot(hl, wh, preferred_element_type=F32)))


def _merge(lay, x, mods, p, ssd_f, ssd_b, ret_f, ret_b, ssd_norm_g, pool_w, pool_scale, w_branch, w_out,
           norm2_g, w_router_pad, emit_h2):
    n, d = x.shape
    tm = _row_tile(lay, 256)
    n8 = n // HALO
    per = tm // HALO
    pool_blk = P_POOL // POOL_WIDTH
    wr_hi = w_router_pad.astype(BF16)
    wr_lo = (w_router_pad - wr_hi.astype(F32)).astype(BF16)
    row = lambda i: (i, 0)
    const2 = lambda i: (0, 0)
    const3 = lambda i: (0, 0, 0)
    in_specs = [
        pl.BlockSpec((tm, d), row),
        pl.BlockSpec((None, 6, 1, d), lambda i: (lay.mod_row(i, tm), 0, 0, 0)),
        pl.BlockSpec((tm, N_BRANCH * d), lambda i: (i, P_GATE // (N_BRANCH * d))),
        pl.BlockSpec((tm, SSD_INNER), lambda i: (i, P_Z // SSD_INNER)),
        pl.BlockSpec((tm, RET_WIDTH), lambda i: (i, P_G // RET_WIDTH)),
        pl.BlockSpec((tm, POOL_WIDTH), lambda i: (i, pool_blk)),
        pl.BlockSpec((HALO, POOL_WIDTH), lambda i: (jnp.maximum(i * per - 1, 0), pool_blk)),
        pl.BlockSpec((HALO, POOL_WIDTH), lambda i: (jnp.minimum(i * per + per, n8 - 1), pool_blk)),
        pl.BlockSpec((tm, SSD_INNER), row),
        pl.BlockSpec((tm, SSD_INNER), row),
        pl.BlockSpec((tm, RET_WIDTH), row),
        pl.BlockSpec((tm, RET_WIDTH), row),
        pl.BlockSpec((1, SSD_INNER), const2),
        pl.BlockSpec((len(POOL_WINDOWS), POOL_GROUP, POOL_GROUP), const3),
        pl.BlockSpec((1, POOL_WIDTH), const2),
        pl.BlockSpec((N_BRANCH, SSD_INNER, d), const3),
        pl.BlockSpec((d, d), const2),
        pl.BlockSpec((1, d), const2),
        pl.BlockSpec((d, LANES), const2),
        pl.BlockSpec((d, LANES), const2),
    ]
    out_shape = [jax.ShapeDtypeStruct((n, d), F32)]
    out_specs = [pl.BlockSpec((tm, d), row)]
    if emit_h2:
        out_shape += [jax.ShapeDtypeStruct((n, d), F32), jax.ShapeDtypeStruct((n, LANES), F32)]
        out_specs += [pl.BlockSpec((tm, d), row), pl.BlockSpec((tm, LANES), row)]
    return pl.pallas_call(
        functools.partial(_merge_kernel, lay, tm, emit_h2),
        out_shape=tuple(out_shape),
        grid=(n // tm,),
        in_specs=in_specs,
        out_specs=tuple(out_specs),
        scratch_shapes=[pltpu.VMEM((tm + 2 * HALO, POOL_WIDTH), F32)],
        compiler_params=_cparams(("arbitrary",)),
        name="merge_h2" if emit_h2 else "merge",
    )(x, mods, p, p, p, p, p, p, ssd_f, ssd_b, ret_f, ret_b, ssd_norm_g.reshape(1, -1),
      pool_w.astype(BF16), pool_scale.reshape(1, -1), w_branch.astype(BF16), w_out.astype(BF16),
      norm2_g.reshape(1, -1), wr_hi, wr_lo)


def _ffn_kernel(x_ref, m_ref, g_ref, wg_ref, wu_ref, w2_ref, o_ref, h_ref, acc_ref):
    j = pl.program_id(1)

    @pl.when(j == 0)
    def _():
        x = x_ref[...]
        y = x * lax.rsqrt(jnp.mean(x * x, axis=-1, keepdims=True) + EPS) * g_ref[...]
        h_ref[...] = (y * (1.0 + m_ref[4]) + m_ref[3]).astype(BF16)
        acc_ref[...] = jnp.zeros_like(acc_ref)

    h = h_ref[...]
    gt = jnp.dot(h, wg_ref[...], preferred_element_type=F32)
    up = jnp.dot(h, wu_ref[...], preferred_element_type=F32)
    acc_ref[...] += jnp.dot((_silu(gt) * up).astype(BF16), w2_ref[...], preferred_element_type=F32)

    @pl.when(j == pl.num_programs(1) - 1)
    def _():
        o_ref[...] = x_ref[...] + m_ref[5] * acc_ref[...]


def _ffn_dense(lay, x, mods, g, w13, w2):
    n, d = x.shape
    ff = w2.shape[0]
    tm = _row_tile(lay, 1024)
    tf = 1408
    nf = ff // tf
    return pl.pallas_call(
        _ffn_kernel,
        out_shape=jax.ShapeDtypeStruct((n, d), F32),
        grid=(n // tm, nf),
        in_specs=[
            pl.BlockSpec((tm, d), lambda i, j: (i, 0)),
            pl.BlockSpec((None, 6, 1, d), lambda i, j: (lay.mod_row(i, tm), 0, 0, 0)),
            pl.BlockSpec((1, d), lambda i, j: (0, 0)),
            pl.BlockSpec((d, tf), lambda i, j: (0, j)),
            pl.BlockSpec((d, tf), lambda i, j: (0, j + nf)),
            pl.BlockSpec((tf, d), lambda i, j: (j, 0)),
        ],
        out_specs=pl.BlockSpec((tm, d), lambda i, j: (i, 0)),
        scratch_shapes=[pltpu.VMEM((tm, d), BF16), pltpu.VMEM((tm, d), F32)],
        compiler_params=_cparams(("arbitrary", "arbitrary")),
        name="ffn_dense",
    )(x, mods, g, w13, w13, w2)


ROUTE_TM = 512
R_E0, R_E1, R_RANK0, R_RANK1, R_W0, R_W1 = range(6)


def _route_kernel(lg_ref, tri_ref, o_ref, cnt_ref, run_ref):
    i = pl.program_id(0)

    @pl.when(i == 0)
    def _():
        run_ref[...] = jnp.zeros_like(run_ref)

    tm = lg_ref.shape[0]
    lane = lax.broadcasted_iota(jnp.int32, (tm, LANES), 1).astype(F32)
    lg = jnp.where(lane < N_EXPERTS, lg_ref[...], -jnp.inf)
    m1 = jnp.max(lg, axis=-1, keepdims=True)
    i1 = jnp.min(jnp.where(lg == m1, lane, float(LANES)), axis=-1, keepdims=True)
    l2 = jnp.where(lane == i1, -jnp.inf, lg)
    m2 = jnp.max(l2, axis=-1, keepdims=True)
    i2 = jnp.min(jnp.where(l2 == m2, lane, float(LANES)), axis=-1, keepdims=True)
    e21 = jnp.exp(m2 - m1)
    w1 = 1.0 / (1.0 + e21)
    w2 = e21 * w1
    sel1, sel2 = lane == i1, lane == i2
    memb = jnp.where(sel1, 1.0, jnp.where(sel2, 1.0, 0.0))
    before = jnp.dot(tri_ref[...], memb.astype(BF16), preferred_element_type=F32) + run_ref[...]
    r1 = jnp.sum(jnp.where(sel1, before, 0.0), axis=-1, keepdims=True)
    r2 = jnp.sum(jnp.where(sel2, before, 0.0), axis=-1, keepdims=True)
    rec = jnp.zeros((tm, LANES), F32)
    for k, v in ((R_E0, i1), (R_E1, i2), (R_RANK0, r1), (R_RANK1, r2), (R_W0, w1), (R_W1, w2)):
        rec = jnp.where(lane == k, v, rec)
    o_ref[...] = rec
    run_ref[...] += jnp.sum(memb, axis=0, keepdims=True)
    cnt_ref[...] = jnp.broadcast_to(run_ref[...], cnt_ref.shape)


def _route(logits):
    n = logits.shape[0]
    tm = ROUTE_TM
    i = np.arange(tm)
    tri = jnp.asarray((i[None, :] < i[:, None]).astype(np.float32), BF16)
    return pl.pallas_call(
        _route_kernel,
        out_shape=(jax.ShapeDtypeStruct((n, LANES), F32), jax.ShapeDtypeStruct((8, LANES), F32)),
        grid=(n // tm,),
        in_specs=[pl.BlockSpec((tm, LANES), lambda i: (i, 0)), pl.BlockSpec((tm, tm), lambda i: (0, 0))],
        out_specs=(pl.BlockSpec((tm, LANES), lambda i: (i, 0)), pl.BlockSpec((8, LANES), lambda i: (0, 0))),
        scratch_shapes=[pltpu.VMEM((1, LANES), F32)],
        compiler_params=_cparams(("arbitrary",)),
        name="moe_route",
    )(logits, tri)


MOE_BM = 1024


def _slot_plan(rec, counts):
    n = rec.shape[0]
    cnt = counts[0, :N_EXPERTS].astype(jnp.int32)
    padded = (cnt + MOE_BM - 1) // MOE_BM * MOE_BM
    pad_ends = jnp.cumsum(padded)
    pad_starts = pad_ends - padded
    n_blocks = -(-(n * TOP_K + N_EXPERTS * (MOE_BM - 1)) // MOE_BM)
    n_used = pad_ends[-1] // MOE_BM
    blk = jnp.minimum(jnp.arange(n_blocks, dtype=jnp.int32), n_used - 1)
    block_e = jnp.minimum(jnp.searchsorted(pad_ends, blk * MOE_BM, side='right'), N_EXPERTS - 1)
    e = rec[:, R_E0:R_E1 + 1].astype(jnp.int32)
    rank = rec[:, R_RANK0:R_RANK1 + 1].astype(jnp.int32)
    start = jnp.zeros_like(e)
    for k in range(N_EXPERTS):
        start = jnp.where(e == k, pad_starts[k], start)
    dest = start + rank
    last_blk = jnp.where(cnt > 0, pad_ends // MOE_BM - 1, -1).astype(jnp.int32)
    block_e = block_e.astype(jnp.int32)
    idx = jnp.arange(n_blocks, dtype=jnp.int32)
    changed = jnp.concatenate([jnp.ones((1,), bool), block_e[1:] != block_e[:-1]])
    first = jnp.logical_and(changed, idx < n_used).astype(jnp.int32)
    return dest, block_e, n_used.astype(jnp.int32).reshape(1), last_blk, first, n_blocks


DISPATCH_TM = 256


def _dispatch_kernel(last_ref, nb_ref, dest_ref, h_ref, xb_ref, zero_ref, sem, zsem):
    i = pl.program_id(0)
    tm = DISPATCH_TM
    n_blocks = xb_ref.shape[0] // MOE_BM

    def clear_block(blk):
        start = pl.multiple_of(blk * MOE_BM, MOE_BM)
        cp = pltpu.make_async_copy(zero_ref, xb_ref.at[pl.ds(start, MOE_BM), :], zsem)
        cp.start()
        cp.wait()

    @pl.when(i == 0)
    def _():
        zero_ref[...] = jnp.zeros_like(zero_ref)
        for e in range(N_EXPERTS):
            @pl.when(last_ref[e] >= 0)
            def _():
                clear_block(last_ref[e])

            @pl.when(nb_ref[0] + e < n_blocks)
            def _():
                clear_block(nb_ref[0] + e)

    for r in range(tm):
        for k in range(TOP_K):
            pltpu.make_async_copy(h_ref.at[pl.ds(r, 1), :],
                                  xb_ref.at[pl.ds(dest_ref[0, TOP_K * r + k], 1), :], sem).start(priority=k)
    for k in range(TOP_K):
        pltpu.make_async_copy(h_ref, xb_ref.at[pl.ds(0, tm), :], sem).wait()


def _dispatch(h2, dest, last_blk, n_used, n_blocks):
    n, d = h2.shape
    tm = DISPATCH_TM
    cap = n_blocks * MOE_BM
    grid_spec = pltpu.PrefetchScalarGridSpec(
        num_scalar_prefetch=2,
        grid=(n // tm,),
        in_specs=[pl.BlockSpec((None, 1, TOP_K * tm), lambda i, lb, nb: (i, 0, 0), memory_space=pltpu.SMEM),
                  pl.BlockSpec((tm, d), lambda i, lb, nb: (i, 0))],
        out_specs=pl.BlockSpec(memory_space=pl.ANY),
        scratch_shapes=[pltpu.VMEM((MOE_BM, d), F32), pltpu.SemaphoreType.DMA, pltpu.SemaphoreType.DMA],
    )
    return pl.pallas_call(
        _dispatch_kernel,
        out_shape=jax.ShapeDtypeStruct((cap, d), F32),
        grid_spec=grid_spec,
        compiler_params=_cparams(("arbitrary",)),
        name="moe_dispatch",
    )(last_blk, n_used, dest.reshape(n // tm, 1, TOP_K * tm), h2)


MOE_TF = 512
MOE_VMEM_LIMIT = 60 * 1024 * 1024


def _moe_kernel(be_ref, nb_ref, first_ref, x_ref, wg_ref, wu_ref, w2_ref, o_ref, h_ref, wgs_ref, wus_ref,
                w2s_ref):
    i, j = pl.program_id(0), pl.program_id(1)

    @pl.when(i < nb_ref[0])
    def _():
        @pl.when(first_ref[i] == 1)
        def _():
            wgs_ref[j] = wg_ref[...].astype(BF16)
            wus_ref[j] = wu_ref[...].astype(BF16)
            w2s_ref[j] = w2_ref[...].astype(BF16)

        @pl.when(j == 0)
        def _():
            h_ref[...] = x_ref[...].astype(BF16)

        h = h_ref[...]
        gt = jnp.dot(h, wgs_ref[j], preferred_element_type=F32)
        up = jnp.dot(h, wus_ref[j], preferred_element_type=F32)
        part = jnp.dot((_silu(gt) * up).astype(BF16), w2s_ref[j], preferred_element_type=F32)

        @pl.when(j == 0)
        def _():
            o_ref[...] = part

        @pl.when(j > 0)
        def _():
            o_ref[...] += part

    @pl.when(jnp.logical_and(i >= nb_ref[0], j == pl.num_programs(1) - 1))
    def _():
        o_ref[...] = jnp.zeros_like(o_ref)


def _moe_blocks(xb, block_e, n_used, first, w13, w2, li):
    cap, d = xb.shape
    _, ne, ff, _ = w2.shape
    tf = MOE_TF
    nf = ff // tf
    n_blocks = cap // MOE_BM

    def row(i, j, be, nb, fs):
        return (jnp.minimum(i, nb[0] - 1), 0)

    def jj(i, j, fs):
        return jnp.where(fs[i] == 1, j, nf - 1)

    grid_spec = pltpu.PrefetchScalarGridSpec(
        num_scalar_prefetch=3,
        grid=(n_blocks, nf),
        in_specs=[
            pl.BlockSpec((MOE_BM, d), row),
            pl.BlockSpec((None, None, d, tf), lambda i, j, be, nb, fs: (li, be[i], 0, jj(i, j, fs))),
            pl.BlockSpec((None, None, d, tf), lambda i, j, be, nb, fs: (li, be[i], 0, jj(i, j, fs) + nf)),
            pl.BlockSpec((None, None, tf, d), lambda i, j, be, nb, fs: (li, be[i], jj(i, j, fs), 0)),
        ],
        out_specs=pl.BlockSpec((MOE_BM, d), lambda i, j, be, nb, fs: (i, 0)),
        scratch_shapes=[pltpu.VMEM((MOE_BM, d), BF16), pltpu.VMEM((nf, d, tf), BF16),
                        pltpu.VMEM((nf, d, tf), BF16), pltpu.VMEM((nf, tf, d), BF16)],
    )
    return pl.pallas_call(
        _moe_kernel,
        out_shape=jax.ShapeDtypeStruct((cap, d), F32),
        grid_spec=grid_spec,
        compiler_params=pltpu.CompilerParams(dimension_semantics=("arbitrary", "arbitrary"),
                                             vmem_limit_bytes=MOE_VMEM_LIMIT),
        name="moe_experts",
    )(block_e, n_used, first, xb, w13, w13, w2)


COMBINE_TM = 256


def _combine_kernel(dest_ref, x_ref, m_ref, rec_ref, yb_ref, o_ref, buf_ref, sem):
    tm = COMBINE_TM

    for r in range(tm):
        for k in range(TOP_K):
            pltpu.make_async_copy(yb_ref.at[pl.ds(dest_ref[0, TOP_K * r + k], 1), :],
                                  buf_ref.at[k, pl.ds(r, 1), :], sem).start(priority=k)
    for k in range(TOP_K):
        pltpu.make_async_copy(yb_ref.at[pl.ds(0, tm), :], buf_ref.at[k], sem).wait()
    rec = rec_ref[...]
    y = rec[:, R_W0:R_W0 + 1] * buf_ref[0] + rec[:, R_W1:R_W1 + 1] * buf_ref[1]
    o_ref[...] = x_ref[...] + m_ref[5] * y


def _combine(lay, x, mods, rec, dest, yb):
    n, d = x.shape
    tm = COMBINE_TM
    row = lambda i: (i, 0)
    return pl.pallas_call(
        _combine_kernel,
        out_shape=jax.ShapeDtypeStruct((n, d), F32),
        grid=(n // tm,),
        in_specs=[pl.BlockSpec((None, 1, TOP_K * tm), lambda i: (i, 0, 0), memory_space=pltpu.SMEM),
                  pl.BlockSpec((tm, d), row),
                  pl.BlockSpec((None, 6, 1, d), lambda i: (lay.mod_row(i, tm), 0, 0, 0)),
                  pl.BlockSpec((tm, LANES), row),
                  pl.BlockSpec(memory_space=pl.ANY)],
        out_specs=pl.BlockSpec((tm, d), row),
        scratch_shapes=[pltpu.VMEM((TOP_K, tm, d), F32), pltpu.SemaphoreType.DMA],
        compiler_params=_cparams(("arbitrary",)),
        name="moe_combine",
    )(dest.reshape(n // tm, 1, TOP_K * tm), x, mods, rec, yb)


def _final_kernel(x_ref, g_ref, o_ref):
    x = x_ref[...]
    o_ref[...] = x * lax.rsqrt(jnp.mean(x * x, axis=-1, keepdims=True) + EPS) * g_ref[...]


def _final_norm(lay, x, g):
    n, d = x.shape
    tm = _row_tile(lay, 1024)
    off = lay.n_ctx_rows // tm
    n_lat = lay.b * lay.seq
    return pl.pallas_call(
        _final_kernel,
        out_shape=jax.ShapeDtypeStruct((n_lat, d), F32),
        grid=(n_lat // tm,),
        in_specs=[pl.BlockSpec((tm, d), lambda i: (i + off, 0)), pl.BlockSpec((1, d), lambda i: (0, 0))],
        out_specs=pl.BlockSpec((tm, d), lambda i: (i, 0)),
        compiler_params=_cparams(("arbitrary",)),
        name="final_norm",
    )(x, g.reshape(1, -1))


def _permute_w_in(w):
    d = w.shape[0]
    parts = [w[:, COL_GATE:COL_GATE + N_BRANCH * D_MODEL], w[:, COL_Q:COL_G_END],
             w[:, COL_POOL:COL_POOL + POOL_WIDTH], w[:, COL_Z:COL_Z + SSD_INNER],
             w[:, COL_XBC:COL_XBC + SSD_XBC]]
    w_dt = jnp.pad(w[:, COL_DT:COL_DT + 2 * SSD_HEADS], ((0, 0), (0, LANES - 2 * SSD_HEADS)))
    return jnp.concatenate(parts, axis=1).astype(BF16), w_dt.astype(BF16)


def kernel(x, c, ctx, c_ctx, ada_w, ada_b, norm1_g, norm2_g, w_in, ssd_conv_w, ssd_conv_b, ssd_dt_bias,
           ssd_a_log, ssd_d, ssd_norm_g, pool_w, pool_scale, ret_decay_logit, w_branch, w_out,
           ffn_w13, ffn_w2, moe_router, moe_w13, moe_w2, final_norm_g):
    b, seq, d = x.shape
    ctx_len = ctx.shape[1]
    depth = w_in.shape[0]
    lay = _Layout(b, ctx_len, seq)

    cvec = jnp.concatenate([c, c_ctx[None, :], jnp.zeros((8 - b - 1, d), F32)], axis=0)
    mods_all = _ada_all(cvec, ada_w, ada_b).reshape(depth, 8, 6, 1, d)
    xa = jnp.concatenate([ctx.reshape(-1, d), x.reshape(-1, d)], axis=0)

    for layer in range(depth):
        mods = mods_all[layer]
        w_main, w_dt = _permute_w_in(w_in[layer])
        p, dt_raw = _in_proj(lay, xa, mods, norm1_g[layer].reshape(1, -1), w_main, w_dt)
        ssd_f, ssd_b = _ssd_scan(lay, p, dt_raw, ssd_conv_w[layer], ssd_conv_b[layer], ssd_dt_bias[layer],
                                 ssd_a_log[layer], ssd_d[layer])
        ret_f, ret_b = _ret_scan(lay, p, ret_decay_logit[layer])
        is_moe = layer % 2 == 1
        if is_moe:
            w_r = jnp.pad(moe_router[layer // 2], ((0, 0), (0, LANES - N_EXPERTS)))
        else:
            w_r = jnp.zeros((d, LANES), F32)
        outs = _merge(lay, xa, mods, p, ssd_f, ssd_b, ret_f, ret_b, ssd_norm_g[layer], pool_w[layer],
                      pool_scale[layer], w_branch[layer], w_out[layer], norm2_g[layer], w_r, is_moe)
        if not is_moe:
            (xa,) = outs
            xa = _ffn_dense(lay, xa, mods, norm2_g[layer].reshape(1, -1),
                            ffn_w13[layer // 2].astype(BF16), ffn_w2[layer // 2].astype(BF16))
        else:
            xa, h2, logits = outs
            rec, counts = _route(logits)
            dest, block_e, n_used, last_blk, first, n_blocks = _slot_plan(rec, counts)
            xb = _dispatch(h2, dest, last_blk, n_used, n_blocks)
            yb = _moe_blocks(xb, block_e, n_used, first, moe_w13, moe_w2, layer // 2)
            xa = _combine(lay, xa, mods, rec, dest, yb)
    return _final_norm(lay, xa, final_norm_g).reshape(b, seq, d)
```

```python
import functools

import numpy as np
import jax
import jax.numpy as jnp
from jax import lax
from jax.experimental import pallas as pl
from jax.experimental.pallas import tpu as pltpu

F32 = jnp.float32
BF16 = jnp.bfloat16
HIGHEST = lax.Precision.HIGHEST

D_MODEL = 1024
GRID_W = 64
EPS = 1e-6
CHUNK = 128
SCAN_SUB = 2
SCAN_BLK = SCAN_SUB * CHUNK
HALO = 16
SSD_HEADS = 8
SSD_HEAD_DIM = 64
SSD_INNER = 512
SSD_STATE = 64
SSD_GROUPS = 2
SSD_CONV = 5
SSD_XBC = 768
POOL_WINDOWS = (2, 4, 8, 16)
POOL_WIDTH = 512
POOL_GROUP = 128
RET_HEADS = 8
RET_DIM = 64
RET_WIDTH = 512
ROPE_BASE = 10000.0
N_BRANCH = 3
N_EXPERTS = 8
TOP_K = 2
LANES = 128
HEAD_GROUP = 4
GROUP_W = HEAD_GROUP * 64

COL_Z = 0
COL_XBC = 512
COL_DT = 1280
COL_POOL = 1296
COL_Q = 1808
COL_G_END = 3856
COL_GATE = 3856
IN_COLS = 6928
P_GATE = 0
P_Q = 3072
P_K = 3584
P_V = 4096
P_G = 4608
P_POOL = 5120
P_Z = 5632
P_XBC = 6144
P_COLS = 6912

VMEM_LIMIT = 56 * 1024 * 1024


def _sigmoid(v):
    return 0.5 * jnp.tanh(0.5 * v) + 0.5


def _silu(v):
    return v * _sigmoid(v)


def _softplus(v):
    return jnp.maximum(v, 0.0) + jnp.log1p(jnp.exp(-jnp.abs(v)))


def _log_sigmoid(v):
    return -_softplus(-v)


def _cparams(sem):
    return pltpu.CompilerParams(dimension_semantics=sem, vmem_limit_bytes=VMEM_LIMIT)


def _split3(x):
    hi = x.astype(BF16)
    r = x - hi.astype(F32)
    mid = r.astype(BF16)
    lo = (r - mid.astype(F32)).astype(BF16)
    return hi, mid, lo


def _dot_sel_right(x, m):
    return sum(jnp.dot(part, m, preferred_element_type=F32) for part in _split3(x))


def _dot_sel_left(m, x):
    return sum(jnp.dot(m, part, preferred_element_type=F32) for part in _split3(x))


SHIFT_PAD = 64


def _band_consts(rows, windows):
    m = np.zeros((len(windows), rows, rows + 2 * SHIFT_PAD), np.float32)
    t = np.arange(rows)
    for i, (lo, hi) in enumerate(windows):
        for o in range(lo, hi + 1):
            m[i, t, SHIFT_PAD + t + o] = 1.0
    return m


def _block_mask(rows, cols, row_blk, col_blk):
    r = lax.broadcasted_iota(jnp.int32, (rows, cols), 0) // row_blk
    c = lax.broadcasted_iota(jnp.int32, (rows, cols), 1) // col_blk
    return r == c


def _ada_kernel(c_ref, w_ref, b_ref, o_ref):
    cv = c_ref[...]
    o_ref[...] = jnp.dot(_silu(cv), w_ref[...], precision=HIGHEST,
                         preferred_element_type=F32) + b_ref[...]


def _ada_all(cvec, ada_w, ada_b):
    depth, d, n6 = ada_w.shape
    tn = 1536
    return pl.pallas_call(
        _ada_kernel,
        out_shape=jax.ShapeDtypeStruct((depth, 8, n6), F32),
        grid=(depth, n6 // tn),
        in_specs=[
            pl.BlockSpec((8, d), lambda l, j: (0, 0)),
            pl.BlockSpec((None, d, tn), lambda l, j: (l, 0, j)),
            pl.BlockSpec((None, 1, tn), lambda l, j: (l, 0, j)),
        ],
        out_specs=pl.BlockSpec((None, 8, tn), lambda l, j: (l, 0, j)),
        compiler_params=_cparams(("arbitrary", "arbitrary")),
        name="ada_mods",
    )(cvec, ada_w, ada_b.reshape(depth, 1, n6))


class _Layout:
    def __init__(self, b, ctx_len, seq):
        self.b, self.ctx, self.seq = b, ctx_len, seq
        self.n_ctx_rows = b * ctx_len
        self.n = b * (ctx_len + seq)
        assert ctx_len % SCAN_BLK == 0 and seq % SCAN_BLK == 0
        self.cblk = ctx_len // SCAN_BLK
        self.lblk = seq // SCAN_BLK
        self.nblk = self.cblk + self.lblk

    def mod_row(self, tile, tm):
        ctx_tiles = self.n_ctx_rows // tm
        per_b = self.seq // tm
        return jnp.where(tile < ctx_tiles, self.b, (tile - ctx_tiles) // per_b)

    def scan_row_block(self, bi, c):
        return jnp.where(c < self.cblk, bi * self.cblk + c,
                         self.b * self.cblk + bi * self.lblk + (c - self.cblk))

    def fwd_block(self, s):
        return s

    def bwd_block(self, s):
        return jnp.where(s < self.cblk, self.cblk - 1 - s, self.nblk - 1 - (s - self.cblk))


def _row_tile(lay, cap):
    tm = cap
    while lay.n_ctx_rows % tm or lay.seq % tm:
        tm //= 2
    return tm


def _in_kernel(x_ref, m_ref, g_ref, w_ref, wdt_ref, o_ref, dt_ref, h_ref):
    @pl.when(pl.program_id(1) == 0)
    def _():
        x = x_ref[...]
        y = x * lax.rsqrt(jnp.mean(x * x, axis=-1, keepdims=True) + EPS) * g_ref[...]
        h_ref[...] = (y * (1.0 + m_ref[1]) + m_ref[0]).astype(BF16)
        dt_ref[...] = jnp.dot(h_ref[...], wdt_ref[...], preferred_element_type=F32)

    o_ref[...] = jnp.dot(h_ref[...], w_ref[...], preferred_element_type=F32).astype(BF16)


def _in_proj(lay, x, mods, g, w, w_dt):
    n, d = x.shape
    tm = _row_tile(lay, 1024)
    tn = 2304
    return pl.pallas_call(
        _in_kernel,
        out_shape=(jax.ShapeDtypeStruct((n, P_COLS), BF16), jax.ShapeDtypeStruct((n, LANES), F32)),
        grid=(n // tm, P_COLS // tn),
        in_specs=[
            pl.BlockSpec((tm, d), lambda i, j: (i, 0)),
            pl.BlockSpec((None, 6, 1, d), lambda i, j: (lay.mod_row(i, tm), 0, 0, 0)),
            pl.BlockSpec((1, d), lambda i, j: (0, 0)),
            pl.BlockSpec((d, tn), lambda i, j: (0, j)),
            pl.BlockSpec((d, LANES), lambda i, j: (0, 0)),
        ],
        out_specs=(pl.BlockSpec((tm, tn), lambda i, j: (i, j)), pl.BlockSpec((tm, LANES), lambda i, j: (i, 0))),
        scratch_shapes=[pltpu.VMEM((tm, d), BF16)],
        compiler_params=_cparams(("arbitrary", "arbitrary")),
        name="in_proj",
    )(x, mods, g, w, w_dt)


def _tri_consts():
    i = np.arange(CHUNK)
    fwd = (i[None, :] <= i[:, None]).astype(np.float32)
    bwd = (i[None, :] >= i[:, None]).astype(np.float32)
    return np.stack([fwd, bwd])


def _expand_consts(heads, width):
    e = np.zeros((2, LANES, heads * width), np.float32)
    for d in range(2):
        for h in range(heads):
            e[d, d * heads + h, h * width:(h + 1) * width] = 1.0
    return e


def _ssd_direction(d, c, lay, xm_ref, xp_ref, xn_ref, dt_ref, tri_ref, exp_ref, cw_ref, cb_ref,
                   dtb_ref, alog_ref, dskip_ref, st_ref, o_ref, xw_ref):
    is_start = jnp.logical_or(c == 0, c == lay.cblk)
    is_end = jnp.logical_or(c == lay.cblk - 1, c == lay.nblk - 1)
    xw_ref[0:HALO, :] = jnp.where(is_start, 0.0, xp_ref[...].astype(F32))
    xw_ref[HALO:HALO + SCAN_BLK, :] = xm_ref[...].astype(F32)
    xw_ref[HALO + SCAN_BLK:, :] = jnp.where(is_end, 0.0, xn_ref[...].astype(F32))
    acc = cb_ref[...] + cw_ref[0:1, :] * xw_ref[HALO - 2:HALO - 2 + SCAN_BLK, :]
    for k in range(1, SSD_CONV):
        acc = acc + cw_ref[k:k + 1, :] * xw_ref[HALO - 2 + k:HALO - 2 + k + SCAN_BLK, :]
    xbc_blk = _silu(acc)
    dt_blk = _softplus(dt_ref[...] + dtb_ref[...])
    for k in (range(SCAN_SUB) if d == 0 else reversed(range(SCAN_SUB))):
        rows = slice(k * CHUNK, (k + 1) * CHUNK)
        _ssd_chunk(d, xbc_blk[rows], dt_blk[rows], tri_ref, exp_ref, alog_ref, dskip_ref, st_ref,
                   o_ref.at[pl.ds(k * CHUNK, CHUNK), :])


def _ssd_chunk(d, xbc, dt_all, tri_ref, exp_ref, alog_ref, dskip_ref, st_ref, o_ref):
    xs = xbc[:, :SSD_INNER]
    bm = xbc[:, SSD_INNER:SSD_INNER + LANES]
    cm = xbc[:, SSD_INNER + LANES:]
    bt = bm.T.astype(BF16)
    top = lax.broadcasted_iota(jnp.int32, (CHUNK, LANES), 0) < SSD_STATE
    zero = jnp.zeros_like(bt)
    bt_bd = jnp.concatenate([jnp.where(top, bt, zero), jnp.where(top, zero, bt)], axis=1)
    cb_all = jnp.dot(cm.astype(BF16), bt_bd, preferred_element_type=F32)

    tri = tri_ref[d]
    acs = _dot_sel_left(tri, dt_all * (-jnp.exp(alog_ref[...])))
    acs_t = acs.T
    dt_x = _dot_sel_right(dt_all, exp_ref[d])
    acs_x = _dot_sel_right(acs, exp_ref[d])
    last = CHUNK - 1 if d == 0 else 0
    tot_x = acs_x[last:last + 1, :]
    u = xs * dt_x
    ud = (u * jnp.exp(tot_x - acs_x)).astype(BF16)
    ub = u.astype(BF16)
    off_x = jnp.exp(acs_x)
    cd_x = jnp.exp(tot_x)

    li = lax.broadcasted_iota(jnp.int32, (CHUNK, CHUNK), 0)
    si = lax.broadcasted_iota(jnp.int32, (CHUNK, CHUNK), 1)
    mask = (si <= li) if d == 0 else (si >= li)
    lane = lax.broadcasted_iota(jnp.int32, (CHUNK, LANES), 1)
    cm_sw = pltpu.roll(cm, SSD_STATE, 1)
    u_mask = _block_mask(HEAD_GROUP * CHUNK, GROUP_W, CHUNK, SSD_HEAD_DIM)
    s_mask = _block_mask(GROUP_W, GROUP_W, SSD_STATE, SSD_HEAD_DIM)
    for g in range(SSD_GROUPS):
        gl = slice(g * GROUP_W, (g + 1) * GROUP_W)
        cb = cb_all[:, g * CHUNK:(g + 1) * CHUNK]
        parts = []
        for hh in range(HEAD_GROUP):
            h = g * HEAD_GROUP + hh
            col = acs_x[:, h * SSD_HEAD_DIM:h * SSD_HEAD_DIM + 1]
            row = acs_t[d * SSD_HEADS + h:d * SSD_HEADS + h + 1, :]
            lm = jnp.exp(jnp.where(mask, col - row, -jnp.inf))
            parts.append((cb * lm).astype(BF16))
        in_g = (lane < SSD_STATE) if g == 0 else (lane >= SSD_STATE)
        c_rep = jnp.where(in_g, cm, cm_sw)
        c_off = jnp.concatenate([c_rep, c_rep], axis=1) * off_x[:, gl]
        parts.append(c_off.astype(BF16))
        lhs = jnp.concatenate(parts, axis=1)
        ub_g = ub[:, gl]
        u_bd = jnp.where(u_mask, jnp.concatenate([ub_g] * HEAD_GROUP, axis=0), jnp.zeros((), BF16))
        st = st_ref[d, g]
        rhs = jnp.concatenate([u_bd, st.astype(BF16)], axis=0)
        y_g = jnp.dot(lhs, rhs, preferred_element_type=F32)
        if d == 0:
            y_g = y_g + dskip_ref[:, gl] * xs[:, gl]
        o_ref[:, gl] = y_g
        bt_g = bt[g * SSD_STATE:(g + 1) * SSD_STATE, :]
        upd = jnp.dot(jnp.concatenate([bt_g] * HEAD_GROUP, axis=0), ud[:, gl], preferred_element_type=F32)
        st_ref[d, g] = st * cd_x[:, gl] + jnp.where(s_mask, upd, 0.0)


def _ssd_kernel(lay, xm_f, xp_f, xn_f, dt_f, xm_b, xp_b, xn_b, dt_b, tri_ref, exp_ref, cw_ref, cb_ref,
                dtb_ref, alog_ref, dskip_ref, of_ref, ob_ref, st_ref, xw_ref):
    s = pl.program_id(1)

    @pl.when(s == 0)
    def _():
        st_ref[...] = jnp.zeros_like(st_ref)

    _ssd_direction(0, lay.fwd_block(s), lay, xm_f, xp_f, xn_f, dt_f, tri_ref, exp_ref, cw_ref, cb_ref,
                   dtb_ref, alog_ref, dskip_ref, st_ref, of_ref, xw_ref)
    _ssd_direction(1, lay.bwd_block(s), lay, xm_b, xp_b, xn_b, dt_b, tri_ref, exp_ref, cw_ref, cb_ref,
                   dtb_ref, alog_ref, dskip_ref, st_ref, ob_ref, xw_ref)


def _halo_specs(lay, width, col_block, block_of):
    n_halo = lay.n // HALO
    per = SCAN_BLK // HALO

    def main(bi, s):
        return (lay.scan_row_block(bi, block_of(s)), col_block)

    def prev(bi, s):
        return (jnp.maximum(lay.scan_row_block(bi, block_of(s)) * per - 1, 0), col_block)

    def nxt(bi, s):
        return (jnp.minimum(lay.scan_row_block(bi, block_of(s)) * per + per, n_halo - 1), col_block)

    return [pl.BlockSpec((SCAN_BLK, width), main), pl.BlockSpec((HALO, width), prev),
            pl.BlockSpec((HALO, width), nxt)]


def _ssd_scan(lay, p, dt_raw, conv_w, conv_b, dt_bias, a_log, d_skip):
    n = lay.n
    pad = LANES - 2 * SSD_HEADS
    dtb = jnp.pad(dt_bias.reshape(1, -1), ((0, 0), (0, pad)))
    alog = jnp.pad(a_log.reshape(1, -1), ((0, 0), (0, pad)))
    dskip = jnp.repeat(d_skip, SSD_HEAD_DIM).reshape(1, SSD_INNER)
    tri = jnp.asarray(_tri_consts(), BF16)
    expand = jnp.asarray(_expand_consts(SSD_HEADS, SSD_HEAD_DIM), BF16)
    xbc_blk = P_XBC // SSD_XBC

    def dt_spec(block_of):
        return pl.BlockSpec((SCAN_BLK, LANES), lambda bi, s: (lay.scan_row_block(bi, block_of(s)), 0))

    def out_spec(block_of):
        return pl.BlockSpec((SCAN_BLK, SSD_INNER), lambda bi, s: (lay.scan_row_block(bi, block_of(s)), 0))

    const2 = lambda bi, s: (0, 0)
    const3 = lambda bi, s: (0, 0, 0)
    in_specs = (
        _halo_specs(lay, SSD_XBC, xbc_blk, lay.fwd_block) + [dt_spec(lay.fwd_block)]
        + _halo_specs(lay, SSD_XBC, xbc_blk, lay.bwd_block) + [dt_spec(lay.bwd_block)]
        + [pl.BlockSpec((2, CHUNK, CHUNK), const3),
           pl.BlockSpec((2, LANES, SSD_INNER), const3),
           pl.BlockSpec((SSD_CONV, SSD_XBC), const2),
           pl.BlockSpec((1, SSD_XBC), const2),
           pl.BlockSpec((1, LANES), const2),
           pl.BlockSpec((1, LANES), const2),
           pl.BlockSpec((1, SSD_INNER), const2)])
    return pl.pallas_call(
        functools.partial(_ssd_kernel, lay),
        out_shape=(jax.ShapeDtypeStruct((n, SSD_INNER), F32), jax.ShapeDtypeStruct((n, SSD_INNER), F32)),
        grid=(lay.b, lay.nblk),
        in_specs=in_specs,
        out_specs=(out_spec(lay.fwd_block), out_spec(lay.bwd_block)),
        scratch_shapes=[pltpu.VMEM((2, SSD_GROUPS, GROUP_W, GROUP_W), F32),
                        pltpu.VMEM((SCAN_BLK + 2 * HALO, SSD_XBC), F32)],
        compiler_params=_cparams(("arbitrary", "arbitrary")),
        name="ssd_scan",
    )(p, p, p, dt_raw, p, p, p, dt_raw, tri, expand, conv_w, conv_b.reshape(1, -1), dtb, alog, dskip)


def _rope_tables(lay):
    n_axis = RET_DIM // 4
    t = np.arange(lay.seq)
    inv = ROPE_BASE ** (-np.arange(n_axis, dtype=np.float32) / n_axis)
    row = (t // GRID_W).astype(np.float32)
    colp = (t % GRID_W).astype(np.float32)
    ang = jnp.concatenate([jnp.asarray(row)[:, None] * inv, jnp.asarray(colp)[:, None] * inv], axis=-1)
    cos, sin = jnp.cos(ang), jnp.sin(ang)
    cos_l = jnp.concatenate([cos, cos, cos, cos], axis=-1)
    sin_l = jnp.concatenate([-sin, sin, -sin, sin], axis=-1)
    cos_t = jnp.concatenate([jnp.ones((lay.ctx, LANES), F32), cos_l], axis=0)
    sin_t = jnp.concatenate([jnp.zeros((lay.ctx, LANES), F32), sin_l], axis=0)
    return cos_t, sin_t


def _rope(xv, cos, sin):
    lane = lax.broadcasted_iota(jnp.int32, (xv.shape[0], LANES), 1)
    first_half = (lane % RET_DIM) < (RET_DIM // 2)
    out = []
    for j in range(RET_WIDTH // LANES):
        v = xv[:, j * LANES:(j + 1) * LANES]
        swapped = jnp.where(first_half, pltpu.roll(v, LANES - RET_DIM // 2, 1),
                            pltpu.roll(v, RET_DIM // 2, 1))
        out.append(v * cos + swapped * sin)
    return jnp.concatenate(out, axis=-1)


def _ret_tables(lgx_ref, lgp_ref, kdec_ref, qdec_ref, cd_ref, dmat_ref):
    idx = lax.broadcasted_iota(jnp.int32, (CHUNK, 1), 0).astype(F32)
    ii = lax.broadcasted_iota(jnp.int32, (CHUNK, CHUNK), 0)
    mi = lax.broadcasted_iota(jnp.int32, (CHUNK, CHUNK), 1)
    for d in range(2):
        lg_x = _log_sigmoid(lgx_ref[d])
        lg_p = _log_sigmoid(lgp_ref[d])
        if d == 0:
            k_pow, q_pow, diff = (CHUNK - 1) - idx, idx + 1.0, ii - mi
        else:
            k_pow, q_pow, diff = idx, CHUNK - idx, mi - ii
        kdec_ref[d] = jnp.exp(lg_x * k_pow)
        qdec_ref[d] = jnp.exp(lg_x * q_pow)
        cd_ref[d] = jnp.exp(lg_x * float(CHUNK))
        dpos = jnp.maximum(diff, 0).astype(F32)
        for h in range(RET_HEADS):
            dmat_ref[d, :, h * CHUNK:(h + 1) * CHUNK] = jnp.where(
                diff >= 0, jnp.exp(lg_p[:, h:h + 1] * dpos), 0.0)


def _ret_direction(d, q_ref, k_ref, v_ref, cos_ref, sin_ref, kdec_ref, qdec_ref, cd_ref, dmat_ref,
                   st_ref, o_ref):
    cos, sin = cos_ref[...], sin_ref[...]
    q_blk = _rope(q_ref[...].astype(F32), cos, sin)
    k_blk = _rope(k_ref[...].astype(F32), cos, sin) * (RET_DIM ** -0.5)
    v_blk = v_ref[...].astype(F32)
    for c in (range(SCAN_SUB) if d == 0 else reversed(range(SCAN_SUB))):
        rows = slice(c * CHUNK, (c + 1) * CHUNK)
        _ret_chunk(d, q_blk[rows], k_blk[rows], v_blk[rows], kdec_ref, qdec_ref, cd_ref, dmat_ref, st_ref,
                   o_ref.at[pl.ds(c * CHUNK, CHUNK), :])


def _ret_chunk(d, q, k, v, kdec_ref, qdec_ref, cd_ref, dmat_ref, st_ref, o_ref):
    vk = (v * kdec_ref[d]).astype(BF16)
    qd = (q * qdec_ref[d]).astype(BF16)
    cd = cd_ref[d]
    qb, vb = q.astype(BF16), v.astype(BF16)
    kt = k.T.astype(BF16)
    k_mask = _block_mask(GROUP_W, HEAD_GROUP * CHUNK, RET_DIM, CHUNK)
    v_mask = _block_mask(HEAD_GROUP * CHUNK, GROUP_W, CHUNK, RET_DIM)
    s_mask = _block_mask(GROUP_W, GROUP_W, RET_DIM, RET_DIM)
    zero = jnp.zeros((), BF16)
    for g in range(RET_HEADS // HEAD_GROUP):
        gl = slice(g * GROUP_W, (g + 1) * GROUP_W)
        sl = slice(g * HEAD_GROUP * CHUNK, (g + 1) * HEAD_GROUP * CHUNK)
        kt_g = kt[gl, :]
        k_bd = jnp.where(k_mask, jnp.concatenate([kt_g] * HEAD_GROUP, axis=1), zero)
        s_all = jnp.dot(qb[:, gl], k_bd, preferred_element_type=F32)
        inner = (s_all * dmat_ref[d, :, sl]).astype(BF16)
        lhs = jnp.concatenate([inner, qd[:, gl]], axis=1)
        v_bd = jnp.where(v_mask, jnp.concatenate([vb[:, gl]] * HEAD_GROUP, axis=0), zero)
        st = st_ref[d, g]
        rhs = jnp.concatenate([v_bd, st.astype(BF16)], axis=0)
        o_ref[:, gl] = jnp.dot(lhs, rhs, preferred_element_type=F32)
        upd = jnp.dot(kt_g, vk[:, gl], preferred_element_type=F32)
        st_ref[d, g] = st * cd[:, gl] + jnp.where(s_mask, upd, 0.0)


def _ret_kernel(qf, kf, vf, cosf, sinf, qb, kb, vb, cosb, sinb, lgx_ref, lgp_ref, of_ref, ob_ref,
                st_ref, kdec_ref, qdec_ref, cd_ref, dmat_ref):
    @pl.when(jnp.logical_and(pl.program_id(0) == 0, pl.program_id(1) == 0))
    def _():
        _ret_tables(lgx_ref, lgp_ref, kdec_ref, qdec_ref, cd_ref, dmat_ref)

    @pl.when(pl.program_id(1) == 0)
    def _():
        st_ref[...] = jnp.zeros_like(st_ref)

    _ret_direction(0, qf, kf, vf, cosf, sinf, kdec_ref, qdec_ref, cd_ref, dmat_ref, st_ref, of_ref)
    _ret_direction(1, qb, kb, vb, cosb, sinb, kdec_ref, qdec_ref, cd_ref, dmat_ref, st_ref, ob_ref)


def _ret_scan(lay, p, decay_logit):
    n = lay.n
    cos_t, sin_t = _rope_tables(lay)
    lgx = jnp.repeat(decay_logit, RET_DIM, axis=-1).reshape(2, 1, RET_WIDTH)
    lgp = jnp.pad(decay_logit, ((0, 0), (0, LANES - RET_HEADS))).reshape(2, 1, LANES)

    def specs(block_of):
        def blk(cb):
            return pl.BlockSpec((SCAN_BLK, RET_WIDTH), lambda bi, s: (lay.scan_row_block(bi, block_of(s)), cb))
        tab = pl.BlockSpec((SCAN_BLK, LANES), lambda bi, s: (block_of(s), 0))
        return [blk(P_Q // RET_WIDTH), blk(P_K // RET_WIDTH), blk(P_V // RET_WIDTH), tab, tab]

    def out_spec(block_of):
        return pl.BlockSpec((SCAN_BLK, RET_WIDTH), lambda bi, s: (lay.scan_row_block(bi, block_of(s)), 0))

    const3 = lambda bi, s: (0, 0, 0)
    return pl.pallas_call(
        _ret_kernel,
        out_shape=(jax.ShapeDtypeStruct((n, RET_WIDTH), F32), jax.ShapeDtypeStruct((n, RET_WIDTH), F32)),
        grid=(lay.b, lay.nblk),
        in_specs=specs(lay.fwd_block) + specs(lay.bwd_block)
        + [pl.BlockSpec((2, 1, RET_WIDTH), const3), pl.BlockSpec((2, 1, LANES), const3)],
        out_specs=(out_spec(lay.fwd_block), out_spec(lay.bwd_block)),
        scratch_shapes=[pltpu.VMEM((2, RET_HEADS // HEAD_GROUP, GROUP_W, GROUP_W), F32),
                        pltpu.VMEM((2, CHUNK, RET_WIDTH), F32),
                        pltpu.VMEM((2, CHUNK, RET_WIDTH), F32),
                        pltpu.VMEM((2, 1, RET_WIDTH), F32),
                        pltpu.VMEM((2, CHUNK, RET_HEADS * CHUNK), F32)],
        compiler_params=_cparams(("arbitrary", "arbitrary")),
        name="ret_scan",
    )(p, p, p, cos_t, sin_t, p, p, p, cos_t, sin_t, lgx, lgp)


def _merge_kernel(lay, tm, emit_h2, x_ref, m_ref, gate_ref, z_ref, g_ref, um_ref, up_ref, un_ref,
                  sf_ref, sb_ref, rf_ref, rb_ref, sng_ref, band_ref, pw_ref, ps_ref, wb_ref, wo_ref, n2g_ref,
                  wrh_ref, wrl_ref, *rest):
    if emit_h2:
        xo_ref, h2_ref, lg_ref = rest
    else:
        (xo_ref,) = rest
    i = pl.program_id(0)
    ctx_tiles = lay.n_ctx_rows // tm
    per_ctx = lay.ctx // tm
    per_lat = lay.seq // tm
    in_ctx = i < ctx_tiles
    t_in_seg = jnp.where(in_ctx, i % per_ctx, (i - ctx_tiles) % per_lat)
    seg_tiles = jnp.where(in_ctx, per_ctx, per_lat)
    seg_len = jnp.where(in_ctx, lay.ctx, lay.seq)
    pos = t_in_seg * tm + lax.broadcasted_iota(jnp.int32, (tm, 1), 0)

    ys = (sf_ref[...] + sb_ref[...]) * _silu(z_ref[...].astype(F32))
    s_br = ys * lax.rsqrt(jnp.mean(ys * ys, axis=-1, keepdims=True) + EPS) * sng_ref[...]

    zb = jnp.zeros((), BF16)
    fill = jnp.zeros((SHIFT_PAD - HALO, POOL_WIDTH), BF16)
    um = um_ref[...]
    u_ext = jnp.concatenate([fill, jnp.where(t_in_seg == 0, zb, up_ref[...]), um,
                             jnp.where(t_in_seg == seg_tiles - 1, zb, un_ref[...]), fill], axis=0)
    pooled = []
    for gi, w in enumerate(POOL_WINDOWS):
        left = w // 2
        right = w - 1 - left
        ls = slice(gi * POOL_GROUP, (gi + 1) * POOL_GROUP)
        tot = jnp.dot(band_ref[gi], u_ext[:, ls], preferred_element_type=F32)
        cnt = (jnp.minimum(pos + right, seg_len - 1) + 1 - jnp.maximum(pos - left, 0)).astype(F32)
        mixed = tot / cnt - um[:, ls].astype(F32)
        pooled.append(jnp.dot(mixed.astype(BF16), pw_ref[gi], preferred_element_type=F32))
    p_br = jnp.concatenate(pooled, axis=-1) * ps_ref[...]

    yr = rf_ref[...] + rb_ref[...]
    normed = []
    for h in range(RET_HEADS):
        yh = yr[:, h * RET_DIM:(h + 1) * RET_DIM]
        mu = jnp.mean(yh, axis=-1, keepdims=True)
        dv = yh - mu
        var = jnp.mean(dv * dv, axis=-1, keepdims=True)
        normed.append(dv * lax.rsqrt(var + EPS))
    r_br = jnp.concatenate(normed, axis=-1) * _silu(g_ref[...].astype(F32))

    acc = None
    for bi, br in enumerate((s_br, p_br, r_br)):
        gate = _sigmoid(gate_ref[:, bi * D_MODEL:(bi + 1) * D_MODEL].astype(F32))
        term = gate * jnp.dot(br.astype(BF16), wb_ref[bi], preferred_element_type=F32)
        acc = term if acc is None else acc + term
    mix = jnp.dot(acc.astype(BF16), wo_ref[...], preferred_element_type=F32)
    xn = x_ref[...] + m_ref[2] * mix
    xo_ref[...] = xn
    if emit_h2:
        y = xn * lax.rsqrt(jnp.mean(xn * xn, axis=-1, keepdims=True) + EPS) * n2g_ref[...]
        h2 = y * (1.0 + m_ref[4]) + m_ref[3]
        h2_ref[...] = h2
        hh = h2.astype(BF16)
        hl = (h2 - hh.astype(F32)).astype(BF16)
        wh, wl = wrh_ref[...], wrl_ref[...]
        lg_ref[...] = (jnp.dot(hh, wh, preferred_element_type=F32)
                       + (jnp.dot(hh, wl, preferred_element_type=F32)
                          + jnp.dot(hl, wh, preferred_element_type=F32)))


def _merge(lay, x, mods, p, ssd_f, ssd_b, ret_f, ret_b, ssd_norm_g, pool_w, pool_scale, w_branch, w_out,
           norm2_g, w_router_pad, emit_h2):
    n, d = x.shape
    tm = _row_tile(lay, 256)
    n8 = n // HALO
    per = tm // HALO
    pool_blk = P_POOL // POOL_WIDTH
    wr_hi = w_router_pad.astype(BF16)
    wr_lo = (w_router_pad - wr_hi.astype(F32)).astype(BF16)
    bands = jnp.asarray(_band_consts(tm, [(-(w // 2), w - 1 - w // 2) for w in POOL_WINDOWS]), BF16)
    row = lambda i: (i, 0)
    const2 = lambda i: (0, 0)
    const3 = lambda i: (0, 0, 0)
    in_specs = [
        pl.BlockSpec((tm, d), row),
        pl.BlockSpec((None, 6, 1, d), lambda i: (lay.mod_row(i, tm), 0, 0, 0)),
        pl.BlockSpec((tm, N_BRANCH * d), lambda i: (i, P_GATE // (N_BRANCH * d))),
        pl.BlockSpec((tm, SSD_INNER), lambda i: (i, P_Z // SSD_INNER)),
        pl.BlockSpec((tm, RET_WIDTH), lambda i: (i, P_G // RET_WIDTH)),
        pl.BlockSpec((tm, POOL_WIDTH), lambda i: (i, pool_blk)),
        pl.BlockSpec((HALO, POOL_WIDTH), lambda i: (jnp.maximum(i * per - 1, 0), pool_blk)),
        pl.BlockSpec((HALO, POOL_WIDTH), lambda i: (jnp.minimum(i * per + per, n8 - 1), pool_blk)),
        pl.BlockSpec((tm, SSD_INNER), row),
        pl.BlockSpec((tm, SSD_INNER), row),
        pl.BlockSpec((tm, RET_WIDTH), row),
        pl.BlockSpec((tm, RET_WIDTH), row),
        pl.BlockSpec((1, SSD_INNER), const2),
        pl.BlockSpec((len(POOL_WINDOWS), tm, tm + 2 * SHIFT_PAD), const3),
        pl.BlockSpec((len(POOL_WINDOWS), POOL_GROUP, POOL_GROUP), const3),
        pl.BlockSpec((1, POOL_WIDTH), const2),
        pl.BlockSpec((N_BRANCH, SSD_INNER, d), const3),
        pl.BlockSpec((d, d), const2),
        pl.BlockSpec((1, d), const2),
        pl.BlockSpec((d, LANES), const2),
        pl.BlockSpec((d, LANES), const2),
    ]
    out_shape = [jax.ShapeDtypeStruct((n, d), F32)]
    out_specs = [pl.BlockSpec((tm, d), row)]
    if emit_h2:
        out_shape += [jax.ShapeDtypeStruct((n, d), F32), jax.ShapeDtypeStruct((n, LANES), F32)]
        out_specs += [pl.BlockSpec((tm, d), row), pl.BlockSpec((tm, LANES), row)]
    return pl.pallas_call(
        functools.partial(_merge_kernel, lay, tm, emit_h2),
        out_shape=tuple(out_shape),
        grid=(n // tm,),
        in_specs=in_specs,
        out_specs=tuple(out_specs),
        compiler_params=_cparams(("arbitrary",)),
        name="merge_h2" if emit_h2 else "merge",
    )(x, mods, p, p, p, p, p, p, ssd_f, ssd_b, ret_f, ret_b, ssd_norm_g.reshape(1, -1),
      bands, pool_w.astype(BF16), pool_scale.reshape(1, -1), w_branch.astype(BF16), w_out.astype(BF16),
      norm2_g.reshape(1, -1), wr_hi, wr_lo)


def _ffn_kernel(x_ref, m_ref, g_ref, wg_ref, wu_ref, w2_ref, o_ref, h_ref, acc_ref):
    j = pl.program_id(1)

    @pl.when(j == 0)
    def _():
        x = x_ref[...]
        y = x * lax.rsqrt(jnp.mean(x * x, axis=-1, keepdims=True) + EPS) * g_ref[...]
        h_ref[...] = (y * (1.0 + m_ref[4]) + m_ref[3]).astype(BF16)
        acc_ref[...] = jnp.zeros_like(acc_ref)

    h = h_ref[...]
    gt = jnp.dot(h, wg_ref[...], preferred_element_type=F32)
    up = jnp.dot(h, wu_ref[...], preferred_element_type=F32)
    acc_ref[...] += jnp.dot((_silu(gt) * up).astype(BF16), w2_ref[...], preferred_element_type=F32)

    @pl.when(j == pl.num_programs(1) - 1)
    def _():
        o_ref[...] = x_ref[...] + m_ref[5] * acc_ref[...]


def _ffn_dense(lay, x, mods, g, w13, w2):
    n, d = x.shape
    ff = w2.shape[0]
    tm = _row_tile(lay, 1024)
    tf = 1408
    nf = ff // tf
    return pl.pallas_call(
        _ffn_kernel,
        out_shape=jax.ShapeDtypeStruct((n, d), F32),
        grid=(n // tm, nf),
        in_specs=[
            pl.BlockSpec((tm, d), lambda i, j: (i, 0)),
            pl.BlockSpec((None, 6, 1, d), lambda i, j: (lay.mod_row(i, tm), 0, 0, 0)),
            pl.BlockSpec((1, d), lambda i, j: (0, 0)),
            pl.BlockSpec((d, tf), lambda i, j: (0, j)),
            pl.BlockSpec((d, tf), lambda i, j: (0, j + nf)),
            pl.BlockSpec((tf, d), lambda i, j: (j, 0)),
        ],
        out_specs=pl.BlockSpec((tm, d), lambda i, j: (i, 0)),
        scratch_shapes=[pltpu.VMEM((tm, d), BF16), pltpu.VMEM((tm, d), F32)],
        compiler_params=_cparams(("arbitrary", "arbitrary")),
        name="ffn_dense",
    )(x, mods, g, w13, w13, w2)


ROUTE_TM = 512
R_E0, R_E1, R_RANK0, R_RANK1, R_W0, R_W1 = range(6)


def _route_kernel(lg_ref, tri_ref, o_ref, cnt_ref, run_ref):
    i = pl.program_id(0)

    @pl.when(i == 0)
    def _():
        run_ref[...] = jnp.zeros_like(run_ref)

    tm = lg_ref.shape[0]
    lane = lax.broadcasted_iota(jnp.int32, (tm, LANES), 1).astype(F32)
    lg = jnp.where(lane < N_EXPERTS, lg_ref[...], -jnp.inf)
    m1 = jnp.max(lg, axis=-1, keepdims=True)
    i1 = jnp.min(jnp.where(lg == m1, lane, float(LANES)), axis=-1, keepdims=True)
    l2 = jnp.where(lane == i1, -jnp.inf, lg)
    m2 = jnp.max(l2, axis=-1, keepdims=True)
    i2 = jnp.min(jnp.where(l2 == m2, lane, float(LANES)), axis=-1, keepdims=True)
    e21 = jnp.exp(m2 - m1)
    w1 = 1.0 / (1.0 + e21)
    w2 = e21 * w1
    sel1, sel2 = lane == i1, lane == i2
    memb = jnp.where(sel1, 1.0, jnp.where(sel2, 1.0, 0.0))
    before = jnp.dot(tri_ref[...], memb.astype(BF16), preferred_element_type=F32) + run_ref[...]
    r1 = jnp.sum(jnp.where(sel1, before, 0.0), axis=-1, keepdims=True)
    r2 = jnp.sum(jnp.where(sel2, before, 0.0), axis=-1, keepdims=True)
    rec = jnp.zeros((tm, LANES), F32)
    for k, v in ((R_E0, i1), (R_E1, i2), (R_RANK0, r1), (R_RANK1, r2), (R_W0, w1), (R_W1, w2)):
        rec = jnp.where(lane == k, v, rec)
    o_ref[...] = rec
    run_ref[...] += jnp.sum(memb, axis=0, keepdims=True)
    cnt_ref[...] = jnp.broadcast_to(run_ref[...], cnt_ref.shape)


def _route(logits):
    n = logits.shape[0]
    tm = ROUTE_TM
    i = np.arange(tm)
    tri = jnp.asarray((i[None, :] < i[:, None]).astype(np.float32), BF16)
    return pl.pallas_call(
        _route_kernel,
        out_shape=(jax.ShapeDtypeStruct((n, LANES), F32), jax.ShapeDtypeStruct((8, LANES), F32)),
        grid=(n // tm,),
        in_specs=[pl.BlockSpec((tm, LANES), lambda i: (i, 0)), pl.BlockSpec((tm, tm), lambda i: (0, 0))],
        out_specs=(pl.BlockSpec((tm, LANES), lambda i: (i, 0)), pl.BlockSpec((8, LANES), lambda i: (0, 0))),
        scratch_shapes=[pltpu.VMEM((1, LANES), F32)],
        compiler_params=_cparams(("arbitrary",)),
        name="moe_route",
    )(logits, tri)


MOE_BM = 1024


def _slot_plan(rec, counts):
    n = rec.shape[0]
    cnt = counts[0, :N_EXPERTS].astype(jnp.int32)
    padded = (cnt + MOE_BM - 1) // MOE_BM * MOE_BM
    pad_ends = jnp.cumsum(padded)
    pad_starts = pad_ends - padded
    n_blocks = -(-(n * TOP_K + N_EXPERTS * (MOE_BM - 1)) // MOE_BM)
    n_used = pad_ends[-1] // MOE_BM
    blk = jnp.minimum(jnp.arange(n_blocks, dtype=jnp.int32), n_used - 1)
    block_e = jnp.minimum(jnp.searchsorted(pad_ends, blk * MOE_BM, side='right'), N_EXPERTS - 1)
    e = rec[:, R_E0:R_E1 + 1].astype(jnp.int32)
    rank = rec[:, R_RANK0:R_RANK1 + 1].astype(jnp.int32)
    start = jnp.zeros_like(e)
    for k in range(N_EXPERTS):
        start = jnp.where(e == k, pad_starts[k], start)
    dest = start + rank
    last_blk = jnp.where(cnt > 0, pad_ends // MOE_BM - 1, -1).astype(jnp.int32)
    return dest, block_e.astype(jnp.int32), n_used.astype(jnp.int32).reshape(1), last_blk, n_blocks


DISPATCH_TM = 256


def _dispatch_kernel(last_ref, nb_ref, dest_ref, h_ref, xb_ref, zero_ref, sem, zsem):
    i = pl.program_id(0)
    tm = DISPATCH_TM
    n_blocks = xb_ref.shape[0] // MOE_BM

    def clear_block(blk):
        start = pl.multiple_of(blk * MOE_BM, MOE_BM)
        cp = pltpu.make_async_copy(zero_ref, xb_ref.at[pl.ds(start, MOE_BM), :], zsem)
        cp.start()
        cp.wait()

    @pl.when(i == 0)
    def _():
        zero_ref[...] = jnp.zeros_like(zero_ref)
        for e in range(N_EXPERTS):
            @pl.when(last_ref[e] >= 0)
            def _():
                clear_block(last_ref[e])

            @pl.when(nb_ref[0] + e < n_blocks)
            def _():
                clear_block(nb_ref[0] + e)

    for r in range(tm):
        for k in range(TOP_K):
            pltpu.make_async_copy(h_ref.at[pl.ds(r, 1), :],
                                  xb_ref.at[pl.ds(dest_ref[0, TOP_K * r + k], 1), :], sem).start(priority=k)
    for k in range(TOP_K):
        pltpu.make_async_copy(h_ref, xb_ref.at[pl.ds(0, tm), :], sem).wait()


def _dispatch(h2, dest, last_blk, n_used, n_blocks):
    n, d = h2.shape
    tm = DISPATCH_TM
    cap = n_blocks * MOE_BM
    grid_spec = pltpu.PrefetchScalarGridSpec(
        num_scalar_prefetch=2,
        grid=(n // tm,),
        in_specs=[pl.BlockSpec((None, 1, TOP_K * tm), lambda i, lb, nb: (i, 0, 0), memory_space=pltpu.SMEM),
                  pl.BlockSpec((tm, d), lambda i, lb, nb: (i, 0))],
        out_specs=pl.BlockSpec(memory_space=pl.ANY),
        scratch_shapes=[pltpu.VMEM((MOE_BM, d), F32), pltpu.SemaphoreType.DMA, pltpu.SemaphoreType.DMA],
    )
    return pl.pallas_call(
        _dispatch_kernel,
        out_shape=jax.ShapeDtypeStruct((cap, d), F32),
        grid_spec=grid_spec,
        compiler_params=_cparams(("arbitrary",)),
        name="moe_dispatch",
    )(last_blk, n_used, dest.reshape(n // tm, 1, TOP_K * tm), h2)


MOE_TF = 896


def _moe_kernel(be_ref, nb_ref, x_ref, wg_ref, wu_ref, w2_ref, o_ref, h_ref):
    i, j = pl.program_id(0), pl.program_id(1)

    @pl.when(i < nb_ref[0])
    def _():
        @pl.when(j == 0)
        def _():
            h_ref[...] = x_ref[...].astype(BF16)

        h = h_ref[...]
        gt = jnp.dot(h, wg_ref[...].astype(BF16), preferred_element_type=F32)
        up = jnp.dot(h, wu_ref[...].astype(BF16), preferred_element_type=F32)
        part = jnp.dot((_silu(gt) * up).astype(BF16), w2_ref[...].astype(BF16), preferred_element_type=F32)

        @pl.when(j == 0)
        def _():
            o_ref[...] = part

        @pl.when(j > 0)
        def _():
            o_ref[...] += part

    @pl.when(jnp.logical_and(i >= nb_ref[0], j == pl.num_programs(1) - 1))
    def _():
        o_ref[...] = jnp.zeros_like(o_ref)


def _moe_blocks(xb, block_e, n_used, w13, w2, li):
    cap, d = xb.shape
    _, ne, ff, _ = w2.shape
    tf = MOE_TF
    nf = ff // tf
    n_blocks = cap // MOE_BM

    def row(i, j, be, nb):
        return (jnp.minimum(i, nb[0] - 1), 0)

    def jj(i, j, nb):
        return jnp.where(i < nb[0], j, nf - 1)

    grid_spec = pltpu.PrefetchScalarGridSpec(
        num_scalar_prefetch=2,
        grid=(n_blocks, nf),
        in_specs=[
            pl.BlockSpec((MOE_BM, d), row),
            pl.BlockSpec((None, None, d, tf), lambda i, j, be, nb: (li, be[i], 0, jj(i, j, nb))),
            pl.BlockSpec((None, None, d, tf), lambda i, j, be, nb: (li, be[i], 0, jj(i, j, nb) + nf)),
            pl.BlockSpec((None, None, tf, d), lambda i, j, be, nb: (li, be[i], jj(i, j, nb), 0)),
        ],
        out_specs=pl.BlockSpec((MOE_BM, d), lambda i, j, be, nb: (i, 0)),
        scratch_shapes=[pltpu.VMEM((MOE_BM, d), BF16)],
    )
    return pl.pallas_call(
        _moe_kernel,
        out_shape=jax.ShapeDtypeStruct((cap, d), F32),
        grid_spec=grid_spec,
        compiler_params=_cparams(("arbitrary", "arbitrary")),
        name="moe_experts",
    )(block_e, n_used, xb, w13, w13, w2)


COMBINE_TM = 256


def _combine_kernel(dest_ref, x_ref, m_ref, rec_ref, yb_ref, o_ref, buf_ref, sem):
    tm = COMBINE_TM

    for r in range(tm):
        for k in range(TOP_K):
            pltpu.make_async_copy(yb_ref.at[pl.ds(dest_ref[0, TOP_K * r + k], 1), :],
                                  buf_ref.at[k, pl.ds(r, 1), :], sem).start(priority=k)
    for k in range(TOP_K):
        pltpu.make_async_copy(yb_ref.at[pl.ds(0, tm), :], buf_ref.at[k], sem).wait()
    rec = rec_ref[...]
    y = rec[:, R_W0:R_W0 + 1] * buf_ref[0] + rec[:, R_W1:R_W1 + 1] * buf_ref[1]
    o_ref[...] = x_ref[...] + m_ref[5] * y


def _combine(lay, x, mods, rec, dest, yb):
    n, d = x.shape
    tm = COMBINE_TM
    row = lambda i: (i, 0)
    return pl.pallas_call(
        _combine_kernel,
        out_shape=jax.ShapeDtypeStruct((n, d), F32),
        grid=(n // tm,),
        in_specs=[pl.BlockSpec((None, 1, TOP_K * tm), lambda i: (i, 0, 0), memory_space=pltpu.SMEM),
                  pl.BlockSpec((tm, d), row),
                  pl.BlockSpec((None, 6, 1, d), lambda i: (lay.mod_row(i, tm), 0, 0, 0)),
                  pl.BlockSpec((tm, LANES), row),
                  pl.BlockSpec(memory_space=pl.ANY)],
        out_specs=pl.BlockSpec((tm, d), row),
        scratch_shapes=[pltpu.VMEM((TOP_K, tm, d), F32), pltpu.SemaphoreType.DMA],
        compiler_params=_cparams(("arbitrary",)),
        name="moe_combine",
    )(dest.reshape(n // tm, 1, TOP_K * tm), x, mods, rec, yb)


def _final_kernel(x_ref, g_ref, o_ref):
    x = x_ref[...]
    o_ref[...] = x * lax.rsqrt(jnp.mean(x * x, axis=-1, keepdims=True) + EPS) * g_ref[...]


def _final_norm(lay, x, g):
    n, d = x.shape
    tm = _row_tile(lay, 1024)
    off = lay.n_ctx_rows // tm
    n_lat = lay.b * lay.seq
    return pl.pallas_call(
        _final_kernel,
        out_shape=jax.ShapeDtypeStruct((n_lat, d), F32),
        grid=(n_lat // tm,),
        in_specs=[pl.BlockSpec((tm, d), lambda i: (i + off, 0)), pl.BlockSpec((1, d), lambda i: (0, 0))],
        out_specs=pl.BlockSpec((tm, d), lambda i: (i, 0)),
        compiler_params=_cparams(("arbitrary",)),
        name="final_norm",
    )(x, g.reshape(1, -1))


def _permute_w_in(w):
    d = w.shape[0]
    parts = [w[:, COL_GATE:COL_GATE + N_BRANCH * D_MODEL], w[:, COL_Q:COL_G_END],
             w[:, COL_POOL:COL_POOL + POOL_WIDTH], w[:, COL_Z:COL_Z + SSD_INNER],
             w[:, COL_XBC:COL_XBC + SSD_XBC]]
    w_dt = jnp.pad(w[:, COL_DT:COL_DT + 2 * SSD_HEADS], ((0, 0), (0, LANES - 2 * SSD_HEADS)))
    return jnp.concatenate(parts, axis=1).astype(BF16), w_dt.astype(BF16)


def kernel(x, c, ctx, c_ctx, ada_w, ada_b, norm1_g, norm2_g, w_in, ssd_conv_w, ssd_conv_b, ssd_dt_bias,
           ssd_a_log, ssd_d, ssd_norm_g, pool_w, pool_scale, ret_decay_logit, w_branch, w_out,
           ffn_w13, ffn_w2, moe_router, moe_w13, moe_w2, final_norm_g):
    b, seq, d = x.shape
    ctx_len = ctx.shape[1]
    depth = w_in.shape[0]
    lay = _Layout(b, ctx_len, seq)

    cvec = jnp.concatenate([c, c_ctx[None, :], jnp.zeros((8 - b - 1, d), F32)], axis=0)
    mods_all = _ada_all(cvec, ada_w, ada_b).reshape(depth, 8, 6, 1, d)
    xa = jnp.concatenate([ctx.reshape(-1, d), x.reshape(-1, d)], axis=0)

    for layer in range(depth):
        mods = mods_all[layer]
        w_main, w_dt = _permute_w_in(w_in[layer])
        p, dt_raw = _in_proj(lay, xa, mods, norm1_g[layer].reshape(1, -1), w_main, w_dt)
        ssd_f, ssd_b = _ssd_scan(lay, p, dt_raw, ssd_conv_w[layer], ssd_conv_b[layer], ssd_dt_bias[layer],
                                 ssd_a_log[layer], ssd_d[layer])
        ret_f, ret_b = _ret_scan(lay, p, ret_decay_logit[layer])
        is_moe = layer % 2 == 1
        if is_moe:
            w_r = jnp.pad(moe_router[layer // 2], ((0, 0), (0, LANES - N_EXPERTS)))
        else:
            w_r = jnp.zeros((d, LANES), F32)
        outs = _merge(lay, xa, mods, p, ssd_f, ssd_b, ret_f, ret_b, ssd_norm_g[layer], pool_w[layer],
                      pool_scale[layer], w_branch[layer], w_out[layer], norm2_g[layer], w_r, is_moe)
        if not is_moe:
            (xa,) = outs
            xa = _ffn_dense(lay, xa, mods, norm2_g[layer].reshape(1, -1),
                            ffn_w13[layer // 2].astype(BF16), ffn_w2[layer // 2].astype(BF16))
        else:
            xa, h2, logits = outs
            rec, counts = _route(logits)
            dest, block_e, n_used, last_blk, n_blocks = _slot_plan(rec, counts)
            xb = _dispatch(h2, dest, last_blk, n_used, n_blocks)
            yb = _moe_blocks(xb, block_e, n_used, moe_w13, moe_w2, layer // 2)
            xa = _combine(lay, xa, mods, rec, dest, yb)
    return _final_norm(lay, xa, final_norm_g).reshape(b, seq, d)
```

```python
import functools

import numpy as np
import jax
import jax.numpy as jnp
from jax import lax
from jax.experimental import pallas as pl
from jax.experimental.pallas import tpu as pltpu

F32 = jnp.float32
BF16 = jnp.bfloat16
HIGHEST = lax.Precision.HIGHEST

D_MODEL = 1024
GRID_W = 64
EPS = 1e-6
CHUNK = 128
SCAN_SUB = 2
SCAN_BLK = SCAN_SUB * CHUNK
HALO = 16
SSD_HEADS = 8
SSD_HEAD_DIM = 64
SSD_INNER = 512
SSD_STATE = 64
SSD_GROUPS = 2
SSD_CONV = 5
SSD_XBC = 768
POOL_WINDOWS = (2, 4, 8, 16)
POOL_WIDTH = 512
POOL_GROUP = 128
RET_HEADS = 8
RET_DIM = 64
RET_WIDTH = 512
ROPE_BASE = 10000.0
N_BRANCH = 3
N_EXPERTS = 8
TOP_K = 2
LANES = 128
HEAD_GROUP = 4
GROUP_W = HEAD_GROUP * 64

COL_Z = 0
COL_XBC = 512
COL_DT = 1280
COL_POOL = 1296
COL_Q = 1808
COL_G_END = 3856
COL_GATE = 3856
IN_COLS = 6928
P_GATE = 0
P_Q = 3072
P_K = 3584
P_V = 4096
P_G = 4608
P_POOL = 5120
P_Z = 5632
P_XBC = 6144
P_COLS = 6912

VMEM_LIMIT = 56 * 1024 * 1024


def _sigmoid(v):
    return 0.5 * jnp.tanh(0.5 * v) + 0.5


def _silu(v):
    return v * _sigmoid(v)


def _softplus(v):
    return jnp.maximum(v, 0.0) + jnp.log1p(jnp.exp(-jnp.abs(v)))


def _log_sigmoid(v):
    return -_softplus(-v)


def _cparams(sem):
    return pltpu.CompilerParams(dimension_semantics=sem, vmem_limit_bytes=VMEM_LIMIT)


def _split3(x):
    hi = x.astype(BF16)
    r = x - hi.astype(F32)
    mid = r.astype(BF16)
    lo = (r - mid.astype(F32)).astype(BF16)
    return hi, mid, lo


def _dot_sel_right(x, m):
    return sum(jnp.dot(part, m, preferred_element_type=F32) for part in _split3(x))


def _dot_sel_left(m, x):
    return sum(jnp.dot(m, part, preferred_element_type=F32) for part in _split3(x))


SHIFT_PAD = 64


def _band_consts(rows, windows):
    m = np.zeros((len(windows), rows, rows + 2 * SHIFT_PAD), np.float32)
    t = np.arange(rows)
    for i, (lo, hi) in enumerate(windows):
        for o in range(lo, hi + 1):
            m[i, t, SHIFT_PAD + t + o] = 1.0
    return m


def _block_mask(rows, cols, row_blk, col_blk):
    r = lax.broadcasted_iota(jnp.int32, (rows, cols), 0) // row_blk
    c = lax.broadcasted_iota(jnp.int32, (rows, cols), 1) // col_blk
    return r == c


def _ada_kernel(c_ref, w_ref, b_ref, o_ref):
    cv = c_ref[...]
    o_ref[...] = jnp.dot(_silu(cv), w_ref[...], precision=HIGHEST,
                         preferred_element_type=F32) + b_ref[...]


def _ada_all(cvec, ada_w, ada_b):
    depth, d, n6 = ada_w.shape
    tn = 1536
    return pl.pallas_call(
        _ada_kernel,
        out_shape=jax.ShapeDtypeStruct((depth, 8, n6), F32),
        grid=(depth, n6 // tn),
        in_specs=[
            pl.BlockSpec((8, d), lambda l, j: (0, 0)),
            pl.BlockSpec((None, d, tn), lambda l, j: (l, 0, j)),
            pl.BlockSpec((None, 1, tn), lambda l, j: (l, 0, j)),
        ],
        out_specs=pl.BlockSpec((None, 8, tn), lambda l, j: (l, 0, j)),
        compiler_params=_cparams(("arbitrary", "arbitrary")),
        name="ada_mods",
    )(cvec, ada_w, ada_b.reshape(depth, 1, n6))


class _Layout:
    def __init__(self, b, ctx_len, seq):
        self.b, self.ctx, self.seq = b, ctx_len, seq
        self.n_ctx_rows = b * ctx_len
        self.n = b * (ctx_len + seq)
        assert ctx_len % SCAN_BLK == 0 and seq % SCAN_BLK == 0
        self.cblk = ctx_len // SCAN_BLK
        self.lblk = seq // SCAN_BLK
        self.nblk = self.cblk + self.lblk

    def mod_row(self, tile, tm):
        ctx_tiles = self.n_ctx_rows // tm
        per_b = self.seq // tm
        return jnp.where(tile < ctx_tiles, self.b, (tile - ctx_tiles) // per_b)

    def scan_row_block(self, bi, c):
        return jnp.where(c < self.cblk, bi * self.cblk + c,
                         self.b * self.cblk + bi * self.lblk + (c - self.cblk))

    def fwd_block(self, s):
        return s

    def bwd_block(self, s):
        return jnp.where(s < self.cblk, self.cblk - 1 - s, self.nblk - 1 - (s - self.cblk))


def _row_tile(lay, cap):
    tm = cap
    while lay.n_ctx_rows % tm or lay.seq % tm:
        tm //= 2
    return tm


def _in_kernel(x_ref, m_ref, g_ref, w_ref, wdt_ref, o_ref, dt_ref, h_ref):
    @pl.when(pl.program_id(1) == 0)
    def _():
        x = x_ref[...]
        y = x * lax.rsqrt(jnp.mean(x * x, axis=-1, keepdims=True) + EPS) * g_ref[...]
        h_ref[...] = (y * (1.0 + m_ref[1]) + m_ref[0]).astype(BF16)
        dt_ref[...] = jnp.dot(h_ref[...], wdt_ref[...], preferred_element_type=F32)

    o_ref[...] = jnp.dot(h_ref[...], w_ref[...], preferred_element_type=F32).astype(BF16)


def _in_proj(lay, x, mods, g, w, w_dt):
    n, d = x.shape
    tm = _row_tile(lay, 1024)
    tn = 2304
    return pl.pallas_call(
        _in_kernel,
        out_shape=(jax.ShapeDtypeStruct((n, P_COLS), BF16), jax.ShapeDtypeStruct((n, LANES), F32)),
        grid=(n // tm, P_COLS // tn),
        in_specs=[
            pl.BlockSpec((tm, d), lambda i, j: (i, 0)),
            pl.BlockSpec((None, 6, 1, d), lambda i, j: (lay.mod_row(i, tm), 0, 0, 0)),
            pl.BlockSpec((1, d), lambda i, j: (0, 0)),
            pl.BlockSpec((d, tn), lambda i, j: (0, j)),
            pl.BlockSpec((d, LANES), lambda i, j: (0, 0)),
        ],
        out_specs=(pl.BlockSpec((tm, tn), lambda i, j: (i, j)), pl.BlockSpec((tm, LANES), lambda i, j: (i, 0))),
        scratch_shapes=[pltpu.VMEM((tm, d), BF16)],
        compiler_params=_cparams(("arbitrary", "arbitrary")),
        name="in_proj",
    )(x, mods, g, w, w_dt)


def _tri_consts():
    i = np.arange(CHUNK)
    fwd = (i[None, :] <= i[:, None]).astype(np.float32)
    bwd = (i[None, :] >= i[:, None]).astype(np.float32)
    return np.stack([fwd, bwd])


def _expand_consts(heads, width):
    e = np.zeros((2, LANES, heads * width), np.float32)
    for d in range(2):
        for h in range(heads):
            e[d, d * heads + h, h * width:(h + 1) * width] = 1.0
    return e


def _ssd_direction(d, c, lay, xm_ref, xp_ref, xn_ref, dt_ref, tri_ref, exp_ref, cw_ref, cb_ref,
                   dtb_ref, alog_ref, dskip_ref, st_ref, o_ref, xw_ref):
    is_start = jnp.logical_or(c == 0, c == lay.cblk)
    is_end = jnp.logical_or(c == lay.cblk - 1, c == lay.nblk - 1)
    xw_ref[0:HALO, :] = jnp.where(is_start, 0.0, xp_ref[...].astype(F32))
    xw_ref[HALO:HALO + SCAN_BLK, :] = xm_ref[...].astype(F32)
    xw_ref[HALO + SCAN_BLK:, :] = jnp.where(is_end, 0.0, xn_ref[...].astype(F32))
    acc = cb_ref[...] + cw_ref[0:1, :] * xw_ref[HALO - 2:HALO - 2 + SCAN_BLK, :]
    for k in range(1, SSD_CONV):
        acc = acc + cw_ref[k:k + 1, :] * xw_ref[HALO - 2 + k:HALO - 2 + k + SCAN_BLK, :]
    xbc_blk = _silu(acc)
    dt_blk = _softplus(dt_ref[...] + dtb_ref[...])
    for k in (range(SCAN_SUB) if d == 0 else reversed(range(SCAN_SUB))):
        rows = slice(k * CHUNK, (k + 1) * CHUNK)
        _ssd_chunk(d, xbc_blk[rows], dt_blk[rows], tri_ref, exp_ref, alog_ref, dskip_ref, st_ref,
                   o_ref.at[pl.ds(k * CHUNK, CHUNK), :])


def _ssd_chunk(d, xbc, dt_all, tri_ref, exp_ref, alog_ref, dskip_ref, st_ref, o_ref):
    xs = xbc[:, :SSD_INNER]
    bm = xbc[:, SSD_INNER:SSD_INNER + LANES]
    cm = xbc[:, SSD_INNER + LANES:]
    bt = bm.T.astype(BF16)
    top = lax.broadcasted_iota(jnp.int32, (CHUNK, LANES), 0) < SSD_STATE
    zero = jnp.zeros_like(bt)
    bt_bd = jnp.concatenate([jnp.where(top, bt, zero), jnp.where(top, zero, bt)], axis=1)
    cb_all = jnp.dot(cm.astype(BF16), bt_bd, preferred_element_type=F32)

    tri = tri_ref[d]
    acs = _dot_sel_left(tri, dt_all * (-jnp.exp(alog_ref[...])))
    acs_t = acs.T
    dt_x = _dot_sel_right(dt_all, exp_ref[d])
    acs_x = _dot_sel_right(acs, exp_ref[d])
    last = CHUNK - 1 if d == 0 else 0
    tot_x = acs_x[last:last + 1, :]
    u = xs * dt_x
    ud = (u * jnp.exp(tot_x - acs_x)).astype(BF16)
    ub = u.astype(BF16)
    off_x = jnp.exp(acs_x)
    cd_x = jnp.exp(tot_x)

    li = lax.broadcasted_iota(jnp.int32, (CHUNK, CHUNK), 0)
    si = lax.broadcasted_iota(jnp.int32, (CHUNK, CHUNK), 1)
    mask = (si <= li) if d == 0 else (si >= li)
    lane = lax.broadcasted_iota(jnp.int32, (CHUNK, LANES), 1)
    cm_sw = pltpu.roll(cm, SSD_STATE, 1)
    u_mask = _block_mask(HEAD_GROUP * CHUNK, GROUP_W, CHUNK, SSD_HEAD_DIM)
    s_mask = _block_mask(GROUP_W, GROUP_W, SSD_STATE, SSD_HEAD_DIM)
    for g in range(SSD_GROUPS):
        gl = slice(g * GROUP_W, (g + 1) * GROUP_W)
        cb = cb_all[:, g * CHUNK:(g + 1) * CHUNK]
        parts = []
        for hh in range(HEAD_GROUP):
            h = g * HEAD_GROUP + hh
            col = acs_x[:, h * SSD_HEAD_DIM:h * SSD_HEAD_DIM + 1]
            row = acs_t[d * SSD_HEADS + h:d * SSD_HEADS + h + 1, :]
            lm = jnp.exp(jnp.where(mask, col - row, -jnp.inf))
            parts.append((cb * lm).astype(BF16))
        in_g = (lane < SSD_STATE) if g == 0 else (lane >= SSD_STATE)
        c_rep = jnp.where(in_g, cm, cm_sw)
        c_off = jnp.concatenate([c_rep, c_rep], axis=1) * off_x[:, gl]
        parts.append(c_off.astype(BF16))
        lhs = jnp.concatenate(parts, axis=1)
        ub_g = ub[:, gl]
        u_bd = jnp.where(u_mask, jnp.concatenate([ub_g] * HEAD_GROUP, axis=0), jnp.zeros((), BF16))
        st = st_ref[d, g]
        rhs = jnp.concatenate([u_bd, st.astype(BF16)], axis=0)
        y_g = jnp.dot(lhs, rhs, preferred_element_type=F32)
        if d == 0:
            y_g = y_g + dskip_ref[:, gl] * xs[:, gl]
        o_ref[:, gl] = y_g
        bt_g = bt[g * SSD_STATE:(g + 1) * SSD_STATE, :]
        upd = jnp.dot(jnp.concatenate([bt_g] * HEAD_GROUP, axis=0), ud[:, gl], preferred_element_type=F32)
        st_ref[d, g] = st * cd_x[:, gl] + jnp.where(s_mask, upd, 0.0)


def _ssd_kernel(lay, xm_f, xp_f, xn_f, dt_f, xm_b, xp_b, xn_b, dt_b, tri_ref, exp_ref, cw_ref, cb_ref,
                dtb_ref, alog_ref, dskip_ref, of_ref, ob_ref, st_ref, xw_ref):
    s = pl.program_id(1)

    @pl.when(s == 0)
    def _():
        st_ref[...] = jnp.zeros_like(st_ref)

    _ssd_direction(0, lay.fwd_block(s), lay, xm_f, xp_f, xn_f, dt_f, tri_ref, exp_ref, cw_ref, cb_ref,
                   dtb_ref, alog_ref, dskip_ref, st_ref, of_ref, xw_ref)
    _ssd_direction(1, lay.bwd_block(s), lay, xm_b, xp_b, xn_b, dt_b, tri_ref, exp_ref, cw_ref, cb_ref,
                   dtb_ref, alog_ref, dskip_ref, st_ref, ob_ref, xw_ref)


def _halo_specs(lay, width, col_block, block_of):
    n_halo = lay.n // HALO
    per = SCAN_BLK // HALO

    def main(bi, s):
        return (lay.scan_row_block(bi, block_of(s)), col_block)

    def prev(bi, s):
        return (jnp.maximum(lay.scan_row_block(bi, block_of(s)) * per - 1, 0), col_block)

    def nxt(bi, s):
        return (jnp.minimum(lay.scan_row_block(bi, block_of(s)) * per + per, n_halo - 1), col_block)

    return [pl.BlockSpec((SCAN_BLK, width), main), pl.BlockSpec((HALO, width), prev),
            pl.BlockSpec((HALO, width), nxt)]


def _ssd_scan(lay, p, dt_raw, conv_w, conv_b, dt_bias, a_log, d_skip):
    n = lay.n
    pad = LANES - 2 * SSD_HEADS
    dtb = jnp.pad(dt_bias.reshape(1, -1), ((0, 0), (0, pad)))
    alog = jnp.pad(a_log.reshape(1, -1), ((0, 0), (0, pad)))
    dskip = jnp.repeat(d_skip, SSD_HEAD_DIM).reshape(1, SSD_INNER)
    tri = jnp.asarray(_tri_consts(), BF16)
    expand = jnp.asarray(_expand_consts(SSD_HEADS, SSD_HEAD_DIM), BF16)
    xbc_blk = P_XBC // SSD_XBC

    def dt_spec(block_of):
        return pl.BlockSpec((SCAN_BLK, LANES), lambda bi, s: (lay.scan_row_block(bi, block_of(s)), 0))

    def out_spec(block_of):
        return pl.BlockSpec((SCAN_BLK, SSD_INNER), lambda bi, s: (lay.scan_row_block(bi, block_of(s)), 0))

    const2 = lambda bi, s: (0, 0)
    const3 = lambda bi, s: (0, 0, 0)
    in_specs = (
        _halo_specs(lay, SSD_XBC, xbc_blk, lay.fwd_block) + [dt_spec(lay.fwd_block)]
        + _halo_specs(lay, SSD_XBC, xbc_blk, lay.bwd_block) + [dt_spec(lay.bwd_block)]
        + [pl.BlockSpec((2, CHUNK, CHUNK), const3),
           pl.BlockSpec((2, LANES, SSD_INNER), const3),
           pl.BlockSpec((SSD_CONV, SSD_XBC), const2),
           pl.BlockSpec((1, SSD_XBC), const2),
           pl.BlockSpec((1, LANES), const2),
           pl.BlockSpec((1, LANES), const2),
           pl.BlockSpec((1, SSD_INNER), const2)])
    return pl.pallas_call(
        functools.partial(_ssd_kernel, lay),
        out_shape=(jax.ShapeDtypeStruct((n, SSD_INNER), F32), jax.ShapeDtypeStruct((n, SSD_INNER), F32)),
        grid=(lay.b, lay.nblk),
        in_specs=in_specs,
        out_specs=(out_spec(lay.fwd_block), out_spec(lay.bwd_block)),
        scratch_shapes=[pltpu.VMEM((2, SSD_GROUPS, GROUP_W, GROUP_W), F32),
                        pltpu.VMEM((SCAN_BLK + 2 * HALO, SSD_XBC), F32)],
        compiler_params=_cparams(("arbitrary", "arbitrary")),
        name="ssd_scan",
    )(p, p, p, dt_raw, p, p, p, dt_raw, tri, expand, conv_w, conv_b.reshape(1, -1), dtb, alog, dskip)


def _rope_tables(lay):
    n_axis = RET_DIM // 4
    t = np.arange(lay.seq)
    inv = ROPE_BASE ** (-np.arange(n_axis, dtype=np.float32) / n_axis)
    row = (t // GRID_W).astype(np.float32)
    colp = (t % GRID_W).astype(np.float32)
    ang = jnp.concatenate([jnp.asarray(row)[:, None] * inv, jnp.asarray(colp)[:, None] * inv], axis=-1)
    cos, sin = jnp.cos(ang), jnp.sin(ang)
    cos_l = jnp.concatenate([cos, cos, cos, cos], axis=-1)
    sin_l = jnp.concatenate([-sin, sin, -sin, sin], axis=-1)
    cos_t = jnp.concatenate([jnp.ones((lay.ctx, LANES), F32), cos_l], axis=0)
    sin_t = jnp.concatenate([jnp.zeros((lay.ctx, LANES), F32), sin_l], axis=0)
    return cos_t, sin_t


def _rope(xv, cos, sin):
    lane = lax.broadcasted_iota(jnp.int32, (xv.shape[0], LANES), 1)
    first_half = (lane % RET_DIM) < (RET_DIM // 2)
    out = []
    for j in range(RET_WIDTH // LANES):
        v = xv[:, j * LANES:(j + 1) * LANES]
        swapped = jnp.where(first_half, pltpu.roll(v, LANES - RET_DIM // 2, 1),
                            pltpu.roll(v, RET_DIM // 2, 1))
        out.append(v * cos + swapped * sin)
    return jnp.concatenate(out, axis=-1)


def _ret_tables(lgx_ref, lgp_ref, kdec_ref, qdec_ref, cd_ref, dmat_ref):
    idx = lax.broadcasted_iota(jnp.int32, (CHUNK, 1), 0).astype(F32)
    ii = lax.broadcasted_iota(jnp.int32, (CHUNK, CHUNK), 0)
    mi = lax.broadcasted_iota(jnp.int32, (CHUNK, CHUNK), 1)
    for d in range(2):
        lg_x = _log_sigmoid(lgx_ref[d])
        lg_p = _log_sigmoid(lgp_ref[d])
        if d == 0:
            k_pow, q_pow, diff = (CHUNK - 1) - idx, idx + 1.0, ii - mi
        else:
            k_pow, q_pow, diff = idx, CHUNK - idx, mi - ii
        kdec_ref[d] = jnp.exp(lg_x * k_pow)
        qdec_ref[d] = jnp.exp(lg_x * q_pow)
        cd_ref[d] = jnp.exp(lg_x * float(CHUNK))
        dpos = jnp.maximum(diff, 0).astype(F32)
        for h in range(RET_HEADS):
            dmat_ref[d, :, h * CHUNK:(h + 1) * CHUNK] = jnp.where(
                diff >= 0, jnp.exp(lg_p[:, h:h + 1] * dpos), 0.0)


def _ret_direction(d, q_ref, k_ref, v_ref, cos_ref, sin_ref, kdec_ref, qdec_ref, cd_ref, dmat_ref,
                   st_ref, o_ref):
    cos, sin = cos_ref[...], sin_ref[...]
    q_blk = _rope(q_ref[...].astype(F32), cos, sin)
    k_blk = _rope(k_ref[...].astype(F32), cos, sin) * (RET_DIM ** -0.5)
    v_blk = v_ref[...].astype(F32)
    for c in (range(SCAN_SUB) if d == 0 else reversed(range(SCAN_SUB))):
        rows = slice(c * CHUNK, (c + 1) * CHUNK)
        _ret_chunk(d, q_blk[rows], k_blk[rows], v_blk[rows], kdec_ref, qdec_ref, cd_ref, dmat_ref, st_ref,
                   o_ref.at[pl.ds(c * CHUNK, CHUNK), :])


def _ret_chunk(d, q, k, v, kdec_ref, qdec_ref, cd_ref, dmat_ref, st_ref, o_ref):
    vk = (v * kdec_ref[d]).astype(BF16)
    qd = (q * qdec_ref[d]).astype(BF16)
    cd = cd_ref[d]
    qb, vb = q.astype(BF16), v.astype(BF16)
    kt = k.T.astype(BF16)
    k_mask = _block_mask(GROUP_W, HEAD_GROUP * CHUNK, RET_DIM, CHUNK)
    v_mask = _block_mask(HEAD_GROUP * CHUNK, GROUP_W, CHUNK, RET_DIM)
    s_mask = _block_mask(GROUP_W, GROUP_W, RET_DIM, RET_DIM)
    zero = jnp.zeros((), BF16)
    for g in range(RET_HEADS // HEAD_GROUP):
        gl = slice(g * GROUP_W, (g + 1) * GROUP_W)
        sl = slice(g * HEAD_GROUP * CHUNK, (g + 1) * HEAD_GROUP * CHUNK)
        kt_g = kt[gl, :]
        k_bd = jnp.where(k_mask, jnp.concatenate([kt_g] * HEAD_GROUP, axis=1), zero)
        s_all = jnp.dot(qb[:, gl], k_bd, preferred_element_type=F32)
        inner = (s_all * dmat_ref[d, :, sl]).astype(BF16)
        lhs = jnp.concatenate([inner, qd[:, gl]], axis=1)
        v_bd = jnp.where(v_mask, jnp.concatenate([vb[:, gl]] * HEAD_GROUP, axis=0), zero)
        st = st_ref[d, g]
        rhs = jnp.concatenate([v_bd, st.astype(BF16)], axis=0)
        o_ref[:, gl] = jnp.dot(lhs, rhs, preferred_element_type=F32)
        upd = jnp.dot(kt_g, vk[:, gl], preferred_element_type=F32)
        st_ref[d, g] = st * cd[:, gl] + jnp.where(s_mask, upd, 0.0)


def _ret_kernel(qf, kf, vf, cosf, sinf, qb, kb, vb, cosb, sinb, lgx_ref, lgp_ref, of_ref, ob_ref,
                st_ref, kdec_ref, qdec_ref, cd_ref, dmat_ref):
    @pl.when(jnp.logical_and(pl.program_id(0) == 0, pl.program_id(1) == 0))
    def _():
        _ret_tables(lgx_ref, lgp_ref, kdec_ref, qdec_ref, cd_ref, dmat_ref)

    @pl.when(pl.program_id(1) == 0)
    def _():
        st_ref[...] = jnp.zeros_like(st_ref)

    _ret_direction(0, qf, kf, vf, cosf, sinf, kdec_ref, qdec_ref, cd_ref, dmat_ref, st_ref, of_ref)
    _ret_direction(1, qb, kb, vb, cosb, sinb, kdec_ref, qdec_ref, cd_ref, dmat_ref, st_ref, ob_ref)


def _ret_scan(lay, p, decay_logit):
    n = lay.n
    cos_t, sin_t = _rope_tables(lay)
    lgx = jnp.repeat(decay_logit, RET_DIM, axis=-1).reshape(2, 1, RET_WIDTH)
    lgp = jnp.pad(decay_logit, ((0, 0), (0, LANES - RET_HEADS))).reshape(2, 1, LANES)

    def specs(block_of):
        def blk(cb):
            return pl.BlockSpec((SCAN_BLK, RET_WIDTH), lambda bi, s: (lay.scan_row_block(bi, block_of(s)), cb))
        tab = pl.BlockSpec((SCAN_BLK, LANES), lambda bi, s: (block_of(s), 0))
        return [blk(P_Q // RET_WIDTH), blk(P_K // RET_WIDTH), blk(P_V // RET_WIDTH), tab, tab]

    def out_spec(block_of):
        return pl.BlockSpec((SCAN_BLK, RET_WIDTH), lambda bi, s: (lay.scan_row_block(bi, block_of(s)), 0))

    const3 = lambda bi, s: (0, 0, 0)
    return pl.pallas_call(
        _ret_kernel,
        out_shape=(jax.ShapeDtypeStruct((n, RET_WIDTH), F32), jax.ShapeDtypeStruct((n, RET_WIDTH), F32)),
        grid=(lay.b, lay.nblk),
        in_specs=specs(lay.fwd_block) + specs(lay.bwd_block)
        + [pl.BlockSpec((2, 1, RET_WIDTH), const3), pl.BlockSpec((2, 1, LANES), const3)],
        out_specs=(out_spec(lay.fwd_block), out_spec(lay.bwd_block)),
        scratch_shapes=[pltpu.VMEM((2, RET_HEADS // HEAD_GROUP, GROUP_W, GROUP_W), F32),
                        pltpu.VMEM((2, CHUNK, RET_WIDTH), F32),
                        pltpu.VMEM((2, CHUNK, RET_WIDTH), F32),
                        pltpu.VMEM((2, 1, RET_WIDTH), F32),
                        pltpu.VMEM((2, CHUNK, RET_HEADS * CHUNK), F32)],
        compiler_params=_cparams(("arbitrary", "arbitrary")),
        name="ret_scan",
    )(p, p, p, cos_t, sin_t, p, p, p, cos_t, sin_t, lgx, lgp)


MERGE_SUB = 2


def _merge_kernel(lay, tm, emit_h2, *refs):
    for k in range(MERGE_SUB):
        _merge_tile(lay, tm, emit_h2, pl.program_id(0) * MERGE_SUB + k, k, *refs)


def _merge_tile(lay, tm, emit_h2, i, k, x_ref, m_ref, gate_ref, z_ref, g_ref, um_ref, up_ref, un_ref,
                sf_ref, sb_ref, rf_ref, rb_ref, sng_ref, band_ref, pw_ref, ps_ref, wb_ref, wo_ref, n2g_ref,
                wrh_ref, wrl_ref, *rest):
    if emit_h2:
        xo_ref, h2_ref, lg_ref = rest
    else:
        (xo_ref,) = rest
    rows = pl.ds(k * tm, tm)
    ctx_tiles = lay.n_ctx_rows // tm
    per_ctx = lay.ctx // tm
    per_lat = lay.seq // tm
    in_ctx = i < ctx_tiles
    t_in_seg = jnp.where(in_ctx, i % per_ctx, (i - ctx_tiles) % per_lat)
    seg_tiles = jnp.where(in_ctx, per_ctx, per_lat)
    seg_len = jnp.where(in_ctx, lay.ctx, lay.seq)
    pos = t_in_seg * tm + lax.broadcasted_iota(jnp.int32, (tm, 1), 0)

    ys = (sf_ref[rows, :] + sb_ref[rows, :]) * _silu(z_ref[rows, :].astype(F32))
    s_br = ys * lax.rsqrt(jnp.mean(ys * ys, axis=-1, keepdims=True) + EPS) * sng_ref[...]

    zb = jnp.zeros((), BF16)
    fill = jnp.zeros((SHIFT_PAD - HALO, POOL_WIDTH), BF16)
    um = um_ref[rows, :]
    before = up_ref[...] if k == 0 else um_ref[k * tm - HALO:k * tm, :]
    after = un_ref[...] if k == MERGE_SUB - 1 else um_ref[(k + 1) * tm:(k + 1) * tm + HALO, :]
    u_ext = jnp.concatenate([fill, jnp.where(t_in_seg == 0, zb, before), um,
                             jnp.where(t_in_seg == seg_tiles - 1, zb, after), fill], axis=0)
    pooled = []
    for gi, w in enumerate(POOL_WINDOWS):
        left = w // 2
        right = w - 1 - left
        ls = slice(gi * POOL_GROUP, (gi + 1) * POOL_GROUP)
        tot = jnp.dot(band_ref[gi], u_ext[:, ls], preferred_element_type=F32)
        cnt = (jnp.minimum(pos + right, seg_len - 1) + 1 - jnp.maximum(pos - left, 0)).astype(F32)
        mixed = tot / cnt - um[:, ls].astype(F32)
        pooled.append(jnp.dot(mixed.astype(BF16), pw_ref[gi], preferred_element_type=F32))
    p_br = jnp.concatenate(pooled, axis=-1) * ps_ref[...]

    yr = rf_ref[rows, :] + rb_ref[rows, :]
    normed = []
    for h in range(RET_HEADS):
        yh = yr[:, h * RET_DIM:(h + 1) * RET_DIM]
        mu = jnp.mean(yh, axis=-1, keepdims=True)
        dv = yh - mu
        var = jnp.mean(dv * dv, axis=-1, keepdims=True)
        normed.append(dv * lax.rsqrt(var + EPS))
    r_br = jnp.concatenate(normed, axis=-1) * _silu(g_ref[rows, :].astype(F32))

    acc = None
    for bi, br in enumerate((s_br, p_br, r_br)):
        gate = _sigmoid(gate_ref[rows, bi * D_MODEL:(bi + 1) * D_MODEL].astype(F32))
        term = gate * jnp.dot(br.astype(BF16), wb_ref[bi], preferred_element_type=F32)
        acc = term if acc is None else acc + term
    mix = jnp.dot(acc.astype(BF16), wo_ref[...], preferred_element_type=F32)
    xn = x_ref[rows, :] + m_ref[2] * mix
    xo_ref[rows, :] = xn
    if emit_h2:
        y = xn * lax.rsqrt(jnp.mean(xn * xn, axis=-1, keepdims=True) + EPS) * n2g_ref[...]
        h2 = y * (1.0 + m_ref[4]) + m_ref[3]
        h2_ref[rows, :] = h2
        hh = h2.astype(BF16)
        hl = (h2 - hh.astype(F32)).astype(BF16)
        wh, wl = wrh_ref[...], wrl_ref[...]
        lg_ref[rows, :] = (jnp.dot(hh, wh, preferred_element_type=F32)
                           + (jnp.dot(hh, wl, preferred_element_type=F32)
                              + jnp.dot(hl, wh, preferred_element_type=F32)))


def _merge(lay, x, mods, p, ssd_f, ssd_b, ret_f, ret_b, ssd_norm_g, pool_w, pool_scale, w_branch, w_out,
           norm2_g, w_router_pad, emit_h2):
    n, d = x.shape
    tm = _row_tile(lay, 256)
    bt = MERGE_SUB * tm
    assert lay.n_ctx_rows % bt == 0 and lay.seq % bt == 0
    n_halo = n // HALO
    per = bt // HALO
    pool_blk = P_POOL // POOL_WIDTH
    wr_hi = w_router_pad.astype(BF16)
    wr_lo = (w_router_pad - wr_hi.astype(F32)).astype(BF16)
    bands = jnp.asarray(_band_consts(tm, [(-(w // 2), w - 1 - w // 2) for w in POOL_WINDOWS]), BF16)
    row = lambda i: (i, 0)
    const2 = lambda i: (0, 0)
    const3 = lambda i: (0, 0, 0)
    in_specs = [
        pl.BlockSpec((bt, d), row),
        pl.BlockSpec((None, 6, 1, d), lambda i: (lay.mod_row(i, bt), 0, 0, 0)),
        pl.BlockSpec((bt, N_BRANCH * d), lambda i: (i, P_GATE // (N_BRANCH * d))),
        pl.BlockSpec((bt, SSD_INNER), lambda i: (i, P_Z // SSD_INNER)),
        pl.BlockSpec((bt, RET_WIDTH), lambda i: (i, P_G // RET_WIDTH)),
        pl.BlockSpec((bt, POOL_WIDTH), lambda i: (i, pool_blk)),
        pl.BlockSpec((HALO, POOL_WIDTH), lambda i: (jnp.maximum(i * per - 1, 0), pool_blk)),
        pl.BlockSpec((HALO, POOL_WIDTH), lambda i: (jnp.minimum(i * per + per, n_halo - 1), pool_blk)),
        pl.BlockSpec((bt, SSD_INNER), row),
        pl.BlockSpec((bt, SSD_INNER), row),
        pl.BlockSpec((bt, RET_WIDTH), row),
        pl.BlockSpec((bt, RET_WIDTH), row),
        pl.BlockSpec((1, SSD_INNER), const2),
        pl.BlockSpec((len(POOL_WINDOWS), tm, tm + 2 * SHIFT_PAD), const3),
        pl.BlockSpec((len(POOL_WINDOWS), POOL_GROUP, POOL_GROUP), const3),
        pl.BlockSpec((1, POOL_WIDTH), const2),
        pl.BlockSpec((N_BRANCH, SSD_INNER, d), const3),
        pl.BlockSpec((d, d), const2),
        pl.BlockSpec((1, d), const2),
        pl.BlockSpec((d, LANES), const2),
        pl.BlockSpec((d, LANES), const2),
    ]
    out_shape = [jax.ShapeDtypeStruct((n, d), F32)]
    out_specs = [pl.BlockSpec((bt, d), row)]
    if emit_h2:
        out_shape += [jax.ShapeDtypeStruct((n, d), F32), jax.ShapeDtypeStruct((n, LANES), F32)]
        out_specs += [pl.BlockSpec((bt, d), row), pl.BlockSpec((bt, LANES), row)]
    return pl.pallas_call(
        functools.partial(_merge_kernel, lay, tm, emit_h2),
        out_shape=tuple(out_shape),
        grid=(n // bt,),
        in_specs=in_specs,
        out_specs=tuple(out_specs),
        compiler_params=_cparams(("arbitrary",)),
        name="merge_h2" if emit_h2 else "merge",
    )(x, mods, p, p, p, p, p, p, ssd_f, ssd_b, ret_f, ret_b, ssd_norm_g.reshape(1, -1),
      bands, pool_w.astype(BF16), pool_scale.reshape(1, -1), w_branch.astype(BF16), w_out.astype(BF16),
      norm2_g.reshape(1, -1), wr_hi, wr_lo)


def _ffn_kernel(x_ref, m_ref, g_ref, wg_ref, wu_ref, w2_ref, o_ref, h_ref, acc_ref):
    j = pl.program_id(1)

    @pl.when(j == 0)
    def _():
        x = x_ref[...]
        y = x * lax.rsqrt(jnp.mean(x * x, axis=-1, keepdims=True) + EPS) * g_ref[...]
        h_ref[...] = (y * (1.0 + m_ref[4]) + m_ref[3]).astype(BF16)
        acc_ref[...] = jnp.zeros_like(acc_ref)

    h = h_ref[...]
    gt = jnp.dot(h, wg_ref[...], preferred_element_type=F32)
    up = jnp.dot(h, wu_ref[...], preferred_element_type=F32)
    acc_ref[...] += jnp.dot((_silu(gt) * up).astype(BF16), w2_ref[...], preferred_element_type=F32)

    @pl.when(j == pl.num_programs(1) - 1)
    def _():
        o_ref[...] = x_ref[...] + m_ref[5] * acc_ref[...]


def _ffn_dense(lay, x, mods, g, w13, w2):
    n, d = x.shape
    ff = w2.shape[0]
    tm = _row_tile(lay, 1024)
    tf = 1408
    nf = ff // tf
    return pl.pallas_call(
        _ffn_kernel,
        out_shape=jax.ShapeDtypeStruct((n, d), F32),
        grid=(n // tm, nf),
        in_specs=[
            pl.BlockSpec((tm, d), lambda i, j: (i, 0)),
            pl.BlockSpec((None, 6, 1, d), lambda i, j: (lay.mod_row(i, tm), 0, 0, 0)),
            pl.BlockSpec((1, d), lambda i, j: (0, 0)),
            pl.BlockSpec((d, tf), lambda i, j: (0, j)),
            pl.BlockSpec((d, tf), lambda i, j: (0, j + nf)),
            pl.BlockSpec((tf, d), lambda i, j: (j, 0)),
        ],
        out_specs=pl.BlockSpec((tm, d), lambda i, j: (i, 0)),
        scratch_shapes=[pltpu.VMEM((tm, d), BF16), pltpu.VMEM((tm, d), F32)],
        compiler_params=_cparams(("arbitrary", "arbitrary")),
        name="ffn_dense",
    )(x, mods, g, w13, w13, w2)


ROUTE_TM = 512
R_E0, R_E1, R_RANK0, R_RANK1, R_W0, R_W1 = range(6)


def _route_kernel(lg_ref, tri_ref, o_ref, cnt_ref, run_ref):
    i = pl.program_id(0)

    @pl.when(i == 0)
    def _():
        run_ref[...] = jnp.zeros_like(run_ref)

    tm = lg_ref.shape[0]
    lane = lax.broadcasted_iota(jnp.int32, (tm, LANES), 1).astype(F32)
    lg = jnp.where(lane < N_EXPERTS, lg_ref[...], -jnp.inf)
    m1 = jnp.max(lg, axis=-1, keepdims=True)
    i1 = jnp.min(jnp.where(lg == m1, lane, float(LANES)), axis=-1, keepdims=True)
    l2 = jnp.where(lane == i1, -jnp.inf, lg)
    m2 = jnp.max(l2, axis=-1, keepdims=True)
    i2 = jnp.min(jnp.where(l2 == m2, lane, float(LANES)), axis=-1, keepdims=True)
    e21 = jnp.exp(m2 - m1)
    w1 = 1.0 / (1.0 + e21)
    w2 = e21 * w1
    sel1, sel2 = lane == i1, lane == i2
    memb = jnp.where(sel1, 1.0, jnp.where(sel2, 1.0, 0.0))
    before = jnp.dot(tri_ref[...], memb.astype(BF16), preferred_element_type=F32) + run_ref[...]
    r1 = jnp.sum(jnp.where(sel1, before, 0.0), axis=-1, keepdims=True)
    r2 = jnp.sum(jnp.where(sel2, before, 0.0), axis=-1, keepdims=True)
    rec = jnp.zeros((tm, LANES), F32)
    for k, v in ((R_E0, i1), (R_E1, i2), (R_RANK0, r1), (R_RANK1, r2), (R_W0, w1), (R_W1, w2)):
        rec = jnp.where(lane == k, v, rec)
    o_ref[...] = rec
    run_ref[...] += jnp.sum(memb, axis=0, keepdims=True)
    cnt_ref[...] = jnp.broadcast_to(run_ref[...], cnt_ref.shape)


def _route(logits):
    n = logits.shape[0]
    tm = ROUTE_TM
    i = np.arange(tm)
    tri = jnp.asarray((i[None, :] < i[:, None]).astype(np.float32), BF16)
    return pl.pallas_call(
        _route_kernel,
        out_shape=(jax.ShapeDtypeStruct((n, LANES), F32), jax.ShapeDtypeStruct((8, LANES), F32)),
        grid=(n // tm,),
        in_specs=[pl.BlockSpec((tm, LANES), lambda i: (i, 0)), pl.BlockSpec((tm, tm), lambda i: (0, 0))],
        out_specs=(pl.BlockSpec((tm, LANES), lambda i: (i, 0)), pl.BlockSpec((8, LANES), lambda i: (0, 0))),
        scratch_shapes=[pltpu.VMEM((1, LANES), F32)],
        compiler_params=_cparams(("arbitrary",)),
        name="moe_route",
    )(logits, tri)


MOE_BM = 1024


def _slot_plan(rec, counts):
    n = rec.shape[0]
    cnt = counts[0, :N_EXPERTS].astype(jnp.int32)
    padded = (cnt + MOE_BM - 1) // MOE_BM * MOE_BM
    pad_ends = jnp.cumsum(padded)
    pad_starts = pad_ends - padded
    n_blocks = -(-(n * TOP_K + N_EXPERTS * (MOE_BM - 1)) // MOE_BM)
    n_used = pad_ends[-1] // MOE_BM
    blk = jnp.minimum(jnp.arange(n_blocks, dtype=jnp.int32), n_used - 1)
    block_e = jnp.minimum(jnp.searchsorted(pad_ends, blk * MOE_BM, side='right'), N_EXPERTS - 1)
    e = rec[:, R_E0:R_E1 + 1].astype(jnp.int32)
    rank = rec[:, R_RANK0:R_RANK1 + 1].astype(jnp.int32)
    start = jnp.zeros_like(e)
    for k in range(N_EXPERTS):
        start = jnp.where(e == k, pad_starts[k], start)
    dest = start + rank
    last_blk = jnp.where(cnt > 0, pad_ends // MOE_BM - 1, -1).astype(jnp.int32)
    return dest, block_e.astype(jnp.int32), n_used.astype(jnp.int32).reshape(1), last_blk, n_blocks


DISPATCH_TM = 256


def _dispatch_kernel(last_ref, nb_ref, dest_ref, h_ref, xb_ref, zero_ref, sem, zsem):
    i = pl.program_id(0)
    tm = DISPATCH_TM
    n_blocks = xb_ref.shape[0] // MOE_BM

    def clear_block(blk):
        start = pl.multiple_of(blk * MOE_BM, MOE_BM)
        cp = pltpu.make_async_copy(zero_ref, xb_ref.at[pl.ds(start, MOE_BM), :], zsem)
        cp.start()
        cp.wait()

    @pl.when(i == 0)
    def _():
        zero_ref[...] = jnp.zeros_like(zero_ref)
        for e in range(N_EXPERTS):
            @pl.when(last_ref[e] >= 0)
            def _():
                clear_block(last_ref[e])

            @pl.when(nb_ref[0] + e < n_blocks)
            def _():
                clear_block(nb_ref[0] + e)

    for r in range(tm):
        for k in range(TOP_K):
            pltpu.make_async_copy(h_ref.at[pl.ds(r, 1), :],
                                  xb_ref.at[pl.ds(dest_ref[0, TOP_K * r + k], 1), :], sem).start(priority=k)
    for k in range(TOP_K):
        pltpu.make_async_copy(h_ref, xb_ref.at[pl.ds(0, tm), :], sem).wait()


def _dispatch(h2, dest, last_blk, n_used, n_blocks):
    n, d = h2.shape
    tm = DISPATCH_TM
    cap = n_blocks * MOE_BM
    grid_spec = pltpu.PrefetchScalarGridSpec(
        num_scalar_prefetch=2,
        grid=(n // tm,),
        in_specs=[pl.BlockSpec((None, 1, TOP_K * tm), lambda i, lb, nb: (i, 0, 0), memory_space=pltpu.SMEM),
                  pl.BlockSpec((tm, d), lambda i, lb, nb: (i, 0))],
        out_specs=pl.BlockSpec(memory_space=pl.ANY),
        scratch_shapes=[pltpu.VMEM((MOE_BM, d), F32), pltpu.SemaphoreType.DMA, pltpu.SemaphoreType.DMA],
    )
    return pl.pallas_call(
        _dispatch_kernel,
        out_shape=jax.ShapeDtypeStruct((cap, d), F32),
        grid_spec=grid_spec,
        compiler_params=_cparams(("arbitrary",)),
        name="moe_dispatch",
    )(last_blk, n_used, dest.reshape(n // tm, 1, TOP_K * tm), h2)


MOE_TF = 512


def _moe_kernel(be_ref, nb_ref, x_ref, wg_ref, wu_ref, w2_ref, o_ref, h_ref, acc_ref):
    i, j = pl.program_id(0), pl.program_id(1)

    @pl.when(i < nb_ref[0])
    def _():
        @pl.when(j == 0)
        def _():
            h_ref[...] = x_ref[...].astype(BF16)
            acc_ref[...] = jnp.zeros_like(acc_ref)

        h = h_ref[...]
        gt = jnp.dot(h, wg_ref[...].astype(BF16), preferred_element_type=F32)
        up = jnp.dot(h, wu_ref[...].astype(BF16), preferred_element_type=F32)
        acc_ref[...] += jnp.dot((_silu(gt) * up).astype(BF16), w2_ref[...].astype(BF16),
                                preferred_element_type=F32)

        @pl.when(j == pl.num_programs(1) - 1)
        def _():
            o_ref[...] = acc_ref[...]

    @pl.when(jnp.logical_and(i >= nb_ref[0], j == pl.num_programs(1) - 1))
    def _():
        o_ref[...] = jnp.zeros_like(o_ref)


def _moe_blocks(xb, block_e, n_used, w13, w2, li):
    cap, d = xb.shape
    _, ne, ff, _ = w2.shape
    tf = MOE_TF
    nf = ff // tf
    n_blocks = cap // MOE_BM

    def row(i, j, be, nb):
        return (jnp.minimum(i, nb[0] - 1), 0)

    def jj(i, j, nb):
        return jnp.where(i < nb[0], j, nf - 1)

    grid_spec = pltpu.PrefetchScalarGridSpec(
        num_scalar_prefetch=2,
        grid=(n_blocks, nf),
        in_specs=[
            pl.BlockSpec((MOE_BM, d), row),
            pl.BlockSpec((None, None, d, tf), lambda i, j, be, nb: (li, be[i], 0, jj(i, j, nb))),
            pl.BlockSpec((None, None, d, tf), lambda i, j, be, nb: (li, be[i], 0, jj(i, j, nb) + nf)),
            pl.BlockSpec((None, None, tf, d), lambda i, j, be, nb: (li, be[i], jj(i, j, nb), 0)),
        ],
        out_specs=pl.BlockSpec((MOE_BM, d), lambda i, j, be, nb: (i, 0)),
        scratch_shapes=[pltpu.VMEM((MOE_BM, d), BF16), pltpu.VMEM((MOE_BM, d), F32)],
    )
    return pl.pallas_call(
        _moe_kernel,
        out_shape=jax.ShapeDtypeStruct((cap, d), F32),
        grid_spec=grid_spec,
        compiler_params=_cparams(("arbitrary", "arbitrary")),
        name="moe_experts",
    )(block_e, n_used, xb, w13, w13, w2)


COMBINE_TM = 256


def _combine_kernel(final, dest_ref, x_ref, m_ref, rec_ref, yb_ref, *rest):
    if final:
        g_ref, o_ref, buf_ref, sem = rest
    else:
        o_ref, buf_ref, sem = rest
    tm = COMBINE_TM

    for r in range(tm):
        for k in range(TOP_K):
            pltpu.make_async_copy(yb_ref.at[pl.ds(dest_ref[0, TOP_K * r + k], 1), :],
                                  buf_ref.at[k, pl.ds(r, 1), :], sem).start(priority=k)
    for k in range(TOP_K):
        pltpu.make_async_copy(yb_ref.at[pl.ds(0, tm), :], buf_ref.at[k], sem).wait()
    rec = rec_ref[...]
    y = rec[:, R_W0:R_W0 + 1] * buf_ref[0] + rec[:, R_W1:R_W1 + 1] * buf_ref[1]
    xn = x_ref[...] + m_ref[5] * y
    if final:
        xn = xn * lax.rsqrt(jnp.mean(xn * xn, axis=-1, keepdims=True) + EPS) * g_ref[...]
    o_ref[...] = xn


def _combine(lay, x, mods, rec, dest, yb, final_g=None):
    n, d = x.shape
    tm = COMBINE_TM
    final = final_g is not None
    off = lay.n_ctx_rows // tm if final else 0
    n_out = n - off * tm
    row = lambda i: (i + off, 0)
    in_specs = [pl.BlockSpec((None, 1, TOP_K * tm), lambda i: (i + off, 0, 0), memory_space=pltpu.SMEM),
                pl.BlockSpec((tm, d), row),
                pl.BlockSpec((None, 6, 1, d), lambda i: (lay.mod_row(i + off, tm), 0, 0, 0)),
                pl.BlockSpec((tm, LANES), row),
                pl.BlockSpec(memory_space=pl.ANY)]
    args = [dest.reshape(n // tm, 1, TOP_K * tm), x, mods, rec, yb]
    if final:
        in_specs.append(pl.BlockSpec((1, d), lambda i: (0, 0)))
        args.append(final_g.reshape(1, -1))
    return pl.pallas_call(
        functools.partial(_combine_kernel, final),
        out_shape=jax.ShapeDtypeStruct((n_out, d), F32),
        grid=(n_out // tm,),
        in_specs=in_specs,
        out_specs=pl.BlockSpec((tm, d), lambda i: (i, 0)),
        scratch_shapes=[pltpu.VMEM((TOP_K, tm, d), F32), pltpu.SemaphoreType.DMA],
        compiler_params=_cparams(("arbitrary",)),
        name="moe_combine_final" if final else "moe_combine",
    )(*args)


def _final_kernel(x_ref, g_ref, o_ref):
    x = x_ref[...]
    o_ref[...] = x * lax.rsqrt(jnp.mean(x * x, axis=-1, keepdims=True) + EPS) * g_ref[...]


def _final_norm(lay, x, g):
    n, d = x.shape
    tm = _row_tile(lay, 1024)
    off = lay.n_ctx_rows // tm
    n_lat = lay.b * lay.seq
    return pl.pallas_call(
        _final_kernel,
        out_shape=jax.ShapeDtypeStruct((n_lat, d), F32),
        grid=(n_lat // tm,),
        in_specs=[pl.BlockSpec((tm, d), lambda i: (i + off, 0)), pl.BlockSpec((1, d), lambda i: (0, 0))],
        out_specs=pl.BlockSpec((tm, d), lambda i: (i, 0)),
        compiler_params=_cparams(("arbitrary",)),
        name="final_norm",
    )(x, g.reshape(1, -1))


def _permute_w_in(w):
    d = w.shape[0]
    parts = [w[:, COL_GATE:COL_GATE + N_BRANCH * D_MODEL], w[:, COL_Q:COL_G_END],
             w[:, COL_POOL:COL_POOL + POOL_WIDTH], w[:, COL_Z:COL_Z + SSD_INNER],
             w[:, COL_XBC:COL_XBC + SSD_XBC]]
    w_dt = jnp.pad(w[:, COL_DT:COL_DT + 2 * SSD_HEADS], ((0, 0), (0, LANES - 2 * SSD_HEADS)))
    return jnp.concatenate(parts, axis=1).astype(BF16), w_dt.astype(BF16)


def kernel(x, c, ctx, c_ctx, ada_w, ada_b, norm1_g, norm2_g, w_in, ssd_conv_w, ssd_conv_b, ssd_dt_bias,
           ssd_a_log, ssd_d, ssd_norm_g, pool_w, pool_scale, ret_decay_logit, w_branch, w_out,
           ffn_w13, ffn_w2, moe_router, moe_w13, moe_w2, final_norm_g):
    b, seq, d = x.shape
    ctx_len = ctx.shape[1]
    depth = w_in.shape[0]
    lay = _Layout(b, ctx_len, seq)

    cvec = jnp.concatenate([c, c_ctx[None, :], jnp.zeros((8 - b - 1, d), F32)], axis=0)
    mods_all = _ada_all(cvec, ada_w, ada_b).reshape(depth, 8, 6, 1, d)
    xa = jnp.concatenate([ctx.reshape(-1, d), x.reshape(-1, d)], axis=0)

    for layer in range(depth):
        mods = mods_all[layer]
        w_main, w_dt = _permute_w_in(w_in[layer])
        p, dt_raw = _in_proj(lay, xa, mods, norm1_g[layer].reshape(1, -1), w_main, w_dt)
        ssd_f, ssd_b = _ssd_scan(lay, p, dt_raw, ssd_conv_w[layer], ssd_conv_b[layer], ssd_dt_bias[layer],
                                 ssd_a_log[layer], ssd_d[layer])
        ret_f, ret_b = _ret_scan(lay, p, ret_decay_logit[layer])
        is_moe = layer % 2 == 1
        if is_moe:
            w_r = jnp.pad(moe_router[layer // 2], ((0, 0), (0, LANES - N_EXPERTS)))
        else:
            w_r = jnp.zeros((d, LANES), F32)
        outs = _merge(lay, xa, mods, p, ssd_f, ssd_b, ret_f, ret_b, ssd_norm_g[layer], pool_w[layer],
                      pool_scale[layer], w_branch[layer], w_out[layer], norm2_g[layer], w_r, is_moe)
        if not is_moe:
            (xa,) = outs
            xa = _ffn_dense(lay, xa, mods, norm2_g[layer].reshape(1, -1),
                            ffn_w13[layer // 2].astype(BF16), ffn_w2[layer // 2].astype(BF16))
        else:
            xa, h2, logits = outs
            rec, counts = _route(logits)
            dest, block_e, n_used, last_blk, n_blocks = _slot_plan(rec, counts)
            xb = _dispatch(h2, dest, last_blk, n_used, n_blocks)
            yb = _moe_blocks(xb, block_e, n_used, moe_w13, moe_w2, layer // 2)
            if layer == depth - 1:
                return _combine(lay, xa, mods, rec, dest, yb, final_norm_g).reshape(b, seq, d)
            xa = _combine(lay, xa, mods, rec, dest, yb)
    return _final_norm(lay, xa, final_norm_g).reshape(b, seq, d)
```

```python
import functools

import numpy as np
import jax
import jax.numpy as jnp
from jax import lax
from jax.experimental import pallas as pl
from jax.experimental.pallas import tpu as pltpu

F32 = jnp.float32
BF16 = jnp.bfloat16
HIGHEST = lax.Precision.HIGHEST

D_MODEL = 1024
GRID_W = 64
EPS = 1e-6
CHUNK = 128
SCAN_SUB = 2
SCAN_BLK = SCAN_SUB * CHUNK
HALO = 16
SSD_HEADS = 8
SSD_HEAD_DIM = 64
SSD_INNER = 512
SSD_STATE = 64
SSD_GROUPS = 2
SSD_CONV = 5
SSD_XBC = 768
POOL_WINDOWS = (2, 4, 8, 16)
POOL_WIDTH = 512
POOL_GROUP = 128
RET_HEADS = 8
RET_DIM = 64
RET_WIDTH = 512
ROPE_BASE = 10000.0
N_BRANCH = 3
N_EXPERTS = 8
TOP_K = 2
LANES = 128
HEAD_GROUP = 4
GROUP_W = HEAD_GROUP * 64

COL_Z = 0
COL_XBC = 512
COL_DT = 1280
COL_POOL = 1296
COL_Q = 1808
COL_G_END = 3856
COL_GATE = 3856
IN_COLS = 6928
P_GATE = 0
P_Q = 3072
P_K = 3584
P_V = 4096
P_G = 4608
P_POOL = 5120
P_Z = 5632
P_XBC = 6144
P_COLS = 6912

VMEM_LIMIT = 56 * 1024 * 1024


def _sigmoid(v):
    return 0.5 * jnp.tanh(0.5 * v) + 0.5


def _silu(v):
    return v * _sigmoid(v)


def _softplus(v):
    return jnp.maximum(v, 0.0) + jnp.log1p(jnp.exp(-jnp.abs(v)))


def _log_sigmoid(v):
    return -_softplus(-v)


def _cparams(sem):
    return pltpu.CompilerParams(dimension_semantics=sem, vmem_limit_bytes=VMEM_LIMIT)


def _split3(x):
    hi = x.astype(BF16)
    r = x - hi.astype(F32)
    mid = r.astype(BF16)
    lo = (r - mid.astype(F32)).astype(BF16)
    return hi, mid, lo


def _dot_sel_right(x, m):
    return sum(jnp.dot(part, m, preferred_element_type=F32) for part in _split3(x))


def _dot_sel_left(m, x):
    return sum(jnp.dot(m, part, preferred_element_type=F32) for part in _split3(x))


SHIFT_PAD = 64


def _band_consts(rows, windows):
    m = np.zeros((len(windows), rows, rows + 2 * SHIFT_PAD), np.float32)
    t = np.arange(rows)
    for i, (lo, hi) in enumerate(windows):
        for o in range(lo, hi + 1):
            m[i, t, SHIFT_PAD + t + o] = 1.0
    return m


def _block_mask(rows, cols, row_blk, col_blk):
    r = lax.broadcasted_iota(jnp.int32, (rows, cols), 0) // row_blk
    c = lax.broadcasted_iota(jnp.int32, (rows, cols), 1) // col_blk
    return r == c


def _ada_kernel(c_ref, w_ref, b_ref, o_ref):
    cv = c_ref[...]
    o_ref[...] = jnp.dot(_silu(cv), w_ref[...], precision=HIGHEST,
                         preferred_element_type=F32) + b_ref[...]


def _ada_all(cvec, ada_w, ada_b):
    depth, d, n6 = ada_w.shape
    tn = 1536
    return pl.pallas_call(
        _ada_kernel,
        out_shape=jax.ShapeDtypeStruct((depth, 8, n6), F32),
        grid=(depth, n6 // tn),
        in_specs=[
            pl.BlockSpec((8, d), lambda l, j: (0, 0)),
            pl.BlockSpec((None, d, tn), lambda l, j: (l, 0, j)),
            pl.BlockSpec((None, 1, tn), lambda l, j: (l, 0, j)),
        ],
        out_specs=pl.BlockSpec((None, 8, tn), lambda l, j: (l, 0, j)),
        compiler_params=_cparams(("arbitrary", "arbitrary")),
        name="ada_mods",
    )(cvec, ada_w, ada_b.reshape(depth, 1, n6))


class _Layout:
    def __init__(self, b, ctx_len, seq):
        self.b, self.ctx, self.seq = b, ctx_len, seq
        self.n_ctx_rows = b * ctx_len
        self.n = b * (ctx_len + seq)
        assert ctx_len % SCAN_BLK == 0 and seq % SCAN_BLK == 0
        self.cblk = ctx_len // SCAN_BLK
        self.lblk = seq // SCAN_BLK
        self.nblk = self.cblk + self.lblk

    def mod_row(self, tile, tm):
        ctx_tiles = self.n_ctx_rows // tm
        per_b = self.seq // tm
        return jnp.where(tile < ctx_tiles, self.b, (tile - ctx_tiles) // per_b)

    def scan_row_block(self, bi, c):
        return jnp.where(c < self.cblk, bi * self.cblk + c,
                         self.b * self.cblk + bi * self.lblk + (c - self.cblk))

    def fwd_block(self, s):
        return s

    def bwd_block(self, s):
        return jnp.where(s < self.cblk, self.cblk - 1 - s, self.nblk - 1 - (s - self.cblk))


def _row_tile(lay, cap):
    tm = cap
    while lay.n_ctx_rows % tm or lay.seq % tm:
        tm //= 2
    return tm


def _in_kernel(x_ref, m_ref, g_ref, w_ref, wdt_ref, o_ref, dt_ref, h_ref):
    @pl.when(pl.program_id(1) == 0)
    def _():
        x = x_ref[...]
        y = x * lax.rsqrt(jnp.mean(x * x, axis=-1, keepdims=True) + EPS) * g_ref[...]
        h_ref[...] = (y * (1.0 + m_ref[1]) + m_ref[0]).astype(BF16)
        dt_ref[...] = jnp.dot(h_ref[...], wdt_ref[...], preferred_element_type=F32)

    o_ref[...] = jnp.dot(h_ref[...], w_ref[...], preferred_element_type=F32).astype(BF16)


def _in_proj(lay, x, mods, g, w, w_dt):
    n, d = x.shape
    tm = _row_tile(lay, 1024)
    tn = 2304
    return pl.pallas_call(
        _in_kernel,
        out_shape=(jax.ShapeDtypeStruct((n, P_COLS), BF16), jax.ShapeDtypeStruct((n, LANES), F32)),
        grid=(n // tm, P_COLS // tn),
        in_specs=[
            pl.BlockSpec((tm, d), lambda i, j: (i, 0)),
            pl.BlockSpec((None, 6, 1, d), lambda i, j: (lay.mod_row(i, tm), 0, 0, 0)),
            pl.BlockSpec((1, d), lambda i, j: (0, 0)),
            pl.BlockSpec((d, tn), lambda i, j: (0, j)),
            pl.BlockSpec((d, LANES), lambda i, j: (0, 0)),
        ],
        out_specs=(pl.BlockSpec((tm, tn), lambda i, j: (i, j)), pl.BlockSpec((tm, LANES), lambda i, j: (i, 0))),
        scratch_shapes=[pltpu.VMEM((tm, d), BF16)],
        compiler_params=_cparams(("arbitrary", "arbitrary")),
        name="in_proj",
    )(x, mods, g, w, w_dt)


def _tri_consts():
    i = np.arange(CHUNK)
    fwd = (i[None, :] <= i[:, None]).astype(np.float32)
    bwd = (i[None, :] >= i[:, None]).astype(np.float32)
    return np.stack([fwd, bwd])


def _expand_consts(heads, width):
    e = np.zeros((2, LANES, heads * width), np.float32)
    for d in range(2):
        for h in range(heads):
            e[d, d * heads + h, h * width:(h + 1) * width] = 1.0
    return e


def _ssd_direction(d, c, lay, xm_ref, xp_ref, xn_ref, dt_ref, tri_ref, exp_ref, cw_ref, cb_ref,
                   dtb_ref, alog_ref, dskip_ref, st_ref, o_ref):
    is_start = jnp.logical_or(c == 0, c == lay.cblk)
    is_end = jnp.logical_or(c == lay.cblk - 1, c == lay.nblk - 1)
    xm = xm_ref[...].astype(F32)
    xw = jnp.concatenate([jnp.where(is_start, 0.0, xp_ref[...].astype(F32)), xm,
                          jnp.where(is_end, 0.0, xn_ref[...].astype(F32))], axis=0)
    mid = SSD_CONV // 2
    acc = cb_ref[...] + cw_ref[mid:mid + 1, :] * xm
    for k in range(SSD_CONV):
        if k != mid:
            shifted = pltpu.roll(xw, (mid - k) % xw.shape[0], 0)[HALO:HALO + SCAN_BLK, :]
            acc = acc + cw_ref[k:k + 1, :] * shifted
    xbc_blk = _silu(acc)
    dt_blk = _softplus(dt_ref[...] + dtb_ref[...])
    for k in (range(SCAN_SUB) if d == 0 else reversed(range(SCAN_SUB))):
        rows = slice(k * CHUNK, (k + 1) * CHUNK)
        _ssd_chunk(d, xbc_blk[rows], dt_blk[rows], tri_ref, exp_ref, alog_ref, dskip_ref, st_ref,
                   o_ref.at[pl.ds(k * CHUNK, CHUNK), :])


def _ssd_chunk(d, xbc, dt_all, tri_ref, exp_ref, alog_ref, dskip_ref, st_ref, o_ref):
    xs = xbc[:, :SSD_INNER]
    bm = xbc[:, SSD_INNER:SSD_INNER + LANES]
    cm = xbc[:, SSD_INNER + LANES:]
    bt = bm.T.astype(BF16)
    top = lax.broadcasted_iota(jnp.int32, (CHUNK, LANES), 0) < SSD_STATE
    zero = jnp.zeros_like(bt)
    bt_bd = jnp.concatenate([jnp.where(top, bt, zero), jnp.where(top, zero, bt)], axis=1)
    cb_all = jnp.dot(cm.astype(BF16), bt_bd, preferred_element_type=F32)

    tri = tri_ref[d]
    acs = _dot_sel_left(tri, dt_all * (-jnp.exp(alog_ref[...])))
    acs_t = acs.T
    dt_x = _dot_sel_right(dt_all, exp_ref[d])
    acs_x = _dot_sel_right(acs, exp_ref[d])
    last = CHUNK - 1 if d == 0 else 0
    tot_x = acs_x[last:last + 1, :]
    u = xs * dt_x
    ud = (u * jnp.exp(tot_x - acs_x)).astype(BF16)
    ub = u.astype(BF16)
    off_x = jnp.exp(acs_x)
    cd_x = jnp.exp(tot_x)

    li = lax.broadcasted_iota(jnp.int32, (CHUNK, CHUNK), 0)
    si = lax.broadcasted_iota(jnp.int32, (CHUNK, CHUNK), 1)
    mask = (si <= li) if d == 0 else (si >= li)
    lane = lax.broadcasted_iota(jnp.int32, (CHUNK, LANES), 1)
    cm_sw = pltpu.roll(cm, SSD_STATE, 1)
    u_mask = _block_mask(HEAD_GROUP * CHUNK, GROUP_W, CHUNK, SSD_HEAD_DIM)
    s_mask = _block_mask(GROUP_W, GROUP_W, SSD_STATE, SSD_HEAD_DIM)
    for g in range(SSD_GROUPS):
        gl = slice(g * GROUP_W, (g + 1) * GROUP_W)
        cb = cb_all[:, g * CHUNK:(g + 1) * CHUNK]
        parts = []
        for hh in range(HEAD_GROUP):
            h = g * HEAD_GROUP + hh
            col = acs_x[:, h * SSD_HEAD_DIM:h * SSD_HEAD_DIM + 1]
            row = acs_t[d * SSD_HEADS + h:d * SSD_HEADS + h + 1, :]
            lm = jnp.exp(jnp.where(mask, col - row, -jnp.inf))
            parts.append((cb * lm).astype(BF16))
        in_g = (lane < SSD_STATE) if g == 0 else (lane >= SSD_STATE)
        c_rep = jnp.where(in_g, cm, cm_sw)
        c_off = jnp.concatenate([c_rep, c_rep], axis=1) * off_x[:, gl]
        parts.append(c_off.astype(BF16))
        lhs = jnp.concatenate(parts, axis=1)
        ub_g = ub[:, gl]
        u_bd = jnp.where(u_mask, jnp.concatenate([ub_g] * HEAD_GROUP, axis=0), jnp.zeros((), BF16))
        st = st_ref[d, g]
        rhs = jnp.concatenate([u_bd, st.astype(BF16)], axis=0)
        y_g = jnp.dot(lhs, rhs, preferred_element_type=F32)
        if d == 0:
            y_g = y_g + dskip_ref[:, gl] * xs[:, gl]
        o_ref[:, gl] = y_g
        bt_g = bt[g * SSD_STATE:(g + 1) * SSD_STATE, :]
        upd = jnp.dot(jnp.concatenate([bt_g] * HEAD_GROUP, axis=0), ud[:, gl], preferred_element_type=F32)
        st_ref[d, g] = st * cd_x[:, gl] + jnp.where(s_mask, upd, 0.0)


def _ssd_kernel(lay, xm_f, xp_f, xn_f, dt_f, xm_b, xp_b, xn_b, dt_b, tri_ref, exp_ref, cw_ref, cb_ref,
                dtb_ref, alog_ref, dskip_ref, of_ref, ob_ref, st_ref):
    s = pl.program_id(1)

    @pl.when(s == 0)
    def _():
        st_ref[...] = jnp.zeros_like(st_ref)

    _ssd_direction(0, lay.fwd_block(s), lay, xm_f, xp_f, xn_f, dt_f, tri_ref, exp_ref, cw_ref, cb_ref,
                   dtb_ref, alog_ref, dskip_ref, st_ref, of_ref)
    _ssd_direction(1, lay.bwd_block(s), lay, xm_b, xp_b, xn_b, dt_b, tri_ref, exp_ref, cw_ref, cb_ref,
                   dtb_ref, alog_ref, dskip_ref, st_ref, ob_ref)


def _halo_specs(lay, width, col_block, block_of):
    n_halo = lay.n // HALO
    per = SCAN_BLK // HALO

    def main(bi, s):
        return (lay.scan_row_block(bi, block_of(s)), col_block)

    def prev(bi, s):
        return (jnp.maximum(lay.scan_row_block(bi, block_of(s)) * per - 1, 0), col_block)

    def nxt(bi, s):
        return (jnp.minimum(lay.scan_row_block(bi, block_of(s)) * per + per, n_halo - 1), col_block)

    return [pl.BlockSpec((SCAN_BLK, width), main), pl.BlockSpec((HALO, width), prev),
            pl.BlockSpec((HALO, width), nxt)]


def _ssd_scan(lay, p, dt_raw, conv_w, conv_b, dt_bias, a_log, d_skip):
    n = lay.n
    pad = LANES - 2 * SSD_HEADS
    dtb = jnp.pad(dt_bias.reshape(1, -1), ((0, 0), (0, pad)))
    alog = jnp.pad(a_log.reshape(1, -1), ((0, 0), (0, pad)))
    dskip = jnp.repeat(d_skip, SSD_HEAD_DIM).reshape(1, SSD_INNER)
    tri = jnp.asarray(_tri_consts(), BF16)
    expand = jnp.asarray(_expand_consts(SSD_HEADS, SSD_HEAD_DIM), BF16)
    xbc_blk = P_XBC // SSD_XBC

    def dt_spec(block_of):
        return pl.BlockSpec((SCAN_BLK, LANES), lambda bi, s: (lay.scan_row_block(bi, block_of(s)), 0))

    def out_spec(block_of):
        return pl.BlockSpec((SCAN_BLK, SSD_INNER), lambda bi, s: (lay.scan_row_block(bi, block_of(s)), 0))

    const2 = lambda bi, s: (0, 0)
    const3 = lambda bi, s: (0, 0, 0)
    in_specs = (
        _halo_specs(lay, SSD_XBC, xbc_blk, lay.fwd_block) + [dt_spec(lay.fwd_block)]
        + _halo_specs(lay, SSD_XBC, xbc_blk, lay.bwd_block) + [dt_spec(lay.bwd_block)]
        + [pl.BlockSpec((2, CHUNK, CHUNK), const3),
           pl.BlockSpec((2, LANES, SSD_INNER), const3),
           pl.BlockSpec((SSD_CONV, SSD_XBC), const2),
           pl.BlockSpec((1, SSD_XBC), const2),
           pl.BlockSpec((1, LANES), const2),
           pl.BlockSpec((1, LANES), const2),
           pl.BlockSpec((1, SSD_INNER), const2)])
    return pl.pallas_call(
        functools.partial(_ssd_kernel, lay),
        out_shape=(jax.ShapeDtypeStruct((n, SSD_INNER), F32), jax.ShapeDtypeStruct((n, SSD_INNER), F32)),
        grid=(lay.b, lay.nblk),
        in_specs=in_specs,
        out_specs=(out_spec(lay.fwd_block), out_spec(lay.bwd_block)),
        scratch_shapes=[pltpu.VMEM((2, SSD_GROUPS, GROUP_W, GROUP_W), F32)],
        compiler_params=_cparams(("arbitrary", "arbitrary")),
        name="ssd_scan",
    )(p, p, p, dt_raw, p, p, p, dt_raw, tri, expand, conv_w, conv_b.reshape(1, -1), dtb, alog, dskip)


def _rope_tables(lay):
    n_axis = RET_DIM // 4
    t = np.arange(lay.seq)
    inv = ROPE_BASE ** (-np.arange(n_axis, dtype=np.float32) / n_axis)
    row = (t // GRID_W).astype(np.float32)
    colp = (t % GRID_W).astype(np.float32)
    ang = jnp.concatenate([jnp.asarray(row)[:, None] * inv, jnp.asarray(colp)[:, None] * inv], axis=-1)
    cos, sin = jnp.cos(ang), jnp.sin(ang)
    cos_l = jnp.concatenate([cos, cos, cos, cos], axis=-1)
    sin_l = jnp.concatenate([-sin, sin, -sin, sin], axis=-1)
    cos_t = jnp.concatenate([jnp.ones((lay.ctx, LANES), F32), cos_l], axis=0)
    sin_t = jnp.concatenate([jnp.zeros((lay.ctx, LANES), F32), sin_l], axis=0)
    return cos_t, sin_t


def _rope(xv, cos, sin):
    lane = lax.broadcasted_iota(jnp.int32, (xv.shape[0], LANES), 1)
    first_half = (lane % RET_DIM) < (RET_DIM // 2)
    out = []
    for j in range(RET_WIDTH // LANES):
        v = xv[:, j * LANES:(j + 1) * LANES]
        swapped = jnp.where(first_half, pltpu.roll(v, LANES - RET_DIM // 2, 1),
                            pltpu.roll(v, RET_DIM // 2, 1))
        out.append(v * cos + swapped * sin)
    return jnp.concatenate(out, axis=-1)


def _ret_tables(lgx_ref, lgp_ref, kdec_ref, qdec_ref, cd_ref, dmat_ref):
    idx = lax.broadcasted_iota(jnp.int32, (CHUNK, 1), 0).astype(F32)
    ii = lax.broadcasted_iota(jnp.int32, (CHUNK, CHUNK), 0)
    mi = lax.broadcasted_iota(jnp.int32, (CHUNK, CHUNK), 1)
    for d in range(2):
        lg_x = _log_sigmoid(lgx_ref[d])
        lg_p = _log_sigmoid(lgp_ref[d])
        if d == 0:
            k_pow, q_pow, diff = (CHUNK - 1) - idx, idx + 1.0, ii - mi
        else:
            k_pow, q_pow, diff = idx, CHUNK - idx, mi - ii
        kdec_ref[d] = jnp.exp(lg_x * k_pow)
        qdec_ref[d] = jnp.exp(lg_x * q_pow)
        cd_ref[d] = jnp.exp(lg_x * float(CHUNK))
        dpos = jnp.maximum(diff, 0).astype(F32)
        for h in range(RET_HEADS):
            dmat_ref[d, :, h * CHUNK:(h + 1) * CHUNK] = jnp.where(
                diff >= 0, jnp.exp(lg_p[:, h:h + 1] * dpos), 0.0)


def _ret_direction(d, q_ref, k_ref, v_ref, cos_ref, sin_ref, kdec_ref, qdec_ref, cd_ref, dmat_ref,
                   st_ref, o_ref):
    cos, sin = cos_ref[...], sin_ref[...]
    q_blk = _rope(q_ref[...].astype(F32), cos, sin)
    k_blk = _rope(k_ref[...].astype(F32), cos, sin) * (RET_DIM ** -0.5)
    v_blk = v_ref[...].astype(F32)
    for c in (range(SCAN_SUB) if d == 0 else reversed(range(SCAN_SUB))):
        rows = slice(c * CHUNK, (c + 1) * CHUNK)
        _ret_chunk(d, q_blk[rows], k_blk[rows], v_blk[rows], kdec_ref, qdec_ref, cd_ref, dmat_ref, st_ref,
                   o_ref.at[pl.ds(c * CHUNK, CHUNK), :])


def _ret_chunk(d, q, k, v, kdec_ref, qdec_ref, cd_ref, dmat_ref, st_ref, o_ref):
    vk = (v * kdec_ref[d]).astype(BF16)
    qd = (q * qdec_ref[d]).astype(BF16)
    cd = cd_ref[d]
    qb, vb = q.astype(BF16), v.astype(BF16)
    kt = k.T.astype(BF16)
    k_mask = _block_mask(GROUP_W, HEAD_GROUP * CHUNK, RET_DIM, CHUNK)
    v_mask = _block_mask(HEAD_GROUP * CHUNK, GROUP_W, CHUNK, RET_DIM)
    s_mask = _block_mask(GROUP_W, GROUP_W, RET_DIM, RET_DIM)
    zero = jnp.zeros((), BF16)
    for g in range(RET_HEADS // HEAD_GROUP):
        gl = slice(g * GROUP_W, (g + 1) * GROUP_W)
        sl = slice(g * HEAD_GROUP * CHUNK, (g + 1) * HEAD_GROUP * CHUNK)
        kt_g = kt[gl, :]
        k_bd = jnp.where(k_mask, jnp.concatenate([kt_g] * HEAD_GROUP, axis=1), zero)
        s_all = jnp.dot(qb[:, gl], k_bd, preferred_element_type=F32)
        inner = (s_all * dmat_ref[d, :, sl]).astype(BF16)
        lhs = jnp.concatenate([inner, qd[:, gl]], axis=1)
        v_bd = jnp.where(v_mask, jnp.concatenate([vb[:, gl]] * HEAD_GROUP, axis=0), zero)
        st = st_ref[d, g]
        rhs = jnp.concatenate([v_bd, st.astype(BF16)], axis=0)
        o_ref[:, gl] = jnp.dot(lhs, rhs, preferred_element_type=F32)
        upd = jnp.dot(kt_g, vk[:, gl], preferred_element_type=F32)
        st_ref[d, g] = st * cd[:, gl] + jnp.where(s_mask, upd, 0.0)


def _ret_kernel(qf, kf, vf, cosf, sinf, qb, kb, vb, cosb, sinb, lgx_ref, lgp_ref, of_ref, ob_ref,
                st_ref, kdec_ref, qdec_ref, cd_ref, dmat_ref):
    @pl.when(jnp.logical_and(pl.program_id(0) == 0, pl.program_id(1) == 0))
    def _():
        _ret_tables(lgx_ref, lgp_ref, kdec_ref, qdec_ref, cd_ref, dmat_ref)

    @pl.when(pl.program_id(1) == 0)
    def _():
        st_ref[...] = jnp.zeros_like(st_ref)

    _ret_direction(0, qf, kf, vf, cosf, sinf, kdec_ref, qdec_ref, cd_ref, dmat_ref, st_ref, of_ref)
    _ret_direction(1, qb, kb, vb, cosb, sinb, kdec_ref, qdec_ref, cd_ref, dmat_ref, st_ref, ob_ref)


def _ret_scan(lay, p, decay_logit):
    n = lay.n
    cos_t, sin_t = _rope_tables(lay)
    lgx = jnp.repeat(decay_logit, RET_DIM, axis=-1).reshape(2, 1, RET_WIDTH)
    lgp = jnp.pad(decay_logit, ((0, 0), (0, LANES - RET_HEADS))).reshape(2, 1, LANES)

    def specs(block_of):
        def blk(cb):
            return pl.BlockSpec((SCAN_BLK, RET_WIDTH), lambda bi, s: (lay.scan_row_block(bi, block_of(s)), cb))
        tab = pl.BlockSpec((SCAN_BLK, LANES), lambda bi, s: (block_of(s), 0))
        return [blk(P_Q // RET_WIDTH), blk(P_K // RET_WIDTH), blk(P_V // RET_WIDTH), tab, tab]

    def out_spec(block_of):
        return pl.BlockSpec((SCAN_BLK, RET_WIDTH), lambda bi, s: (lay.scan_row_block(bi, block_of(s)), 0))

    const3 = lambda bi, s: (0, 0, 0)
    return pl.pallas_call(
        _ret_kernel,
        out_shape=(jax.ShapeDtypeStruct((n, RET_WIDTH), F32), jax.ShapeDtypeStruct((n, RET_WIDTH), F32)),
        grid=(lay.b, lay.nblk),
        in_specs=specs(lay.fwd_block) + specs(lay.bwd_block)
        + [pl.BlockSpec((2, 1, RET_WIDTH), const3), pl.BlockSpec((2, 1, LANES), const3)],
        out_specs=(out_spec(lay.fwd_block), out_spec(lay.bwd_block)),
        scratch_shapes=[pltpu.VMEM((2, RET_HEADS // HEAD_GROUP, GROUP_W, GROUP_W), F32),
                        pltpu.VMEM((2, CHUNK, RET_WIDTH), F32),
                        pltpu.VMEM((2, CHUNK, RET_WIDTH), F32),
                        pltpu.VMEM((2, 1, RET_WIDTH), F32),
                        pltpu.VMEM((2, CHUNK, RET_HEADS * CHUNK), F32)],
        compiler_params=_cparams(("arbitrary", "arbitrary")),
        name="ret_scan",
    )(p, p, p, cos_t, sin_t, p, p, p, cos_t, sin_t, lgx, lgp)


MERGE_SUB = 2


def _merge_kernel(lay, tm, emit_h2, *refs):
    for k in range(MERGE_SUB):
        _merge_tile(lay, tm, emit_h2, pl.program_id(0) * MERGE_SUB + k, k, *refs)


def _merge_tile(lay, tm, emit_h2, i, k, x_ref, m_ref, gate_ref, z_ref, g_ref, um_ref, up_ref, un_ref,
                sf_ref, sb_ref, rf_ref, rb_ref, sng_ref, band_ref, pw_ref, ps_ref, wb_ref, wo_ref, n2g_ref,
                wrh_ref, wrl_ref, *rest):
    if emit_h2:
        xo_ref, h2_ref, lg_ref = rest
    else:
        (xo_ref,) = rest
    rows = pl.ds(k * tm, tm)
    ctx_tiles = lay.n_ctx_rows // tm
    per_ctx = lay.ctx // tm
    per_lat = lay.seq // tm
    in_ctx = i < ctx_tiles
    t_in_seg = jnp.where(in_ctx, i % per_ctx, (i - ctx_tiles) % per_lat)
    seg_tiles = jnp.where(in_ctx, per_ctx, per_lat)
    seg_len = jnp.where(in_ctx, lay.ctx, lay.seq)
    pos = t_in_seg * tm + lax.broadcasted_iota(jnp.int32, (tm, 1), 0)

    ys = (sf_ref[rows, :] + sb_ref[rows, :]) * _silu(z_ref[rows, :].astype(F32))
    s_br = ys * lax.rsqrt(jnp.mean(ys * ys, axis=-1, keepdims=True) + EPS) * sng_ref[...]

    zb = jnp.zeros((), BF16)
    fill = jnp.zeros((SHIFT_PAD - HALO, POOL_WIDTH), BF16)
    um = um_ref[rows, :]
    before = up_ref[...] if k == 0 else um_ref[k * tm - HALO:k * tm, :]
    after = un_ref[...] if k == MERGE_SUB - 1 else um_ref[(k + 1) * tm:(k + 1) * tm + HALO, :]
    u_ext = jnp.concatenate([fill, jnp.where(t_in_seg == 0, zb, before), um,
                             jnp.where(t_in_seg == seg_tiles - 1, zb, after), fill], axis=0)
    pooled = []
    for gi, w in enumerate(POOL_WINDOWS):
        left = w // 2
        right = w - 1 - left
        ls = slice(gi * POOL_GROUP, (gi + 1) * POOL_GROUP)
        tot = jnp.dot(band_ref[gi], u_ext[:, ls], preferred_element_type=F32)
        cnt = (jnp.minimum(pos + right, seg_len - 1) + 1 - jnp.maximum(pos - left, 0)).astype(F32)
        mixed = tot / cnt - um[:, ls].astype(F32)
        pooled.append(jnp.dot(mixed.astype(BF16), pw_ref[gi], preferred_element_type=F32))
    p_br = jnp.concatenate(pooled, axis=-1) * ps_ref[...]

    yr = rf_ref[rows, :] + rb_ref[rows, :]
    normed = []
    for h in range(RET_HEADS):
        yh = yr[:, h * RET_DIM:(h + 1) * RET_DIM]
        mu = jnp.mean(yh, axis=-1, keepdims=True)
        dv = yh - mu
        var = jnp.mean(dv * dv, axis=-1, keepdims=True)
        normed.append(dv * lax.rsqrt(var + EPS))
    r_br = jnp.concatenate(normed, axis=-1) * _silu(g_ref[rows, :].astype(F32))

    acc = None
    for bi, br in enumerate((s_br, p_br, r_br)):
        gate = _sigmoid(gate_ref[rows, bi * D_MODEL:(bi + 1) * D_MODEL].astype(F32))
        term = gate * jnp.dot(br.astype(BF16), wb_ref[bi], preferred_element_type=F32)
        acc = term if acc is None else acc + term
    mix = jnp.dot(acc.astype(BF16), wo_ref[...], preferred_element_type=F32)
    xn = x_ref[rows, :] + m_ref[2] * mix
    xo_ref[rows, :] = xn
    if emit_h2:
        y = xn * lax.rsqrt(jnp.mean(xn * xn, axis=-1, keepdims=True) + EPS) * n2g_ref[...]
        h2 = y * (1.0 + m_ref[4]) + m_ref[3]
        h2_ref[rows, :] = h2
        hh = h2.astype(BF16)
        hl = (h2 - hh.astype(F32)).astype(BF16)
        wh, wl = wrh_ref[...], wrl_ref[...]
        lg_ref[rows, :] = (jnp.dot(hh, wh, preferred_element_type=F32)
                           + (jnp.dot(hh, wl, preferred_element_type=F32)
                              + jnp.dot(hl, wh, preferred_element_type=F32)))


def _merge(lay, x, mods, p, ssd_f, ssd_b, ret_f, ret_b, ssd_norm_g, pool_w, pool_scale, w_branch, w_out,
           norm2_g, w_router_pad, emit_h2):
    n, d = x.shape
    tm = _row_tile(lay, 256)
    bt = MERGE_SUB * tm
    assert lay.n_ctx_rows % bt == 0 and lay.seq % bt == 0
    n_halo = n // HALO
    per = bt // HALO
    pool_blk = P_POOL // POOL_WIDTH
    wr_hi = w_router_pad.astype(BF16)
    wr_lo = (w_router_pad - wr_hi.astype(F32)).astype(BF16)
    bands = jnp.asarray(_band_consts(tm, [(-(w // 2), w - 1 - w // 2) for w in POOL_WINDOWS]), BF16)
    row = lambda i: (i, 0)
    const2 = lambda i: (0, 0)
    const3 = lambda i: (0, 0, 0)
    in_specs = [
        pl.BlockSpec((bt, d), row),
        pl.BlockSpec((None, 6, 1, d), lambda i: (lay.mod_row(i, bt), 0, 0, 0)),
        pl.BlockSpec((bt, N_BRANCH * d), lambda i: (i, P_GATE // (N_BRANCH * d))),
        pl.BlockSpec((bt, SSD_INNER), lambda i: (i, P_Z // SSD_INNER)),
        pl.BlockSpec((bt, RET_WIDTH), lambda i: (i, P_G // RET_WIDTH)),
        pl.BlockSpec((bt, POOL_WIDTH), lambda i: (i, pool_blk)),
        pl.BlockSpec((HALO, POOL_WIDTH), lambda i: (jnp.maximum(i * per - 1, 0), pool_blk)),
        pl.BlockSpec((HALO, POOL_WIDTH), lambda i: (jnp.minimum(i * per + per, n_halo - 1), pool_blk)),
        pl.BlockSpec((bt, SSD_INNER), row),
        pl.BlockSpec((bt, SSD_INNER), row),
        pl.BlockSpec((bt, RET_WIDTH), row),
        pl.BlockSpec((bt, RET_WIDTH), row),
        pl.BlockSpec((1, SSD_INNER), const2),
        pl.BlockSpec((len(POOL_WINDOWS), tm, tm + 2 * SHIFT_PAD), const3),
        pl.BlockSpec((len(POOL_WINDOWS), POOL_GROUP, POOL_GROUP), const3),
        pl.BlockSpec((1, POOL_WIDTH), const2),
        pl.BlockSpec((N_BRANCH, SSD_INNER, d), const3),
        pl.BlockSpec((d, d), const2),
        pl.BlockSpec((1, d), const2),
        pl.BlockSpec((d, LANES), const2),
        pl.BlockSpec((d, LANES), const2),
    ]
    out_shape = [jax.ShapeDtypeStruct((n, d), F32)]
    out_specs = [pl.BlockSpec((bt, d), row)]
    if emit_h2:
        out_shape += [jax.ShapeDtypeStruct((n, d), F32), jax.ShapeDtypeStruct((n, LANES), F32)]
        out_specs += [pl.BlockSpec((bt, d), row), pl.BlockSpec((bt, LANES), row)]
    return pl.pallas_call(
        functools.partial(_merge_kernel, lay, tm, emit_h2),
        out_shape=tuple(out_shape),
        grid=(n // bt,),
        in_specs=in_specs,
        out_specs=tuple(out_specs),
        compiler_params=_cparams(("arbitrary",)),
        name="merge_h2" if emit_h2 else "merge",
    )(x, mods, p, p, p, p, p, p, ssd_f, ssd_b, ret_f, ret_b, ssd_norm_g.reshape(1, -1),
      bands, pool_w.astype(BF16), pool_scale.reshape(1, -1), w_branch.astype(BF16), w_out.astype(BF16),
      norm2_g.reshape(1, -1), wr_hi, wr_lo)


def _ffn_kernel(x_ref, m_ref, g_ref, wg_ref, wu_ref, w2_ref, o_ref, h_ref, acc_ref):
    j = pl.program_id(1)

    @pl.when(j == 0)
    def _():
        x = x_ref[...]
        y = x * lax.rsqrt(jnp.mean(x * x, axis=-1, keepdims=True) + EPS) * g_ref[...]
        h_ref[...] = (y * (1.0 + m_ref[4]) + m_ref[3]).astype(BF16)
        acc_ref[...] = jnp.zeros_like(acc_ref)

    h = h_ref[...]
    gt = jnp.dot(h, wg_ref[...], preferred_element_type=F32)
    up = jnp.dot(h, wu_ref[...], preferred_element_type=F32)
    acc_ref[...] += jnp.dot((_silu(gt) * up).astype(BF16), w2_ref[...], preferred_element_type=F32)

    @pl.when(j == pl.num_programs(1) - 1)
    def _():
        o_ref[...] = x_ref[...] + m_ref[5] * acc_ref[...]


def _ffn_dense(lay, x, mods, g, w13, w2):
    n, d = x.shape
    ff = w2.shape[0]
    tm = _row_tile(lay, 1024)
    tf = 1408
    nf = ff // tf
    return pl.pallas_call(
        _ffn_kernel,
        out_shape=jax.ShapeDtypeStruct((n, d), F32),
        grid=(n // tm, nf),
        in_specs=[
            pl.BlockSpec((tm, d), lambda i, j: (i, 0)),
            pl.BlockSpec((None, 6, 1, d), lambda i, j: (lay.mod_row(i, tm), 0, 0, 0)),
            pl.BlockSpec((1, d), lambda i, j: (0, 0)),
            pl.BlockSpec((d, tf), lambda i, j: (0, j)),
            pl.BlockSpec((d, tf), lambda i, j: (0, j + nf)),
            pl.BlockSpec((tf, d), lambda i, j: (j, 0)),
        ],
        out_specs=pl.BlockSpec((tm, d), lambda i, j: (i, 0)),
        scratch_shapes=[pltpu.VMEM((tm, d), BF16), pltpu.VMEM((tm, d), F32)],
        compiler_params=_cparams(("arbitrary", "arbitrary")),
        name="ffn_dense",
    )(x, mods, g, w13, w13, w2)


ROUTE_TM = 512
R_E0, R_E1, R_RANK0, R_RANK1, R_W0, R_W1 = range(6)


def _route_kernel(lg_ref, tri_ref, o_ref, cnt_ref, run_ref):
    i = pl.program_id(0)

    @pl.when(i == 0)
    def _():
        run_ref[...] = jnp.zeros_like(run_ref)

    tm = lg_ref.shape[0]
    lane = lax.broadcasted_iota(jnp.int32, (tm, LANES), 1).astype(F32)
    lg = jnp.where(lane < N_EXPERTS, lg_ref[...], -jnp.inf)
    m1 = jnp.max(lg, axis=-1, keepdims=True)
    i1 = jnp.min(jnp.where(lg == m1, lane, float(LANES)), axis=-1, keepdims=True)
    l2 = jnp.where(lane == i1, -jnp.inf, lg)
    m2 = jnp.max(l2, axis=-1, keepdims=True)
    i2 = jnp.min(jnp.where(l2 == m2, lane, float(LANES)), axis=-1, keepdims=True)
    e21 = jnp.exp(m2 - m1)
    w1 = 1.0 / (1.0 + e21)
    w2 = e21 * w1
    sel1, sel2 = lane == i1, lane == i2
    memb = jnp.where(sel1, 1.0, jnp.where(sel2, 1.0, 0.0))
    before = jnp.dot(tri_ref[...], memb.astype(BF16), preferred_element_type=F32) + run_ref[...]
    r1 = jnp.sum(jnp.where(sel1, before, 0.0), axis=-1, keepdims=True)
    r2 = jnp.sum(jnp.where(sel2, before, 0.0), axis=-1, keepdims=True)
    rec = jnp.zeros((tm, LANES), F32)
    for k, v in ((R_E0, i1), (R_E1, i2), (R_RANK0, r1), (R_RANK1, r2), (R_W0, w1), (R_W1, w2)):
        rec = jnp.where(lane == k, v, rec)
    o_ref[...] = rec
    run_ref[...] += jnp.sum(memb, axis=0, keepdims=True)
    cnt_ref[...] = jnp.broadcast_to(run_ref[...], cnt_ref.shape)


def _route(logits):
    n = logits.shape[0]
    tm = ROUTE_TM
    i = np.arange(tm)
    tri = jnp.asarray((i[None, :] < i[:, None]).astype(np.float32), BF16)
    return pl.pallas_call(
        _route_kernel,
        out_shape=(jax.ShapeDtypeStruct((n, LANES), F32), jax.ShapeDtypeStruct((8, LANES), F32)),
        grid=(n // tm,),
        in_specs=[pl.BlockSpec((tm, LANES), lambda i: (i, 0)), pl.BlockSpec((tm, tm), lambda i: (0, 0))],
        out_specs=(pl.BlockSpec((tm, LANES), lambda i: (i, 0)), pl.BlockSpec((8, LANES), lambda i: (0, 0))),
        scratch_shapes=[pltpu.VMEM((1, LANES), F32)],
        compiler_params=_cparams(("arbitrary",)),
        name="moe_route",
    )(logits, tri)


MOE_BM = 1024


def _slot_plan(rec, counts):
    n = rec.shape[0]
    cnt = counts[0, :N_EXPERTS].astype(jnp.int32)
    padded = (cnt + MOE_BM - 1) // MOE_BM * MOE_BM
    pad_ends = jnp.cumsum(padded)
    pad_starts = pad_ends - padded
    n_blocks = -(-(n * TOP_K + N_EXPERTS * (MOE_BM - 1)) // MOE_BM)
    n_used = pad_ends[-1] // MOE_BM
    blk = jnp.minimum(jnp.arange(n_blocks, dtype=jnp.int32), n_used - 1)
    block_e = jnp.minimum(jnp.searchsorted(pad_ends, blk * MOE_BM, side='right'), N_EXPERTS - 1)
    e = rec[:, R_E0:R_E1 + 1].astype(jnp.int32)
    rank = rec[:, R_RANK0:R_RANK1 + 1].astype(jnp.int32)
    start = jnp.zeros_like(e)
    for k in range(N_EXPERTS):
        start = jnp.where(e == k, pad_starts[k], start)
    dest = start + rank
    last_blk = jnp.where(cnt > 0, pad_ends // MOE_BM - 1, -1).astype(jnp.int32)
    return dest, block_e.astype(jnp.int32), n_used.astype(jnp.int32).reshape(1), last_blk, n_blocks


DISPATCH_TM = 256


def _dispatch_kernel(last_ref, nb_ref, dest_ref, h_ref, xb_ref, zero_ref, sem, zsem):
    i = pl.program_id(0)
    tm = DISPATCH_TM
    n_blocks = xb_ref.shape[0] // MOE_BM

    def clear_block(blk):
        start = pl.multiple_of(blk * MOE_BM, MOE_BM)
        cp = pltpu.make_async_copy(zero_ref, xb_ref.at[pl.ds(start, MOE_BM), :], zsem)
        cp.start()
        cp.wait()

    @pl.when(i == 0)
    def _():
        zero_ref[...] = jnp.zeros_like(zero_ref)
        for e in range(N_EXPERTS):
            @pl.when(last_ref[e] >= 0)
            def _():
                clear_block(last_ref[e])

            @pl.when(nb_ref[0] + e < n_blocks)
            def _():
                clear_block(nb_ref[0] + e)

    for r in range(tm):
        for k in range(TOP_K):
            pltpu.make_async_copy(h_ref.at[pl.ds(r, 1), :],
                                  xb_ref.at[pl.ds(dest_ref[0, TOP_K * r + k], 1), :], sem).start(priority=k)
    for k in range(TOP_K):
        pltpu.make_async_copy(h_ref, xb_ref.at[pl.ds(0, tm), :], sem).wait()


def _dispatch(h2, dest, last_blk, n_used, n_blocks):
    n, d = h2.shape
    tm = DISPATCH_TM
    cap = n_blocks * MOE_BM
    grid_spec = pltpu.PrefetchScalarGridSpec(
        num_scalar_prefetch=2,
        grid=(n // tm,),
        in_specs=[pl.BlockSpec((None, 1, TOP_K * tm), lambda i, lb, nb: (i, 0, 0), memory_space=pltpu.SMEM),
                  pl.BlockSpec((tm, d), lambda i, lb, nb: (i, 0))],
        out_specs=pl.BlockSpec(memory_space=pl.ANY),
        scratch_shapes=[pltpu.VMEM((MOE_BM, d), F32), pltpu.SemaphoreType.DMA, pltpu.SemaphoreType.DMA],
    )
    return pl.pallas_call(
        _dispatch_kernel,
        out_shape=jax.ShapeDtypeStruct((cap, d), F32),
        grid_spec=grid_spec,
        compiler_params=_cparams(("arbitrary",)),
        name="moe_dispatch",
    )(last_blk, n_used, dest.reshape(n // tm, 1, TOP_K * tm), h2)


MOE_TF = 512


def _moe_kernel(be_ref, nb_ref, x_ref, wg_ref, wu_ref, w2_ref, o_ref, h_ref, acc_ref):
    i, j = pl.program_id(0), pl.program_id(1)

    @pl.when(i < nb_ref[0])
    def _():
        @pl.when(j == 0)
        def _():
            h_ref[...] = x_ref[...].astype(BF16)
            acc_ref[...] = jnp.zeros_like(acc_ref)

        h = h_ref[...]
        gt = jnp.dot(h, wg_ref[...].astype(BF16), preferred_element_type=F32)
        up = jnp.dot(h, wu_ref[...].astype(BF16), preferred_element_type=F32)
        acc_ref[...] += jnp.dot((_silu(gt) * up).astype(BF16), w2_ref[...].astype(BF16),
                                preferred_element_type=F32)

        @pl.when(j == pl.num_programs(1) - 1)
        def _():
            o_ref[...] = acc_ref[...]

    @pl.when(jnp.logical_and(i >= nb_ref[0], j == pl.num_programs(1) - 1))
    def _():
        o_ref[...] = jnp.zeros_like(o_ref)


def _moe_blocks(xb, block_e, n_used, w13, w2, li):
    cap, d = xb.shape
    _, ne, ff, _ = w2.shape
    tf = MOE_TF
    nf = ff // tf
    n_blocks = cap // MOE_BM

    def row(i, j, be, nb):
        return (jnp.minimum(i, nb[0] - 1), 0)

    def jj(i, j, nb):
        return jnp.where(i < nb[0], j, nf - 1)

    grid_spec = pltpu.PrefetchScalarGridSpec(
        num_scalar_prefetch=2,
        grid=(n_blocks, nf),
        in_specs=[
            pl.BlockSpec((MOE_BM, d), row),
            pl.BlockSpec((None, None, d, tf), lambda i, j, be, nb: (li, be[i], 0, jj(i, j, nb))),
            pl.BlockSpec((None, None, d, tf), lambda i, j, be, nb: (li, be[i], 0, jj(i, j, nb) + nf)),
            pl.BlockSpec((None, None, tf, d), lambda i, j, be, nb: (li, be[i], jj(i, j, nb), 0)),
        ],
        out_specs=pl.BlockSpec((MOE_BM, d), lambda i, j, be, nb: (i, 0)),
        scratch_shapes=[pltpu.VMEM((MOE_BM, d), BF16), pltpu.VMEM((MOE_BM, d), F32)],
    )
    return pl.pallas_call(
        _moe_kernel,
        out_shape=jax.ShapeDtypeStruct((cap, d), F32),
        grid_spec=grid_spec,
        compiler_params=_cparams(("arbitrary", "arbitrary")),
        name="moe_experts",
    )(block_e, n_used, xb, w13, w13, w2)


COMBINE_TM = 256


def _combine_kernel(final, dest_ref, x_ref, m_ref, rec_ref, yb_ref, *rest):
    if final:
        g_ref, o_ref, buf_ref, sem = rest
    else:
        o_ref, buf_ref, sem = rest
    tm = COMBINE_TM

    for r in range(tm):
        for k in range(TOP_K):
            pltpu.make_async_copy(yb_ref.at[pl.ds(dest_ref[0, TOP_K * r + k], 1), :],
                                  buf_ref.at[k, pl.ds(r, 1), :], sem).start(priority=k)
    for k in range(TOP_K):
        pltpu.make_async_copy(yb_ref.at[pl.ds(0, tm), :], buf_ref.at[k], sem).wait()
    rec = rec_ref[...]
    y = rec[:, R_W0:R_W0 + 1] * buf_ref[0] + rec[:, R_W1:R_W1 + 1] * buf_ref[1]
    xn = x_ref[...] + m_ref[5] * y
    if final:
        xn = xn * lax.rsqrt(jnp.mean(xn * xn, axis=-1, keepdims=True) + EPS) * g_ref[...]
    o_ref[...] = xn


def _combine(lay, x, mods, rec, dest, yb, final_g=None):
    n, d = x.shape
    tm = COMBINE_TM
    final = final_g is not None
    off = lay.n_ctx_rows // tm if final else 0
    n_out = n - off * tm
    row = lambda i: (i + off, 0)
    in_specs = [pl.BlockSpec((None, 1, TOP_K * tm), lambda i: (i + off, 0, 0), memory_space=pltpu.SMEM),
                pl.BlockSpec((tm, d), row),
                pl.BlockSpec((None, 6, 1, d), lambda i: (lay.mod_row(i + off, tm), 0, 0, 0)),
                pl.BlockSpec((tm, LANES), row),
                pl.BlockSpec(memory_space=pl.ANY)]
    args = [dest.reshape(n // tm, 1, TOP_K * tm), x, mods, rec, yb]
    if final:
        in_specs.append(pl.BlockSpec((1, d), lambda i: (0, 0)))
        args.append(final_g.reshape(1, -1))
    return pl.pallas_call(
        functools.partial(_combine_kernel, final),
        out_shape=jax.ShapeDtypeStruct((n_out, d), F32),
        grid=(n_out // tm,),
        in_specs=in_specs,
        out_specs=pl.BlockSpec((tm, d), lambda i: (i, 0)),
        scratch_shapes=[pltpu.VMEM((TOP_K, tm, d), F32), pltpu.SemaphoreType.DMA],
        compiler_params=_cparams(("arbitrary",)),
        name="moe_combine_final" if final else "moe_combine",
    )(*args)


def _final_kernel(x_ref, g_ref, o_ref):
    x = x_ref[...]
    o_ref[...] = x * lax.rsqrt(jnp.mean(x * x, axis=-1, keepdims=True) + EPS) * g_ref[...]


def _final_norm(lay, x, g):
    n, d = x.shape
    tm = _row_tile(lay, 1024)
    off = lay.n_ctx_rows // tm
    n_lat = lay.b * lay.seq
    return pl.pallas_call(
        _final_kernel,
        out_shape=jax.ShapeDtypeStruct((n_lat, d), F32),
        grid=(n_lat // tm,),
        in_specs=[pl.BlockSpec((tm, d), lambda i: (i + off, 0)), pl.BlockSpec((1, d), lambda i: (0, 0))],
        out_specs=pl.BlockSpec((tm, d), lambda i: (i, 0)),
        compiler_params=_cparams(("arbitrary",)),
        name="final_norm",
    )(x, g.reshape(1, -1))


def _permute_w_in(w):
    d = w.shape[0]
    parts = [w[:, COL_GATE:COL_GATE + N_BRANCH * D_MODEL], w[:, COL_Q:COL_G_END],
             w[:, COL_POOL:COL_POOL + POOL_WIDTH], w[:, COL_Z:COL_Z + SSD_INNER],
             w[:, COL_XBC:COL_XBC + SSD_XBC]]
    w_dt = jnp.pad(w[:, COL_DT:COL_DT + 2 * SSD_HEADS], ((0, 0), (0, LANES - 2 * SSD_HEADS)))
    return jnp.concatenate(parts, axis=1).astype(BF16), w_dt.astype(BF16)


def kernel(x, c, ctx, c_ctx, ada_w, ada_b, norm1_g, norm2_g, w_in, ssd_conv_w, ssd_conv_b, ssd_dt_bias,
           ssd_a_log, ssd_d, ssd_norm_g, pool_w, pool_scale, ret_decay_logit, w_branch, w_out,
           ffn_w13, ffn_w2, moe_router, moe_w13, moe_w2, final_norm_g):
    b, seq, d = x.shape
    ctx_len = ctx.shape[1]
    depth = w_in.shape[0]
    lay = _Layout(b, ctx_len, seq)

    cvec = jnp.concatenate([c, c_ctx[None, :], jnp.zeros((8 - b - 1, d), F32)], axis=0)
    mods_all = _ada_all(cvec, ada_w, ada_b).reshape(depth, 8, 6, 1, d)
    xa = jnp.concatenate([ctx.reshape(-1, d), x.reshape(-1, d)], axis=0)

    for layer in range(depth):
        mods = mods_all[layer]
        w_main, w_dt = _permute_w_in(w_in[layer])
        p, dt_raw = _in_proj(lay, xa, mods, norm1_g[layer].reshape(1, -1), w_main, w_dt)
        ssd_f, ssd_b = _ssd_scan(lay, p, dt_raw, ssd_conv_w[layer], ssd_conv_b[layer], ssd_dt_bias[layer],
                                 ssd_a_log[layer], ssd_d[layer])
        ret_f, ret_b = _ret_scan(lay, p, ret_decay_logit[layer])
        is_moe = layer % 2 == 1
        if is_moe:
            w_r = jnp.pad(moe_router[layer // 2], ((0, 0), (0, LANES - N_EXPERTS)))
        else:
            w_r = jnp.zeros((d, LANES), F32)
        outs = _merge(lay, xa, mods, p, ssd_f, ssd_b, ret_f, ret_b, ssd_norm_g[layer], pool_w[layer],
                      pool_scale[layer], w_branch[layer], w_out[layer], norm2_g[layer], w_r, is_moe)
        if not is_moe:
            (xa,) = outs
            xa = _ffn_dense(lay, xa, mods, norm2_g[layer].reshape(1, -1),
                            ffn_w13[layer // 2].astype(BF16), ffn_w2[layer // 2].astype(BF16))
        else:
            xa, h2, logits = outs
            rec, counts = _route(logits)
            dest, block_e, n_used, last_blk, n_blocks = _slot_plan(rec, counts)
            xb = _dispatch(h2, dest, last_blk, n_used, n_blocks)
            yb = _moe_blocks(xb, block_e, n_used, moe_w13, moe_w2, layer // 2)
            if layer == depth - 1:
                return _combine(lay, xa, mods, rec, dest, yb, final_norm_g).reshape(b, seq, d)
            xa = _combine(lay, xa, mods, rec, dest, yb)
    return _final_norm(lay, xa, final_norm_g).reshape(b, seq, d)
```

```python
import functools

import numpy as np
import jax
import jax.numpy as jnp
from jax import lax
from jax.experimental import pallas as pl
from jax.experimental.pallas import tpu as pltpu

F32 = jnp.float32
BF16 = jnp.bfloat16
HIGHEST = lax.Precision.HIGHEST

D_MODEL = 1024
GRID_W = 64
EPS = 1e-6
CHUNK = 128
SCAN_SUB = 2
SCAN_BLK = SCAN_SUB * CHUNK
HALO = 16
SSD_HEADS = 8
SSD_HEAD_DIM = 64
SSD_INNER = 512
SSD_STATE = 64
SSD_GROUPS = 2
SSD_CONV = 5
SSD_XBC = 768
POOL_WINDOWS = (2, 4, 8, 16)
POOL_WIDTH = 512
POOL_GROUP = 128
RET_HEADS = 8
RET_DIM = 64
RET_WIDTH = 512
ROPE_BASE = 10000.0
N_BRANCH = 3
N_EXPERTS = 8
TOP_K = 2
LANES = 128
HEAD_GROUP = 4
GROUP_W = HEAD_GROUP * 64

COL_Z = 0
COL_XBC = 512
COL_DT = 1280
COL_POOL = 1296
COL_Q = 1808
COL_G_END = 3856
COL_GATE = 3856
IN_COLS = 6928
P_GATE = 0
P_Q = 3072
P_K = 3584
P_V = 4096
P_G = 4608
P_POOL = 5120
P_Z = 5632
P_XBC = 6144
P_COLS = 6912

VMEM_LIMIT = 56 * 1024 * 1024


def _sigmoid(v):
    return 0.5 * jnp.tanh(0.5 * v) + 0.5


def _silu(v):
    return v * _sigmoid(v)


def _softplus(v):
    return jnp.maximum(v, 0.0) + jnp.log1p(jnp.exp(-jnp.abs(v)))


def _log_sigmoid(v):
    return -_softplus(-v)


def _cparams(sem):
    return pltpu.CompilerParams(dimension_semantics=sem, vmem_limit_bytes=VMEM_LIMIT)


def _split3(x):
    hi = x.astype(BF16)
    r = x - hi.astype(F32)
    mid = r.astype(BF16)
    lo = (r - mid.astype(F32)).astype(BF16)
    return hi, mid, lo


def _dot_sel_right(x, m):
    return sum(jnp.dot(part, m, preferred_element_type=F32) for part in _split3(x))


def _dot_sel_left(m, x):
    return sum(jnp.dot(m, part, preferred_element_type=F32) for part in _split3(x))


SHIFT_PAD = 64


def _band_consts(rows, windows):
    m = np.zeros((len(windows), rows, rows + 2 * SHIFT_PAD), np.float32)
    t = np.arange(rows)
    for i, (lo, hi) in enumerate(windows):
        for o in range(lo, hi + 1):
            m[i, t, SHIFT_PAD + t + o] = 1.0
    return m


def _block_mask(rows, cols, row_blk, col_blk):
    r = lax.broadcasted_iota(jnp.int32, (rows, cols), 0) // row_blk
    c = lax.broadcasted_iota(jnp.int32, (rows, cols), 1) // col_blk
    return r == c


def _ada_kernel(c_ref, w_ref, b_ref, o_ref):
    cv = c_ref[...]
    o_ref[...] = jnp.dot(_silu(cv), w_ref[...], precision=HIGHEST,
                         preferred_element_type=F32) + b_ref[...]


def _ada_all(cvec, ada_w, ada_b):
    depth, d, n6 = ada_w.shape
    tn = 1536
    return pl.pallas_call(
        _ada_kernel,
        out_shape=jax.ShapeDtypeStruct((depth, 8, n6), F32),
        grid=(depth, n6 // tn),
        in_specs=[
            pl.BlockSpec((8, d), lambda l, j: (0, 0)),
            pl.BlockSpec((None, d, tn), lambda l, j: (l, 0, j)),
            pl.BlockSpec((None, 1, tn), lambda l, j: (l, 0, j)),
        ],
        out_specs=pl.BlockSpec((None, 8, tn), lambda l, j: (l, 0, j)),
        compiler_params=_cparams(("arbitrary", "arbitrary")),
        name="ada_mods",
    )(cvec, ada_w, ada_b.reshape(depth, 1, n6))


class _Layout:
    def __init__(self, b, ctx_len, seq):
        self.b, self.ctx, self.seq = b, ctx_len, seq
        self.n_ctx_rows = b * ctx_len
        self.n = b * (ctx_len + seq)
        assert ctx_len % SCAN_BLK == 0 and seq % SCAN_BLK == 0
        self.cblk = ctx_len // SCAN_BLK
        self.lblk = seq // SCAN_BLK
        self.nblk = self.cblk + self.lblk

    def mod_row(self, tile, tm):
        ctx_tiles = self.n_ctx_rows // tm
        per_b = self.seq // tm
        return jnp.where(tile < ctx_tiles, self.b, (tile - ctx_tiles) // per_b)

    def scan_row_block(self, bi, c):
        return jnp.where(c < self.cblk, bi * self.cblk + c,
                         self.b * self.cblk + bi * self.lblk + (c - self.cblk))

    def fwd_block(self, s):
        return s

    def bwd_block(self, s):
        return jnp.where(s < self.cblk, self.cblk - 1 - s, self.nblk - 1 - (s - self.cblk))


def _row_tile(lay, cap):
    tm = cap
    while lay.n_ctx_rows % tm or lay.seq % tm:
        tm //= 2
    return tm


def _in_kernel(x_ref, m_ref, g_ref, w_ref, wdt_ref, o_ref, dt_ref, h_ref):
    @pl.when(pl.program_id(1) == 0)
    def _():
        x = x_ref[...]
        y = x * lax.rsqrt(jnp.mean(x * x, axis=-1, keepdims=True) + EPS) * g_ref[...]
        h_ref[...] = (y * (1.0 + m_ref[1]) + m_ref[0]).astype(BF16)
        dt_ref[...] = jnp.dot(h_ref[...], wdt_ref[...], preferred_element_type=F32)

    o_ref[...] = jnp.dot(h_ref[...], w_ref[...], preferred_element_type=F32).astype(BF16)


def _in_proj(lay, x, mods, g, w, w_dt):
    n, d = x.shape
    tm = _row_tile(lay, 1024)
    tn = 2304
    return pl.pallas_call(
        _in_kernel,
        out_shape=(jax.ShapeDtypeStruct((n, P_COLS), BF16), jax.ShapeDtypeStruct((n, LANES), F32)),
        grid=(n // tm, P_COLS // tn),
        in_specs=[
            pl.BlockSpec((tm, d), lambda i, j: (i, 0)),
            pl.BlockSpec((None, 6, 1, d), lambda i, j: (lay.mod_row(i, tm), 0, 0, 0)),
            pl.BlockSpec((1, d), lambda i, j: (0, 0)),
            pl.BlockSpec((d, tn), lambda i, j: (0, j)),
            pl.BlockSpec((d, LANES), lambda i, j: (0, 0)),
        ],
        out_specs=(pl.BlockSpec((tm, tn), lambda i, j: (i, j)), pl.BlockSpec((tm, LANES), lambda i, j: (i, 0))),
        scratch_shapes=[pltpu.VMEM((tm, d), BF16)],
        compiler_params=_cparams(("arbitrary", "arbitrary")),
        name="in_proj",
    )(x, mods, g, w, w_dt)


def _tri_consts():
    i = np.arange(CHUNK)
    fwd = (i[None, :] <= i[:, None]).astype(np.float32)
    bwd = (i[None, :] >= i[:, None]).astype(np.float32)
    return np.stack([fwd, bwd])


def _expand_consts(heads, width):
    e = np.zeros((2, LANES, heads * width), np.float32)
    for d in range(2):
        for h in range(heads):
            e[d, d * heads + h, h * width:(h + 1) * width] = 1.0
    return e


def _ssd_direction(d, c, lay, xm_ref, xp_ref, xn_ref, dt_ref, tri_ref, exp_ref, cw_ref, cb_ref,
                   dtb_ref, alog_ref, dskip_ref, st_ref, o_ref):
    is_start = jnp.logical_or(c == 0, c == lay.cblk)
    is_end = jnp.logical_or(c == lay.cblk - 1, c == lay.nblk - 1)
    xm = xm_ref[...].astype(F32)
    xw = jnp.concatenate([jnp.where(is_start, 0.0, xp_ref[...].astype(F32)), xm,
                          jnp.where(is_end, 0.0, xn_ref[...].astype(F32))], axis=0)
    mid = SSD_CONV // 2
    acc = cb_ref[...] + cw_ref[mid:mid + 1, :] * xm
    for k in range(SSD_CONV):
        if k != mid:
            shifted = pltpu.roll(xw, (mid - k) % xw.shape[0], 0)[HALO:HALO + SCAN_BLK, :]
            acc = acc + cw_ref[k:k + 1, :] * shifted
    xbc_blk = _silu(acc)
    dt_blk = _softplus(dt_ref[...] + dtb_ref[...])
    for k in (range(SCAN_SUB) if d == 0 else reversed(range(SCAN_SUB))):
        rows = slice(k * CHUNK, (k + 1) * CHUNK)
        _ssd_chunk(d, xbc_blk[rows], dt_blk[rows], tri_ref, exp_ref, alog_ref, dskip_ref, st_ref,
                   o_ref.at[pl.ds(k * CHUNK, CHUNK), :])


def _ssd_chunk(d, xbc, dt_all, tri_ref, exp_ref, alog_ref, dskip_ref, st_ref, o_ref):
    xs = xbc[:, :SSD_INNER]
    bm = xbc[:, SSD_INNER:SSD_INNER + LANES]
    cm = xbc[:, SSD_INNER + LANES:]
    bt = bm.T.astype(BF16)
    top = lax.broadcasted_iota(jnp.int32, (CHUNK, LANES), 0) < SSD_STATE
    zero = jnp.zeros_like(bt)
    bt_bd = jnp.concatenate([jnp.where(top, bt, zero), jnp.where(top, zero, bt)], axis=1)
    cb_all = jnp.dot(cm.astype(BF16), bt_bd, preferred_element_type=F32)

    tri = tri_ref[d]
    acs = _dot_sel_left(tri, dt_all * (-jnp.exp(alog_ref[...])))
    acs_t = acs.T
    dt_x = _dot_sel_right(dt_all, exp_ref[d])
    acs_x = _dot_sel_right(acs, exp_ref[d])
    last = CHUNK - 1 if d == 0 else 0
    tot_x = acs_x[last:last + 1, :]
    u = xs * dt_x
    ud = (u * jnp.exp(tot_x - acs_x)).astype(BF16)
    ub = u.astype(BF16)
    off_x = jnp.exp(acs_x)
    cd_x = jnp.exp(tot_x)

    li = lax.broadcasted_iota(jnp.int32, (CHUNK, CHUNK), 0)
    si = lax.broadcasted_iota(jnp.int32, (CHUNK, CHUNK), 1)
    mask = (si <= li) if d == 0 else (si >= li)
    lane = lax.broadcasted_iota(jnp.int32, (CHUNK, LANES), 1)
    cm_sw = pltpu.roll(cm, SSD_STATE, 1)
    u_mask = _block_mask(HEAD_GROUP * CHUNK, GROUP_W, CHUNK, SSD_HEAD_DIM)
    s_mask = _block_mask(GROUP_W, GROUP_W, SSD_STATE, SSD_HEAD_DIM)
    for g in range(SSD_GROUPS):
        gl = slice(g * GROUP_W, (g + 1) * GROUP_W)
        cb = cb_all[:, g * CHUNK:(g + 1) * CHUNK]
        parts = []
        for hh in range(HEAD_GROUP):
            h = g * HEAD_GROUP + hh
            col = acs_x[:, h * SSD_HEAD_DIM:h * SSD_HEAD_DIM + 1]
            row = acs_t[d * SSD_HEADS + h:d * SSD_HEADS + h + 1, :]
            lm = jnp.exp(jnp.where(mask, col - row, -jnp.inf))
            parts.append((cb * lm).astype(BF16))
        in_g = (lane < SSD_STATE) if g == 0 else (lane >= SSD_STATE)
        c_rep = jnp.where(in_g, cm, cm_sw)
        c_off = jnp.concatenate([c_rep, c_rep], axis=1) * off_x[:, gl]
        parts.append(c_off.astype(BF16))
        lhs = jnp.concatenate(parts, axis=1)
        ub_g = ub[:, gl]
        u_bd = jnp.where(u_mask, jnp.concatenate([ub_g] * HEAD_GROUP, axis=0), jnp.zeros((), BF16))
        st = st_ref[d, g]
        rhs = jnp.concatenate([u_bd, st.astype(BF16)], axis=0)
        y_g = jnp.dot(lhs, rhs, preferred_element_type=F32)
        if d == 0:
            y_g = y_g + dskip_ref[:, gl] * xs[:, gl]
        o_ref[:, gl] = y_g
        bt_g = bt[g * SSD_STATE:(g + 1) * SSD_STATE, :]
        upd = jnp.dot(jnp.concatenate([bt_g] * HEAD_GROUP, axis=0), ud[:, gl], preferred_element_type=F32)
        st_ref[d, g] = st * cd_x[:, gl] + jnp.where(s_mask, upd, 0.0)


def _ssd_kernel(lay, xm_f, xp_f, xn_f, dt_f, xm_b, xp_b, xn_b, dt_b, tri_ref, exp_ref, cw_ref, cb_ref,
                dtb_ref, alog_ref, dskip_ref, of_ref, ob_ref, st_ref):
    s = pl.program_id(1)

    @pl.when(s == 0)
    def _():
        st_ref[...] = jnp.zeros_like(st_ref)

    _ssd_direction(0, lay.fwd_block(s), lay, xm_f, xp_f, xn_f, dt_f, tri_ref, exp_ref, cw_ref, cb_ref,
                   dtb_ref, alog_ref, dskip_ref, st_ref, of_ref)
    _ssd_direction(1, lay.bwd_block(s), lay, xm_b, xp_b, xn_b, dt_b, tri_ref, exp_ref, cw_ref, cb_ref,
                   dtb_ref, alog_ref, dskip_ref, st_ref, ob_ref)


def _halo_specs(lay, width, col_block, block_of):
    n_halo = lay.n // HALO
    per = SCAN_BLK // HALO

    def main(bi, s):
        return (lay.scan_row_block(bi, block_of(s)), col_block)

    def prev(bi, s):
        return (jnp.maximum(lay.scan_row_block(bi, block_of(s)) * per - 1, 0), col_block)

    def nxt(bi, s):
        return (jnp.minimum(lay.scan_row_block(bi, block_of(s)) * per + per, n_halo - 1), col_block)

    return [pl.BlockSpec((SCAN_BLK, width), main), pl.BlockSpec((HALO, width), prev),
            pl.BlockSpec((HALO, width), nxt)]


def _ssd_scan(lay, p, dt_raw, conv_w, conv_b, dt_bias, a_log, d_skip):
    n = lay.n
    pad = LANES - 2 * SSD_HEADS
    dtb = jnp.pad(dt_bias.reshape(1, -1), ((0, 0), (0, pad)))
    alog = jnp.pad(a_log.reshape(1, -1), ((0, 0), (0, pad)))
    dskip = jnp.repeat(d_skip, SSD_HEAD_DIM).reshape(1, SSD_INNER)
    tri = jnp.asarray(_tri_consts(), BF16)
    expand = jnp.asarray(_expand_consts(SSD_HEADS, SSD_HEAD_DIM), BF16)
    xbc_blk = P_XBC // SSD_XBC

    def dt_spec(block_of):
        return pl.BlockSpec((SCAN_BLK, LANES), lambda bi, s: (lay.scan_row_block(bi, block_of(s)), 0))

    def out_spec(block_of):
        return pl.BlockSpec((SCAN_BLK, SSD_INNER), lambda bi, s: (lay.scan_row_block(bi, block_of(s)), 0))

    const2 = lambda bi, s: (0, 0)
    const3 = lambda bi, s: (0, 0, 0)
    in_specs = (
        _halo_specs(lay, SSD_XBC, xbc_blk, lay.fwd_block) + [dt_spec(lay.fwd_block)]
        + _halo_specs(lay, SSD_XBC, xbc_blk, lay.bwd_block) + [dt_spec(lay.bwd_block)]
        + [pl.BlockSpec((2, CHUNK, CHUNK), const3),
           pl.BlockSpec((2, LANES, SSD_INNER), const3),
           pl.BlockSpec((SSD_CONV, SSD_XBC), const2),
           pl.BlockSpec((1, SSD_XBC), const2),
           pl.BlockSpec((1, LANES), const2),
           pl.BlockSpec((1, LANES), const2),
           pl.BlockSpec((1, SSD_INNER), const2)])
    return pl.pallas_call(
        functools.partial(_ssd_kernel, lay),
        out_shape=(jax.ShapeDtypeStruct((n, SSD_INNER), F32), jax.ShapeDtypeStruct((n, SSD_INNER), F32)),
        grid=(lay.b, lay.nblk),
        in_specs=in_specs,
        out_specs=(out_spec(lay.fwd_block), out_spec(lay.bwd_block)),
        scratch_shapes=[pltpu.VMEM((2, SSD_GROUPS, GROUP_W, GROUP_W), F32)],
        compiler_params=_cparams(("arbitrary", "arbitrary")),
        name="ssd_scan",
    )(p, p, p, dt_raw, p, p, p, dt_raw, tri, expand, conv_w, conv_b.reshape(1, -1), dtb, alog, dskip)


def _rope_tables(lay):
    n_axis = RET_DIM // 4
    t = np.arange(lay.seq)
    inv = ROPE_BASE ** (-np.arange(n_axis, dtype=np.float32) / n_axis)
    row = (t // GRID_W).astype(np.float32)
    colp = (t % GRID_W).astype(np.float32)
    ang = jnp.concatenate([jnp.asarray(row)[:, None] * inv, jnp.asarray(colp)[:, None] * inv], axis=-1)
    cos, sin = jnp.cos(ang), jnp.sin(ang)
    cos_l = jnp.concatenate([cos, cos, cos, cos], axis=-1)
    sin_l = jnp.concatenate([-sin, sin, -sin, sin], axis=-1)
    cos_t = jnp.concatenate([jnp.ones((lay.ctx, LANES), F32), cos_l], axis=0)
    sin_t = jnp.concatenate([jnp.zeros((lay.ctx, LANES), F32), sin_l], axis=0)
    return cos_t, sin_t


def _rope(xv, cos, sin):
    lane = lax.broadcasted_iota(jnp.int32, (xv.shape[0], LANES), 1)
    first_half = (lane % RET_DIM) < (RET_DIM // 2)
    out = []
    for j in range(RET_WIDTH // LANES):
        v = xv[:, j * LANES:(j + 1) * LANES]
        swapped = jnp.where(first_half, pltpu.roll(v, LANES - RET_DIM // 2, 1),
                            pltpu.roll(v, RET_DIM // 2, 1))
        out.append(v * cos + swapped * sin)
    return jnp.concatenate(out, axis=-1)


def _ret_tables(lgx_ref, lgp_ref, kdec_ref, qdec_ref, cd_ref, dmat_ref):
    idx = lax.broadcasted_iota(jnp.int32, (CHUNK, 1), 0).astype(F32)
    ii = lax.broadcasted_iota(jnp.int32, (CHUNK, CHUNK), 0)
    mi = lax.broadcasted_iota(jnp.int32, (CHUNK, CHUNK), 1)
    for d in range(2):
        lg_x = _log_sigmoid(lgx_ref[d])
        lg_p = _log_sigmoid(lgp_ref[d])
        if d == 0:
            k_pow, q_pow, diff = (CHUNK - 1) - idx, idx + 1.0, ii - mi
        else:
            k_pow, q_pow, diff = idx, CHUNK - idx, mi - ii
        kdec_ref[d] = jnp.exp(lg_x * k_pow)
        qdec_ref[d] = jnp.exp(lg_x * q_pow)
        cd_ref[d] = jnp.exp(lg_x * float(CHUNK))
        dpos = jnp.maximum(diff, 0).astype(F32)
        for h in range(RET_HEADS):
            dmat_ref[d, :, h * CHUNK:(h + 1) * CHUNK] = jnp.where(
                diff >= 0, jnp.exp(lg_p[:, h:h + 1] * dpos), 0.0)


def _ret_direction(d, q_ref, k_ref, v_ref, cos_ref, sin_ref, kdec_ref, qdec_ref, cd_ref, dmat_ref,
                   st_ref, o_ref):
    cos, sin = cos_ref[...], sin_ref[...]
    q_blk = _rope(q_ref[...].astype(F32), cos, sin)
    k_blk = _rope(k_ref[...].astype(F32), cos, sin) * (RET_DIM ** -0.5)
    v_blk = v_ref[...].astype(F32)
    for c in (range(SCAN_SUB) if d == 0 else reversed(range(SCAN_SUB))):
        rows = slice(c * CHUNK, (c + 1) * CHUNK)
        _ret_chunk(d, q_blk[rows], k_blk[rows], v_blk[rows], kdec_ref, qdec_ref, cd_ref, dmat_ref, st_ref,
                   o_ref.at[pl.ds(c * CHUNK, CHUNK), :])


def _ret_chunk(d, q, k, v, kdec_ref, qdec_ref, cd_ref, dmat_ref, st_ref, o_ref):
    vk = (v * kdec_ref[d]).astype(BF16)
    qd = (q * qdec_ref[d]).astype(BF16)
    cd = cd_ref[d]
    qb, vb = q.astype(BF16), v.astype(BF16)
    kt = k.T.astype(BF16)
    k_mask = _block_mask(GROUP_W, HEAD_GROUP * CHUNK, RET_DIM, CHUNK)
    v_mask = _block_mask(HEAD_GROUP * CHUNK, GROUP_W, CHUNK, RET_DIM)
    s_mask = _block_mask(GROUP_W, GROUP_W, RET_DIM, RET_DIM)
    zero = jnp.zeros((), BF16)
    for g in range(RET_HEADS // HEAD_GROUP):
        gl = slice(g * GROUP_W, (g + 1) * GROUP_W)
        sl = slice(g * HEAD_GROUP * CHUNK, (g + 1) * HEAD_GROUP * CHUNK)
        kt_g = kt[gl, :]
        k_bd = jnp.where(k_mask, jnp.concatenate([kt_g] * HEAD_GROUP, axis=1), zero)
        s_all = jnp.dot(qb[:, gl], k_bd, preferred_element_type=F32)
        inner = (s_all * dmat_ref[d, :, sl]).astype(BF16)
        lhs = jnp.concatenate([inner, qd[:, gl]], axis=1)
        v_bd = jnp.where(v_mask, jnp.concatenate([vb[:, gl]] * HEAD_GROUP, axis=0), zero)
        st = st_ref[d, g]
        rhs = jnp.concatenate([v_bd, st.astype(BF16)], axis=0)
        o_ref[:, gl] = jnp.dot(lhs, rhs, preferred_element_type=F32)
        upd = jnp.dot(kt_g, vk[:, gl], preferred_element_type=F32)
        st_ref[d, g] = st * cd[:, gl] + jnp.where(s_mask, upd, 0.0)


def _ret_kernel(qf, kf, vf, cosf, sinf, qb, kb, vb, cosb, sinb, lgx_ref, lgp_ref, of_ref, ob_ref,
                st_ref, kdec_ref, qdec_ref, cd_ref, dmat_ref):
    @pl.when(jnp.logical_and(pl.program_id(0) == 0, pl.program_id(1) == 0))
    def _():
        _ret_tables(lgx_ref, lgp_ref, kdec_ref, qdec_ref, cd_ref, dmat_ref)

    @pl.when(pl.program_id(1) == 0)
    def _():
        st_ref[...] = jnp.zeros_like(st_ref)

    _ret_direction(0, qf, kf, vf, cosf, sinf, kdec_ref, qdec_ref, cd_ref, dmat_ref, st_ref, of_ref)
    _ret_direction(1, qb, kb, vb, cosb, sinb, kdec_ref, qdec_ref, cd_ref, dmat_ref, st_ref, ob_ref)


def _ret_scan(lay, p, decay_logit):
    n = lay.n
    cos_t, sin_t = _rope_tables(lay)
    lgx = jnp.repeat(decay_logit, RET_DIM, axis=-1).reshape(2, 1, RET_WIDTH)
    lgp = jnp.pad(decay_logit, ((0, 0), (0, LANES - RET_HEADS))).reshape(2, 1, LANES)

    def specs(block_of):
        def blk(cb):
            return pl.BlockSpec((SCAN_BLK, RET_WIDTH), lambda bi, s: (lay.scan_row_block(bi, block_of(s)), cb))
        tab = pl.BlockSpec((SCAN_BLK, LANES), lambda bi, s: (block_of(s), 0))
        return [blk(P_Q // RET_WIDTH), blk(P_K // RET_WIDTH), blk(P_V // RET_WIDTH), tab, tab]

    def out_spec(block_of):
        return pl.BlockSpec((SCAN_BLK, RET_WIDTH), lambda bi, s: (lay.scan_row_block(bi, block_of(s)), 0))

    const3 = lambda bi, s: (0, 0, 0)
    return pl.pallas_call(
        _ret_kernel,
        out_shape=(jax.ShapeDtypeStruct((n, RET_WIDTH), F32), jax.ShapeDtypeStruct((n, RET_WIDTH), F32)),
        grid=(lay.b, lay.nblk),
        in_specs=specs(lay.fwd_block) + specs(lay.bwd_block)
        + [pl.BlockSpec((2, 1, RET_WIDTH), const3), pl.BlockSpec((2, 1, LANES), const3)],
        out_specs=(out_spec(lay.fwd_block), out_spec(lay.bwd_block)),
        scratch_shapes=[pltpu.VMEM((2, RET_HEADS // HEAD_GROUP, GROUP_W, GROUP_W), F32),
                        pltpu.VMEM((2, CHUNK, RET_WIDTH), F32),
                        pltpu.VMEM((2, CHUNK, RET_WIDTH), F32),
                        pltpu.VMEM((2, 1, RET_WIDTH), F32),
                        pltpu.VMEM((2, CHUNK, RET_HEADS * CHUNK), F32)],
        compiler_params=_cparams(("arbitrary", "arbitrary")),
        name="ret_scan",
    )(p, p, p, cos_t, sin_t, p, p, p, cos_t, sin_t, lgx, lgp)


MERGE_SUB = 2


def _merge_kernel(lay, tm, emit_h2, *refs):
    for k in range(MERGE_SUB):
        _merge_tile(lay, tm, emit_h2, pl.program_id(0) * MERGE_SUB + k, k, *refs)


def _merge_tile(lay, tm, emit_h2, i, k, x_ref, m_ref, gate_ref, z_ref, g_ref, um_ref, up_ref, un_ref,
                sf_ref, sb_ref, rf_ref, rb_ref, sng_ref, band_ref, pw_ref, ps_ref, wb_ref, wo_ref, n2g_ref,
                wrh_ref, wrl_ref, *rest):
    if emit_h2:
        xo_ref, h2_ref, lg_ref = rest
    else:
        (xo_ref,) = rest
    rows = pl.ds(k * tm, tm)
    ctx_tiles = lay.n_ctx_rows // tm
    per_ctx = lay.ctx // tm
    per_lat = lay.seq // tm
    in_ctx = i < ctx_tiles
    t_in_seg = jnp.where(in_ctx, i % per_ctx, (i - ctx_tiles) % per_lat)
    seg_tiles = jnp.where(in_ctx, per_ctx, per_lat)
    seg_len = jnp.where(in_ctx, lay.ctx, lay.seq)
    pos = t_in_seg * tm + lax.broadcasted_iota(jnp.int32, (tm, 1), 0)

    ys = (sf_ref[rows, :] + sb_ref[rows, :]) * _silu(z_ref[rows, :].astype(F32))
    s_br = ys * lax.rsqrt(jnp.mean(ys * ys, axis=-1, keepdims=True) + EPS) * sng_ref[...]

    zb = jnp.zeros((), BF16)
    fill = jnp.zeros((SHIFT_PAD - HALO, POOL_WIDTH), BF16)
    um = um_ref[rows, :]
    before = up_ref[...] if k == 0 else um_ref[k * tm - HALO:k * tm, :]
    after = un_ref[...] if k == MERGE_SUB - 1 else um_ref[(k + 1) * tm:(k + 1) * tm + HALO, :]
    u_ext = jnp.concatenate([fill, jnp.where(t_in_seg == 0, zb, before), um,
                             jnp.where(t_in_seg == seg_tiles - 1, zb, after), fill], axis=0)
    pooled = []
    for gi, w in enumerate(POOL_WINDOWS):
        left = w // 2
        right = w - 1 - left
        ls = slice(gi * POOL_GROUP, (gi + 1) * POOL_GROUP)
        tot = jnp.dot(band_ref[gi], u_ext[:, ls], preferred_element_type=F32)
        cnt = (jnp.minimum(pos + right, seg_len - 1) + 1 - jnp.maximum(pos - left, 0)).astype(F32)
        mixed = tot / cnt - um[:, ls].astype(F32)
        pooled.append(jnp.dot(mixed.astype(BF16), pw_ref[gi], preferred_element_type=F32))
    p_br = jnp.concatenate(pooled, axis=-1) * ps_ref[...]

    yr = rf_ref[rows, :] + rb_ref[rows, :]
    normed = []
    for h in range(RET_HEADS):
        yh = yr[:, h * RET_DIM:(h + 1) * RET_DIM]
        mu = jnp.mean(yh, axis=-1, keepdims=True)
        dv = yh - mu
        var = jnp.mean(dv * dv, axis=-1, keepdims=True)
        normed.append(dv * lax.rsqrt(var + EPS))
    r_br = jnp.concatenate(normed, axis=-1) * _silu(g_ref[rows, :].astype(F32))

    acc = None
    for bi, br in enumerate((s_br, p_br, r_br)):
        gate = _sigmoid(gate_ref[rows, bi * D_MODEL:(bi + 1) * D_MODEL].astype(F32))
        term = gate * jnp.dot(br.astype(BF16), wb_ref[bi], preferred_element_type=F32)
        acc = term if acc is None else acc + term
    mix = jnp.dot(acc.astype(BF16), wo_ref[...], preferred_element_type=F32)
    xn = x_ref[rows, :] + m_ref[2] * mix
    xo_ref[rows, :] = xn
    if emit_h2:
        y = xn * lax.rsqrt(jnp.mean(xn * xn, axis=-1, keepdims=True) + EPS) * n2g_ref[...]
        h2 = y * (1.0 + m_ref[4]) + m_ref[3]
        h2_ref[rows, :] = h2
        hh = h2.astype(BF16)
        hl = (h2 - hh.astype(F32)).astype(BF16)
        wh, wl = wrh_ref[...], wrl_ref[...]
        lg_ref[rows, :] = (jnp.dot(hh, wh, preferred_element_type=F32)
                           + (jnp.dot(hh, wl, preferred_element_type=F32)
                              + jnp.dot(hl, wh, preferred_element_type=F32)))


def _merge(lay, x, mods, p, ssd_f, ssd_b, ret_f, ret_b, ssd_norm_g, pool_w, pool_scale, w_branch, w_out,
           norm2_g, w_router_pad, emit_h2):
    n, d = x.shape
    tm = _row_tile(lay, 256)
    bt = MERGE_SUB * tm
    assert lay.n_ctx_rows % bt == 0 and lay.seq % bt == 0
    n_halo = n // HALO
    per = bt // HALO
    pool_blk = P_POOL // POOL_WIDTH
    wr_hi = w_router_pad.astype(BF16)
    wr_lo = (w_router_pad - wr_hi.astype(F32)).astype(BF16)
    bands = jnp.asarray(_band_consts(tm, [(-(w // 2), w - 1 - w // 2) for w in POOL_WINDOWS]), BF16)
    row = lambda i: (i, 0)
    const2 = lambda i: (0, 0)
    const3 = lambda i: (0, 0, 0)
    in_specs = [
        pl.BlockSpec((bt, d), row),
        pl.BlockSpec((None, 6, 1, d), lambda i: (lay.mod_row(i, bt), 0, 0, 0)),
        pl.BlockSpec((bt, N_BRANCH * d), lambda i: (i, P_GATE // (N_BRANCH * d))),
        pl.BlockSpec((bt, SSD_INNER), lambda i: (i, P_Z // SSD_INNER)),
        pl.BlockSpec((bt, RET_WIDTH), lambda i: (i, P_G // RET_WIDTH)),
        pl.BlockSpec((bt, POOL_WIDTH), lambda i: (i, pool_blk)),
        pl.BlockSpec((HALO, POOL_WIDTH), lambda i: (jnp.maximum(i * per - 1, 0), pool_blk)),
        pl.BlockSpec((HALO, POOL_WIDTH), lambda i: (jnp.minimum(i * per + per, n_halo - 1), pool_blk)),
        pl.BlockSpec((bt, SSD_INNER), row),
        pl.BlockSpec((bt, SSD_INNER), row),
        pl.BlockSpec((bt, RET_WIDTH), row),
        pl.BlockSpec((bt, RET_WIDTH), row),
        pl.BlockSpec((1, SSD_INNER), const2),
        pl.BlockSpec((len(POOL_WINDOWS), tm, tm + 2 * SHIFT_PAD), const3),
        pl.BlockSpec((len(POOL_WINDOWS), POOL_GROUP, POOL_GROUP), const3),
        pl.BlockSpec((1, POOL_WIDTH), const2),
        pl.BlockSpec((N_BRANCH, SSD_INNER, d), const3),
        pl.BlockSpec((d, d), const2),
        pl.BlockSpec((1, d), const2),
        pl.BlockSpec((d, LANES), const2),
        pl.BlockSpec((d, LANES), const2),
    ]
    out_shape = [jax.ShapeDtypeStruct((n, d), F32)]
    out_specs = [pl.BlockSpec((bt, d), row)]
    if emit_h2:
        out_shape += [jax.ShapeDtypeStruct((n, d), F32), jax.ShapeDtypeStruct((n, LANES), F32)]
        out_specs += [pl.BlockSpec((bt, d), row), pl.BlockSpec((bt, LANES), row)]
    return pl.pallas_call(
        functools.partial(_merge_kernel, lay, tm, emit_h2),
        out_shape=tuple(out_shape),
        grid=(n // bt,),
        in_specs=in_specs,
        out_specs=tuple(out_specs),
        compiler_params=_cparams(("arbitrary",)),
        name="merge_h2" if emit_h2 else "merge",
    )(x, mods, p, p, p, p, p, p, ssd_f, ssd_b, ret_f, ret_b, ssd_norm_g.reshape(1, -1),
      bands, pool_w.astype(BF16), pool_scale.reshape(1, -1), w_branch.astype(BF16), w_out.astype(BF16),
      norm2_g.reshape(1, -1), wr_hi, wr_lo)


def _ffn_kernel(x_ref, m_ref, g_ref, wg_ref, wu_ref, w2_ref, o_ref, h_ref, acc_ref):
    j = pl.program_id(1)

    @pl.when(j == 0)
    def _():
        x = x_ref[...]
        y = x * lax.rsqrt(jnp.mean(x * x, axis=-1, keepdims=True) + EPS) * g_ref[...]
        h_ref[...] = (y * (1.0 + m_ref[4]) + m_ref[3]).astype(BF16)
        acc_ref[...] = jnp.zeros_like(acc_ref)

    h = h_ref[...]
    gt = jnp.dot(h, wg_ref[...], preferred_element_type=F32)
    up = jnp.dot(h, wu_ref[...], preferred_element_type=F32)
    acc_ref[...] += jnp.dot((_silu(gt) * up).astype(BF16), w2_ref[...], preferred_element_type=F32)

    @pl.when(j == pl.num_programs(1) - 1)
    def _():
        o_ref[...] = x_ref[...] + m_ref[5] * acc_ref[...]


def _ffn_dense(lay, x, mods, g, w13, w2):
    n, d = x.shape
    ff = w2.shape[0]
    tm = _row_tile(lay, 1024)
    tf = 1408
    nf = ff // tf
    return pl.pallas_call(
        _ffn_kernel,
        out_shape=jax.ShapeDtypeStruct((n, d), F32),
        grid=(n // tm, nf),
        in_specs=[
            pl.BlockSpec((tm, d), lambda i, j: (i, 0)),
            pl.BlockSpec((None, 6, 1, d), lambda i, j: (lay.mod_row(i, tm), 0, 0, 0)),
            pl.BlockSpec((1, d), lambda i, j: (0, 0)),
            pl.BlockSpec((d, tf), lambda i, j: (0, j)),
            pl.BlockSpec((d, tf), lambda i, j: (0, j + nf)),
            pl.BlockSpec((tf, d), lambda i, j: (j, 0)),
        ],
        out_specs=pl.BlockSpec((tm, d), lambda i, j: (i, 0)),
        scratch_shapes=[pltpu.VMEM((tm, d), BF16), pltpu.VMEM((tm, d), F32)],
        compiler_params=_cparams(("arbitrary", "arbitrary")),
        name="ffn_dense",
    )(x, mods, g, w13, w13, w2)


ROUTE_TM = 1024
R_E0, R_E1, R_RANK0, R_RANK1, R_W0, R_W1 = range(6)


def _route_kernel(lg_ref, tri_ref, o_ref, cnt_ref, run_ref):
    i = pl.program_id(0)

    @pl.when(i == 0)
    def _():
        run_ref[...] = jnp.zeros_like(run_ref)

    tm = lg_ref.shape[0]
    lane = lax.broadcasted_iota(jnp.int32, (tm, LANES), 1).astype(F32)
    lg = jnp.where(lane < N_EXPERTS, lg_ref[...], -jnp.inf)
    m1 = jnp.max(lg, axis=-1, keepdims=True)
    i1 = jnp.min(jnp.where(lg == m1, lane, float(LANES)), axis=-1, keepdims=True)
    l2 = jnp.where(lane == i1, -jnp.inf, lg)
    m2 = jnp.max(l2, axis=-1, keepdims=True)
    i2 = jnp.min(jnp.where(l2 == m2, lane, float(LANES)), axis=-1, keepdims=True)
    e21 = jnp.exp(m2 - m1)
    w1 = 1.0 / (1.0 + e21)
    w2 = e21 * w1
    sel1, sel2 = lane == i1, lane == i2
    memb = jnp.where(sel1, 1.0, jnp.where(sel2, 1.0, 0.0))
    before = jnp.dot(tri_ref[...], memb.astype(BF16), preferred_element_type=F32) + run_ref[...]
    r1 = jnp.sum(jnp.where(sel1, before, 0.0), axis=-1, keepdims=True)
    r2 = jnp.sum(jnp.where(sel2, before, 0.0), axis=-1, keepdims=True)
    rec = jnp.zeros((tm, LANES), F32)
    for k, v in ((R_E0, i1), (R_E1, i2), (R_RANK0, r1), (R_RANK1, r2), (R_W0, w1), (R_W1, w2)):
        rec = jnp.where(lane == k, v, rec)
    o_ref[...] = rec
    run_ref[...] += jnp.sum(memb, axis=0, keepdims=True)
    cnt_ref[...] = jnp.broadcast_to(run_ref[...], cnt_ref.shape)


def _route(logits):
    n = logits.shape[0]
    tm = ROUTE_TM
    while n % tm:
        tm //= 2
    i = np.arange(tm)
    tri = jnp.asarray((i[None, :] < i[:, None]).astype(np.float32), BF16)
    return pl.pallas_call(
        _route_kernel,
        out_shape=(jax.ShapeDtypeStruct((n, LANES), F32), jax.ShapeDtypeStruct((8, LANES), F32)),
        grid=(n // tm,),
        in_specs=[pl.BlockSpec((tm, LANES), lambda i: (i, 0)), pl.BlockSpec((tm, tm), lambda i: (0, 0))],
        out_specs=(pl.BlockSpec((tm, LANES), lambda i: (i, 0)), pl.BlockSpec((8, LANES), lambda i: (0, 0))),
        scratch_shapes=[pltpu.VMEM((1, LANES), F32)],
        compiler_params=_cparams(("arbitrary",)),
        name="moe_route",
    )(logits, tri)


MOE_BM = 1024


def _slot_plan(rec, counts):
    n = rec.shape[0]
    cnt = counts[0, :N_EXPERTS].astype(jnp.int32)
    padded = (cnt + MOE_BM - 1) // MOE_BM * MOE_BM
    pad_ends = jnp.cumsum(padded)
    pad_starts = pad_ends - padded
    n_blocks = -(-(n * TOP_K + N_EXPERTS * (MOE_BM - 1)) // MOE_BM)
    n_used = pad_ends[-1] // MOE_BM
    blk = jnp.minimum(jnp.arange(n_blocks, dtype=jnp.int32), n_used - 1)
    block_e = jnp.minimum(jnp.sum((blk[:, None] * MOE_BM >= pad_ends[None, :]).astype(jnp.int32), axis=1),
                          N_EXPERTS - 1)
    e = rec[:, R_E0:R_E1 + 1].astype(jnp.int32)
    rank = rec[:, R_RANK0:R_RANK1 + 1].astype(jnp.int32)
    start = jnp.zeros_like(e)
    for k in range(N_EXPERTS):
        start = jnp.where(e == k, pad_starts[k], start)
    dest = start + rank
    last_blk = jnp.where(cnt > 0, pad_ends // MOE_BM - 1, -1).astype(jnp.int32)
    return dest, block_e.astype(jnp.int32), n_used.astype(jnp.int32).reshape(1), last_blk, n_blocks


DISPATCH_TM = 256


def _dispatch_kernel(last_ref, nb_ref, dest_ref, h_ref, xb_ref, zero_ref, sem, zsem):
    i = pl.program_id(0)
    tm = DISPATCH_TM
    n_blocks = xb_ref.shape[0] // MOE_BM

    def clear_block(blk):
        start = pl.multiple_of(blk * MOE_BM, MOE_BM)
        cp = pltpu.make_async_copy(zero_ref, xb_ref.at[pl.ds(start, MOE_BM), :], zsem)
        cp.start()
        cp.wait()

    @pl.when(i == 0)
    def _():
        zero_ref[...] = jnp.zeros_like(zero_ref)
        for e in range(N_EXPERTS):
            @pl.when(last_ref[e] >= 0)
            def _():
                clear_block(last_ref[e])

            @pl.when(nb_ref[0] + e < n_blocks)
            def _():
                clear_block(nb_ref[0] + e)

    for r in range(tm):
        for k in range(TOP_K):
            pltpu.make_async_copy(h_ref.at[pl.ds(r, 1), :],
                                  xb_ref.at[pl.ds(dest_ref[0, TOP_K * r + k], 1), :], sem).start(priority=k)
    for k in range(TOP_K):
        pltpu.make_async_copy(h_ref, xb_ref.at[pl.ds(0, tm), :], sem).wait()


def _dispatch(h2, dest, last_blk, n_used, n_blocks):
    n, d = h2.shape
    tm = DISPATCH_TM
    cap = n_blocks * MOE_BM
    grid_spec = pltpu.PrefetchScalarGridSpec(
        num_scalar_prefetch=2,
        grid=(n // tm,),
        in_specs=[pl.BlockSpec((None, 1, TOP_K * tm), lambda i, lb, nb: (i, 0, 0), memory_space=pltpu.SMEM),
                  pl.BlockSpec((tm, d), lambda i, lb, nb: (i, 0))],
        out_specs=pl.BlockSpec(memory_space=pl.ANY),
        scratch_shapes=[pltpu.VMEM((MOE_BM, d), F32), pltpu.SemaphoreType.DMA, pltpu.SemaphoreType.DMA],
    )
    return pl.pallas_call(
        _dispatch_kernel,
        out_shape=jax.ShapeDtypeStruct((cap, d), F32),
        grid_spec=grid_spec,
        compiler_params=_cparams(("arbitrary",)),
        name="moe_dispatch",
    )(last_blk, n_used, dest.reshape(n // tm, 1, TOP_K * tm), h2)


MOE_TF = 512


def _moe_kernel(be_ref, nb_ref, x_ref, wg_ref, wu_ref, w2_ref, o_ref, h_ref, acc_ref):
    i, j = pl.program_id(0), pl.program_id(1)

    @pl.when(i < nb_ref[0])
    def _():
        @pl.when(j == 0)
        def _():
            h_ref[...] = x_ref[...].astype(BF16)
            acc_ref[...] = jnp.zeros_like(acc_ref)

        h = h_ref[...]
        gt = jnp.dot(h, wg_ref[...].astype(BF16), preferred_element_type=F32)
        up = jnp.dot(h, wu_ref[...].astype(BF16), preferred_element_type=F32)
        acc_ref[...] += jnp.dot((_silu(gt) * up).astype(BF16), w2_ref[...].astype(BF16),
                                preferred_element_type=F32)

        @pl.when(j == pl.num_programs(1) - 1)
        def _():
            o_ref[...] = acc_ref[...]

    @pl.when(jnp.logical_and(i >= nb_ref[0], j == pl.num_programs(1) - 1))
    def _():
        o_ref[...] = jnp.zeros_like(o_ref)


def _moe_blocks(xb, block_e, n_used, w13, w2, li):
    cap, d = xb.shape
    _, ne, ff, _ = w2.shape
    tf = MOE_TF
    nf = ff // tf
    n_blocks = cap // MOE_BM

    def row(i, j, be, nb):
        return (jnp.minimum(i, nb[0] - 1), 0)

    def jj(i, j, nb):
        return jnp.where(i < nb[0], j, nf - 1)

    grid_spec = pltpu.PrefetchScalarGridSpec(
        num_scalar_prefetch=2,
        grid=(n_blocks, nf),
        in_specs=[
            pl.BlockSpec((MOE_BM, d), row),
            pl.BlockSpec((None, None, d, tf), lambda i, j, be, nb: (li, be[i], 0, jj(i, j, nb))),
            pl.BlockSpec((None, None, d, tf), lambda i, j, be, nb: (li, be[i], 0, jj(i, j, nb) + nf)),
            pl.BlockSpec((None, None, tf, d), lambda i, j, be, nb: (li, be[i], jj(i, j, nb), 0)),
        ],
        out_specs=pl.BlockSpec((MOE_BM, d), lambda i, j, be, nb: (i, 0)),
        scratch_shapes=[pltpu.VMEM((MOE_BM, d), BF16), pltpu.VMEM((MOE_BM, d), F32)],
    )
    return pl.pallas_call(
        _moe_kernel,
        out_shape=jax.ShapeDtypeStruct((cap, d), F32),
        grid_spec=grid_spec,
        compiler_params=_cparams(("arbitrary", "arbitrary")),
        name="moe_experts",
    )(block_e, n_used, xb, w13, w13, w2)


COMBINE_TM = 256


def _combine_kernel(final, dest_ref, dnext_ref, x_ref, m_ref, rec_ref, yb_ref, *rest):
    if final:
        g_ref, o_ref, buf_ref, sem = rest
    else:
        o_ref, buf_ref, sem = rest
    tm = COMBINE_TM
    i, n = pl.program_id(0), pl.num_programs(0)

    def issue(idx_ref, slot):
        for r in range(tm):
            for k in range(TOP_K):
                pltpu.make_async_copy(yb_ref.at[pl.ds(idx_ref[0, TOP_K * r + k], 1), :],
                                      buf_ref.at[slot, k, pl.ds(r, 1), :], sem.at[slot]).start(priority=k)

    def finish(slot):
        for k in range(TOP_K):
            pltpu.make_async_copy(yb_ref.at[pl.ds(0, tm), :], buf_ref.at[slot, k], sem.at[slot]).wait()
        rec = rec_ref[...]
        y = rec[:, R_W0:R_W0 + 1] * buf_ref[slot, 0] + rec[:, R_W1:R_W1 + 1] * buf_ref[slot, 1]
        xn = x_ref[...] + m_ref[5] * y
        if final:
            xn = xn * lax.rsqrt(jnp.mean(xn * xn, axis=-1, keepdims=True) + EPS) * g_ref[...]
        o_ref[...] = xn

    @pl.when(i == 0)
    def _():
        issue(dest_ref, 0)

    for slot in range(2):
        @pl.when(i % 2 == slot)
        def _():
            @pl.when(i + 1 < n)
            def _():
                issue(dnext_ref, 1 - slot)

            finish(slot)


def _combine(lay, x, mods, rec, dest, yb, final_g=None):
    n, d = x.shape
    tm = COMBINE_TM
    final = final_g is not None
    off = lay.n_ctx_rows // tm if final else 0
    n_out = n - off * tm
    row = lambda i: (i + off, 0)
    last = n // tm - 1
    in_specs = [pl.BlockSpec((None, 1, TOP_K * tm), lambda i: (i + off, 0, 0), memory_space=pltpu.SMEM),
                pl.BlockSpec((None, 1, TOP_K * tm), lambda i: (jnp.minimum(i + off + 1, last), 0, 0),
                             memory_space=pltpu.SMEM),
                pl.BlockSpec((tm, d), row),
                pl.BlockSpec((None, 6, 1, d), lambda i: (lay.mod_row(i + off, tm), 0, 0, 0)),
                pl.BlockSpec((tm, LANES), row),
                pl.BlockSpec(memory_space=pl.ANY)]
    dest_tiles = dest.reshape(n // tm, 1, TOP_K * tm)
    args = [dest_tiles, dest_tiles, x, mods, rec, yb]
    if final:
        in_specs.append(pl.BlockSpec((1, d), lambda i: (0, 0)))
        args.append(final_g.reshape(1, -1))
    return pl.pallas_call(
        functools.partial(_combine_kernel, final),
        out_shape=jax.ShapeDtypeStruct((n_out, d), F32),
        grid=(n_out // tm,),
        in_specs=in_specs,
        out_specs=pl.BlockSpec((tm, d), lambda i: (i, 0)),
        scratch_shapes=[pltpu.VMEM((2, TOP_K, tm, d), F32), pltpu.SemaphoreType.DMA((2,))],
        compiler_params=_cparams(("arbitrary",)),
        name="moe_combine_final" if final else "moe_combine",
    )(*args)


def _final_kernel(x_ref, g_ref, o_ref):
    x = x_ref[...]
    o_ref[...] = x * lax.rsqrt(jnp.mean(x * x, axis=-1, keepdims=True) + EPS) * g_ref[...]


def _final_norm(lay, x, g):
    n, d = x.shape
    tm = _row_tile(lay, 1024)
    off = lay.n_ctx_rows // tm
    n_lat = lay.b * lay.seq
    return pl.pallas_call(
        _final_kernel,
        out_shape=jax.ShapeDtypeStruct((n_lat, d), F32),
        grid=(n_lat // tm,),
        in_specs=[pl.BlockSpec((tm, d), lambda i: (i + off, 0)), pl.BlockSpec((1, d), lambda i: (0, 0))],
        out_specs=pl.BlockSpec((tm, d), lambda i: (i, 0)),
        compiler_params=_cparams(("arbitrary",)),
        name="final_norm",
    )(x, g.reshape(1, -1))


def _permute_w_in(w):
    d = w.shape[0]
    parts = [w[:, COL_GATE:COL_GATE + N_BRANCH * D_MODEL], w[:, COL_Q:COL_G_END],
             w[:, COL_POOL:COL_POOL + POOL_WIDTH], w[:, COL_Z:COL_Z + SSD_INNER],
             w[:, COL_XBC:COL_XBC + SSD_XBC]]
    w_dt = jnp.pad(w[:, COL_DT:COL_DT + 2 * SSD_HEADS], ((0, 0), (0, LANES - 2 * SSD_HEADS)))
    return jnp.concatenate(parts, axis=1).astype(BF16), w_dt.astype(BF16)


def kernel(x, c, ctx, c_ctx, ada_w, ada_b, norm1_g, norm2_g, w_in, ssd_conv_w, ssd_conv_b, ssd_dt_bias,
           ssd_a_log, ssd_d, ssd_norm_g, pool_w, pool_scale, ret_decay_logit, w_branch, w_out,
           ffn_w13, ffn_w2, moe_router, moe_w13, moe_w2, final_norm_g):
    b, seq, d = x.shape
    ctx_len = ctx.shape[1]
    depth = w_in.shape[0]
    lay = _Layout(b, ctx_len, seq)

    cvec = jnp.concatenate([c, c_ctx[None, :], jnp.zeros((8 - b - 1, d), F32)], axis=0)
    mods_all = _ada_all(cvec, ada_w, ada_b).reshape(depth, 8, 6, 1, d)
    xa = jnp.concatenate([ctx.reshape(-1, d), x.reshape(-1, d)], axis=0)

    for layer in range(depth):
        mods = mods_all[layer]
        w_main, w_dt = _permute_w_in(w_in[layer])
        p, dt_raw = _in_proj(lay, xa, mods, norm1_g[layer].reshape(1, -1), w_main, w_dt)
        ssd_f, ssd_b = _ssd_scan(lay, p, dt_raw, ssd_conv_w[layer], ssd_conv_b[layer], ssd_dt_bias[layer],
                                 ssd_a_log[layer], ssd_d[layer])
        ret_f, ret_b = _ret_scan(lay, p, ret_decay_logit[layer])
        is_moe = layer % 2 == 1
        if is_moe:
            w_r = jnp.pad(moe_router[layer // 2], ((0, 0), (0, LANES - N_EXPERTS)))
        else:
            w_r = jnp.zeros((d, LANES), F32)
        outs = _merge(lay, xa, mods, p, ssd_f, ssd_b, ret_f, ret_b, ssd_norm_g[layer], pool_w[layer],
                      pool_scale[layer], w_branch[layer], w_out[layer], norm2_g[layer], w_r, is_moe)
        if not is_moe:
            (xa,) = outs
            xa = _ffn_dense(lay, xa, mods, norm2_g[layer].reshape(1, -1),
                            ffn_w13[layer // 2].astype(BF16), ffn_w2[layer // 2].astype(BF16))
        else:
            xa, h2, logits = outs
            rec, counts = _route(logits)
            dest, block_e, n_used, last_blk, n_blocks = _slot_plan(rec, counts)
            xb = _dispatch(h2, dest, last_blk, n_used, n_blocks)
            yb = _moe_blocks(xb, block_e, n_used, moe_w13, moe_w2, layer // 2)
            if layer == depth - 1:
                return _combine(lay, xa, mods, rec, dest, yb, final_norm_g).reshape(b, seq, d)
            xa = _combine(lay, xa, mods, rec, dest, yb)
    return _final_norm(lay, xa, final_norm_g).reshape(b, seq, d)
```

```python
import functools

import numpy as np
import jax
import jax.numpy as jnp
from jax import lax
from jax.experimental import pallas as pl
from jax.experimental.pallas import tpu as pltpu

F32 = jnp.float32
BF16 = jnp.bfloat16
HIGHEST = lax.Precision.HIGHEST

D_MODEL = 1024
GRID_W = 64
EPS = 1e-6
CHUNK = 128
SCAN_SUB = 2
SCAN_BLK = SCAN_SUB * CHUNK
HALO = 16
SSD_HEADS = 8
SSD_HEAD_DIM = 64
SSD_INNER = 512
SSD_STATE = 64
SSD_GROUPS = 2
SSD_CONV = 5
SSD_XBC = 768
POOL_WINDOWS = (2, 4, 8, 16)
POOL_WIDTH = 512
POOL_GROUP = 128
RET_HEADS = 8
RET_DIM = 64
RET_WIDTH = 512
ROPE_BASE = 10000.0
N_BRANCH = 3
N_EXPERTS = 8
TOP_K = 2
LANES = 128
HEAD_GROUP = 4
GROUP_W = HEAD_GROUP * 64

COL_Z = 0
COL_XBC = 512
COL_DT = 1280
COL_POOL = 1296
COL_Q = 1808
COL_G_END = 3856
COL_GATE = 3856
IN_COLS = 6928
P_GATE = 0
P_Q = 3072
P_K = 3584
P_V = 4096
P_G = 4608
P_POOL = 5120
P_Z = 5632
P_XBC = 6144
P_COLS = 6912

VMEM_LIMIT = 56 * 1024 * 1024


def _sigmoid(v):
    return 0.5 * jnp.tanh(0.5 * v) + 0.5


def _silu(v):
    return v * _sigmoid(v)


def _softplus(v):
    return jnp.maximum(v, 0.0) + jnp.log1p(jnp.exp(-jnp.abs(v)))


def _log_sigmoid(v):
    return -_softplus(-v)


def _cparams(sem):
    return pltpu.CompilerParams(dimension_semantics=sem, vmem_limit_bytes=VMEM_LIMIT)


def _split3(x):
    hi = x.astype(BF16)
    r = x - hi.astype(F32)
    mid = r.astype(BF16)
    lo = (r - mid.astype(F32)).astype(BF16)
    return hi, mid, lo


def _dot_sel_right(x, m):
    return sum(jnp.dot(part, m, preferred_element_type=F32) for part in _split3(x))


def _dot_sel_left(m, x):
    return sum(jnp.dot(m, part, preferred_element_type=F32) for part in _split3(x))


SHIFT_PAD = 64


def _band_consts(rows, windows):
    m = np.zeros((len(windows), rows, rows + 2 * SHIFT_PAD), np.float32)
    t = np.arange(rows)
    for i, (lo, hi) in enumerate(windows):
        for o in range(lo, hi + 1):
            m[i, t, SHIFT_PAD + t + o] = 1.0
    return m


def _block_mask(rows, cols, row_blk, col_blk):
    r = lax.broadcasted_iota(jnp.int32, (rows, cols), 0) // row_blk
    c = lax.broadcasted_iota(jnp.int32, (rows, cols), 1) // col_blk
    return r == c


def _ada_kernel(c_ref, w_ref, b_ref, o_ref):
    cv = c_ref[...]
    o_ref[...] = jnp.dot(_silu(cv), w_ref[...], precision=HIGHEST,
                         preferred_element_type=F32) + b_ref[...]


def _ada_all(cvec, ada_w, ada_b):
    depth, d, n6 = ada_w.shape
    tn = 1536
    return pl.pallas_call(
        _ada_kernel,
        out_shape=jax.ShapeDtypeStruct((depth, 8, n6), F32),
        grid=(depth, n6 // tn),
        in_specs=[
            pl.BlockSpec((8, d), lambda l, j: (0, 0)),
            pl.BlockSpec((None, d, tn), lambda l, j: (l, 0, j)),
            pl.BlockSpec((None, 1, tn), lambda l, j: (l, 0, j)),
        ],
        out_specs=pl.BlockSpec((None, 8, tn), lambda l, j: (l, 0, j)),
        compiler_params=_cparams(("arbitrary", "arbitrary")),
        name="ada_mods",
    )(cvec, ada_w, ada_b.reshape(depth, 1, n6))


class _Layout:
    def __init__(self, b, ctx_len, seq):
        self.b, self.ctx, self.seq = b, ctx_len, seq
        self.n_ctx_rows = b * ctx_len
        self.n = b * (ctx_len + seq)
        assert ctx_len % SCAN_BLK == 0 and seq % SCAN_BLK == 0
        self.cblk = ctx_len // SCAN_BLK
        self.lblk = seq // SCAN_BLK
        self.nblk = self.cblk + self.lblk

    def mod_row(self, tile, tm):
        ctx_tiles = self.n_ctx_rows // tm
        per_b = self.seq // tm
        return jnp.where(tile < ctx_tiles, self.b, (tile - ctx_tiles) // per_b)

    def scan_row_block(self, bi, c):
        return jnp.where(c < self.cblk, bi * self.cblk + c,
                         self.b * self.cblk + bi * self.lblk + (c - self.cblk))

    def fwd_block(self, s):
        return s

    def bwd_block(self, s):
        return jnp.where(s < self.cblk, self.cblk - 1 - s, self.nblk - 1 - (s - self.cblk))


def _row_tile(lay, cap):
    tm = cap
    while lay.n_ctx_rows % tm or lay.seq % tm:
        tm //= 2
    return tm


def _in_kernel(x_ref, m_ref, g_ref, w_ref, wdt_ref, o_ref, dt_ref, h_ref):
    @pl.when(pl.program_id(1) == 0)
    def _():
        x = x_ref[...]
        y = x * lax.rsqrt(jnp.mean(x * x, axis=-1, keepdims=True) + EPS) * g_ref[...]
        h_ref[...] = (y * (1.0 + m_ref[1]) + m_ref[0]).astype(BF16)
        dt_ref[...] = jnp.dot(h_ref[...], wdt_ref[...], preferred_element_type=F32)

    o_ref[...] = jnp.dot(h_ref[...], w_ref[...], preferred_element_type=F32).astype(BF16)


def _in_proj(lay, x, mods, g, w, w_dt):
    n, d = x.shape
    tm = _row_tile(lay, 1024)
    tn = 2304
    return pl.pallas_call(
        _in_kernel,
        out_shape=(jax.ShapeDtypeStruct((n, P_COLS), BF16), jax.ShapeDtypeStruct((n, LANES), F32)),
        grid=(n // tm, P_COLS // tn),
        in_specs=[
            pl.BlockSpec((tm, d), lambda i, j: (i, 0)),
            pl.BlockSpec((None, 6, 1, d), lambda i, j: (lay.mod_row(i, tm), 0, 0, 0)),
            pl.BlockSpec((1, d), lambda i, j: (0, 0)),
            pl.BlockSpec((d, tn), lambda i, j: (0, j)),
            pl.BlockSpec((d, LANES), lambda i, j: (0, 0)),
        ],
        out_specs=(pl.BlockSpec((tm, tn), lambda i, j: (i, j)), pl.BlockSpec((tm, LANES), lambda i, j: (i, 0))),
        scratch_shapes=[pltpu.VMEM((tm, d), BF16)],
        compiler_params=_cparams(("arbitrary", "arbitrary")),
        name="in_proj",
    )(x, mods, g, w, w_dt)


def _tri_consts():
    i = np.arange(CHUNK)
    fwd = (i[None, :] <= i[:, None]).astype(np.float32)
    bwd = (i[None, :] >= i[:, None]).astype(np.float32)
    return np.stack([fwd, bwd])


def _expand_consts(heads, width):
    e = np.zeros((2, LANES, heads * width), np.float32)
    for d in range(2):
        for h in range(heads):
            e[d, d * heads + h, h * width:(h + 1) * width] = 1.0
    return e


def _ssd_direction(d, c, lay, xm_ref, xp_ref, xn_ref, dt_ref, tri_ref, exp_ref, cw_ref, cb_ref,
                   dtb_ref, alog_ref, dskip_ref, st_ref, o_ref):
    is_start = jnp.logical_or(c == 0, c == lay.cblk)
    is_end = jnp.logical_or(c == lay.cblk - 1, c == lay.nblk - 1)
    xm = xm_ref[...].astype(F32)
    xw = jnp.concatenate([jnp.where(is_start, 0.0, xp_ref[...].astype(F32)), xm,
                          jnp.where(is_end, 0.0, xn_ref[...].astype(F32))], axis=0)
    mid = SSD_CONV // 2
    acc = cb_ref[...] + cw_ref[mid:mid + 1, :] * xm
    for k in range(SSD_CONV):
        if k != mid:
            shifted = pltpu.roll(xw, (mid - k) % xw.shape[0], 0)[HALO:HALO + SCAN_BLK, :]
            acc = acc + cw_ref[k:k + 1, :] * shifted
    xbc_blk = _silu(acc)
    dt_blk = _softplus(dt_ref[...] + dtb_ref[...])
    for k in (range(SCAN_SUB) if d == 0 else reversed(range(SCAN_SUB))):
        rows = slice(k * CHUNK, (k + 1) * CHUNK)
        _ssd_chunk(d, xbc_blk[rows], dt_blk[rows], tri_ref, exp_ref, alog_ref, dskip_ref, st_ref,
                   o_ref.at[pl.ds(k * CHUNK, CHUNK), :])


def _ssd_chunk(d, xbc, dt_all, tri_ref, exp_ref, alog_ref, dskip_ref, st_ref, o_ref):
    xs = xbc[:, :SSD_INNER]
    bm = xbc[:, SSD_INNER:SSD_INNER + LANES]
    cm = xbc[:, SSD_INNER + LANES:]
    bt = bm.T.astype(BF16)
    top = lax.broadcasted_iota(jnp.int32, (CHUNK, LANES), 0) < SSD_STATE
    zero = jnp.zeros_like(bt)
    bt_bd = jnp.concatenate([jnp.where(top, bt, zero), jnp.where(top, zero, bt)], axis=1)
    cb_all = jnp.dot(cm.astype(BF16), bt_bd, preferred_element_type=F32)

    tri = tri_ref[d]
    acs = _dot_sel_left(tri, dt_all * (-jnp.exp(alog_ref[...])))
    acs_t = acs.T
    dt_x = _dot_sel_right(dt_all, exp_ref[d])
    acs_x = _dot_sel_right(acs, exp_ref[d])
    last = CHUNK - 1 if d == 0 else 0
    tot_x = acs_x[last:last + 1, :]
    u = xs * dt_x
    ud = (u * jnp.exp(tot_x - acs_x)).astype(BF16)
    ub = u.astype(BF16)
    off_x = jnp.exp(acs_x)
    cd_x = jnp.exp(tot_x)

    li = lax.broadcasted_iota(jnp.int32, (CHUNK, CHUNK), 0)
    si = lax.broadcasted_iota(jnp.int32, (CHUNK, CHUNK), 1)
    mask = (si <= li) if d == 0 else (si >= li)
    lane = lax.broadcasted_iota(jnp.int32, (CHUNK, LANES), 1)
    cm_sw = pltpu.roll(cm, SSD_STATE, 1)
    u_mask = _block_mask(HEAD_GROUP * CHUNK, GROUP_W, CHUNK, SSD_HEAD_DIM)
    s_mask = _block_mask(GROUP_W, GROUP_W, SSD_STATE, SSD_HEAD_DIM)
    for g in range(SSD_GROUPS):
        gl = slice(g * GROUP_W, (g + 1) * GROUP_W)
        cb = cb_all[:, g * CHUNK:(g + 1) * CHUNK]
        parts = []
        for hh in range(HEAD_GROUP):
            h = g * HEAD_GROUP + hh
            col = acs_x[:, h * SSD_HEAD_DIM:h * SSD_HEAD_DIM + 1]
            row = acs_t[d * SSD_HEADS + h:d * SSD_HEADS + h + 1, :]
            lm = jnp.exp(jnp.where(mask, col - row, -jnp.inf))
            parts.append((cb * lm).astype(BF16))
        in_g = (lane < SSD_STATE) if g == 0 else (lane >= SSD_STATE)
        c_rep = jnp.where(in_g, cm, cm_sw)
        c_off = jnp.concatenate([c_rep, c_rep], axis=1) * off_x[:, gl]
        parts.append(c_off.astype(BF16))
        lhs = jnp.concatenate(parts, axis=1)
        ub_g = ub[:, gl]
        u_bd = jnp.where(u_mask, jnp.concatenate([ub_g] * HEAD_GROUP, axis=0), jnp.zeros((), BF16))
        st = st_ref[d, g]
        rhs = jnp.concatenate([u_bd, st.astype(BF16)], axis=0)
        y_g = jnp.dot(lhs, rhs, preferred_element_type=F32)
        if d == 0:
            y_g = y_g + dskip_ref[:, gl] * xs[:, gl]
        o_ref[:, gl] = y_g
        bt_g = bt[g * SSD_STATE:(g + 1) * SSD_STATE, :]
        upd = jnp.dot(jnp.concatenate([bt_g] * HEAD_GROUP, axis=0), ud[:, gl], preferred_element_type=F32)
        st_ref[d, g] = st * cd_x[:, gl] + jnp.where(s_mask, upd, 0.0)


def _ssd_kernel(lay, xm_f, xp_f, xn_f, dt_f, xm_b, xp_b, xn_b, dt_b, tri_ref, exp_ref, cw_ref, cb_ref,
                dtb_ref, alog_ref, dskip_ref, of_ref, ob_ref, st_ref):
    s = pl.program_id(1)

    @pl.when(s == 0)
    def _():
        st_ref[...] = jnp.zeros_like(st_ref)

    _ssd_direction(0, lay.fwd_block(s), lay, xm_f, xp_f, xn_f, dt_f, tri_ref, exp_ref, cw_ref, cb_ref,
                   dtb_ref, alog_ref, dskip_ref, st_ref, of_ref)
    _ssd_direction(1, lay.bwd_block(s), lay, xm_b, xp_b, xn_b, dt_b, tri_ref, exp_ref, cw_ref, cb_ref,
                   dtb_ref, alog_ref, dskip_ref, st_ref, ob_ref)


def _halo_specs(lay, width, col_block, block_of):
    n_halo = lay.n // HALO
    per = SCAN_BLK // HALO

    def main(bi, s):
        return (lay.scan_row_block(bi, block_of(s)), col_block)

    def prev(bi, s):
        return (jnp.maximum(lay.scan_row_block(bi, block_of(s)) * per - 1, 0), col_block)

    def nxt(bi, s):
        return (jnp.minimum(lay.scan_row_block(bi, block_of(s)) * per + per, n_halo - 1), col_block)

    return [pl.BlockSpec((SCAN_BLK, width), main), pl.BlockSpec((HALO, width), prev),
            pl.BlockSpec((HALO, width), nxt)]


def _ssd_scan(lay, p, dt_raw, conv_w, conv_b, dt_bias, a_log, d_skip):
    n = lay.n
    pad = LANES - 2 * SSD_HEADS
    dtb = jnp.pad(dt_bias.reshape(1, -1), ((0, 0), (0, pad)))
    alog = jnp.pad(a_log.reshape(1, -1), ((0, 0), (0, pad)))
    dskip = jnp.repeat(d_skip, SSD_HEAD_DIM).reshape(1, SSD_INNER)
    tri = jnp.asarray(_tri_consts(), BF16)
    expand = jnp.asarray(_expand_consts(SSD_HEADS, SSD_HEAD_DIM), BF16)
    xbc_blk = P_XBC // SSD_XBC

    def dt_spec(block_of):
        return pl.BlockSpec((SCAN_BLK, LANES), lambda bi, s: (lay.scan_row_block(bi, block_of(s)), 0))

    def out_spec(block_of):
        return pl.BlockSpec((SCAN_BLK, SSD_INNER), lambda bi, s: (lay.scan_row_block(bi, block_of(s)), 0))

    const2 = lambda bi, s: (0, 0)
    const3 = lambda bi, s: (0, 0, 0)
    in_specs = (
        _halo_specs(lay, SSD_XBC, xbc_blk, lay.fwd_block) + [dt_spec(lay.fwd_block)]
        + _halo_specs(lay, SSD_XBC, xbc_blk, lay.bwd_block) + [dt_spec(lay.bwd_block)]
        + [pl.BlockSpec((2, CHUNK, CHUNK), const3),
           pl.BlockSpec((2, LANES, SSD_INNER), const3),
           pl.BlockSpec((SSD_CONV, SSD_XBC), const2),
           pl.BlockSpec((1, SSD_XBC), const2),
           pl.BlockSpec((1, LANES), const2),
           pl.BlockSpec((1, LANES), const2),
           pl.BlockSpec((1, SSD_INNER), const2)])
    return pl.pallas_call(
        functools.partial(_ssd_kernel, lay),
        out_shape=(jax.ShapeDtypeStruct((n, SSD_INNER), F32), jax.ShapeDtypeStruct((n, SSD_INNER), F32)),
        grid=(lay.b, lay.nblk),
        in_specs=in_specs,
        out_specs=(out_spec(lay.fwd_block), out_spec(lay.bwd_block)),
        scratch_shapes=[pltpu.VMEM((2, SSD_GROUPS, GROUP_W, GROUP_W), F32)],
        compiler_params=_cparams(("arbitrary", "arbitrary")),
        name="ssd_scan",
    )(p, p, p, dt_raw, p, p, p, dt_raw, tri, expand, conv_w, conv_b.reshape(1, -1), dtb, alog, dskip)


def _rope_tables(lay):
    n_axis = RET_DIM // 4
    t = np.arange(lay.seq)
    inv = ROPE_BASE ** (-np.arange(n_axis, dtype=np.float32) / n_axis)
    row = (t // GRID_W).astype(np.float32)
    colp = (t % GRID_W).astype(np.float32)
    ang = jnp.concatenate([jnp.asarray(row)[:, None] * inv, jnp.asarray(colp)[:, None] * inv], axis=-1)
    cos, sin = jnp.cos(ang), jnp.sin(ang)
    cos_l = jnp.concatenate([cos, cos, cos, cos], axis=-1)
    sin_l = jnp.concatenate([-sin, sin, -sin, sin], axis=-1)
    cos_t = jnp.concatenate([jnp.ones((lay.ctx, LANES), F32), cos_l], axis=0)
    sin_t = jnp.concatenate([jnp.zeros((lay.ctx, LANES), F32), sin_l], axis=0)
    return cos_t, sin_t


def _rope(xv, cos, sin):
    lane = lax.broadcasted_iota(jnp.int32, (xv.shape[0], LANES), 1)
    first_half = (lane % RET_DIM) < (RET_DIM // 2)
    out = []
    for j in range(RET_WIDTH // LANES):
        v = xv[:, j * LANES:(j + 1) * LANES]
        swapped = jnp.where(first_half, pltpu.roll(v, LANES - RET_DIM // 2, 1),
                            pltpu.roll(v, RET_DIM // 2, 1))
        out.append(v * cos + swapped * sin)
    return jnp.concatenate(out, axis=-1)


def _ret_tables(lgx_ref, lgp_ref, kdec_ref, qdec_ref, cd_ref, dmat_ref):
    idx = lax.broadcasted_iota(jnp.int32, (CHUNK, 1), 0).astype(F32)
    ii = lax.broadcasted_iota(jnp.int32, (CHUNK, CHUNK), 0)
    mi = lax.broadcasted_iota(jnp.int32, (CHUNK, CHUNK), 1)
    for d in range(2):
        lg_x = _log_sigmoid(lgx_ref[d])
        lg_p = _log_sigmoid(lgp_ref[d])
        if d == 0:
            k_pow, q_pow, diff = (CHUNK - 1) - idx, idx + 1.0, ii - mi
        else:
            k_pow, q_pow, diff = idx, CHUNK - idx, mi - ii
        kdec_ref[d] = jnp.exp(lg_x * k_pow)
        qdec_ref[d] = jnp.exp(lg_x * q_pow)
        cd_ref[d] = jnp.exp(lg_x * float(CHUNK))
        dpos = jnp.maximum(diff, 0).astype(F32)
        for h in range(RET_HEADS):
            dmat_ref[d, :, h * CHUNK:(h + 1) * CHUNK] = jnp.where(
                diff >= 0, jnp.exp(lg_p[:, h:h + 1] * dpos), 0.0)


def _ret_direction(d, q_ref, k_ref, v_ref, cos_ref, sin_ref, kdec_ref, qdec_ref, cd_ref, dmat_ref,
                   st_ref, o_ref):
    cos, sin = cos_ref[...], sin_ref[...]
    q_blk = _rope(q_ref[...].astype(F32), cos, sin)
    k_blk = _rope(k_ref[...].astype(F32), cos, sin) * (RET_DIM ** -0.5)
    v_blk = v_ref[...].astype(F32)
    for c in (range(SCAN_SUB) if d == 0 else reversed(range(SCAN_SUB))):
        rows = slice(c * CHUNK, (c + 1) * CHUNK)
        _ret_chunk(d, q_blk[rows], k_blk[rows], v_blk[rows], kdec_ref, qdec_ref, cd_ref, dmat_ref, st_ref,
                   o_ref.at[pl.ds(c * CHUNK, CHUNK), :])


def _ret_chunk(d, q, k, v, kdec_ref, qdec_ref, cd_ref, dmat_ref, st_ref, o_ref):
    vk = (v * kdec_ref[d]).astype(BF16)
    qd = (q * qdec_ref[d]).astype(BF16)
    cd = cd_ref[d]
    qb, vb = q.astype(BF16), v.astype(BF16)
    kt = k.T.astype(BF16)
    k_mask = _block_mask(GROUP_W, HEAD_GROUP * CHUNK, RET_DIM, CHUNK)
    v_mask = _block_mask(HEAD_GROUP * CHUNK, GROUP_W, CHUNK, RET_DIM)
    s_mask = _block_mask(GROUP_W, GROUP_W, RET_DIM, RET_DIM)
    zero = jnp.zeros((), BF16)
    for g in range(RET_HEADS // HEAD_GROUP):
        gl = slice(g * GROUP_W, (g + 1) * GROUP_W)
        sl = slice(g * HEAD_GROUP * CHUNK, (g + 1) * HEAD_GROUP * CHUNK)
        kt_g = kt[gl, :]
        k_bd = jnp.where(k_mask, jnp.concatenate([kt_g] * HEAD_GROUP, axis=1), zero)
        s_all = jnp.dot(qb[:, gl], k_bd, preferred_element_type=F32)
        inner = (s_all * dmat_ref[d, :, sl]).astype(BF16)
        lhs = jnp.concatenate([inner, qd[:, gl]], axis=1)
        v_bd = jnp.where(v_mask, jnp.concatenate([vb[:, gl]] * HEAD_GROUP, axis=0), zero)
        st = st_ref[d, g]
        rhs = jnp.concatenate([v_bd, st.astype(BF16)], axis=0)
        o_ref[:, gl] = jnp.dot(lhs, rhs, preferred_element_type=F32)
        upd = jnp.dot(kt_g, vk[:, gl], preferred_element_type=F32)
        st_ref[d, g] = st * cd[:, gl] + jnp.where(s_mask, upd, 0.0)


def _ret_kernel(qf, kf, vf, cosf, sinf, qb, kb, vb, cosb, sinb, lgx_ref, lgp_ref, of_ref, ob_ref,
                st_ref, kdec_ref, qdec_ref, cd_ref, dmat_ref):
    @pl.when(jnp.logical_and(pl.program_id(0) == 0, pl.program_id(1) == 0))
    def _():
        _ret_tables(lgx_ref, lgp_ref, kdec_ref, qdec_ref, cd_ref, dmat_ref)

    @pl.when(pl.program_id(1) == 0)
    def _():
        st_ref[...] = jnp.zeros_like(st_ref)

    _ret_direction(0, qf, kf, vf, cosf, sinf, kdec_ref, qdec_ref, cd_ref, dmat_ref, st_ref, of_ref)
    _ret_direction(1, qb, kb, vb, cosb, sinb, kdec_ref, qdec_ref, cd_ref, dmat_ref, st_ref, ob_ref)


def _ret_scan(lay, p, decay_logit):
    n = lay.n
    cos_t, sin_t = _rope_tables(lay)
    lgx = jnp.repeat(decay_logit, RET_DIM, axis=-1).reshape(2, 1, RET_WIDTH)
    lgp = jnp.pad(decay_logit, ((0, 0), (0, LANES - RET_HEADS))).reshape(2, 1, LANES)

    def specs(block_of):
        def blk(cb):
            return pl.BlockSpec((SCAN_BLK, RET_WIDTH), lambda bi, s: (lay.scan_row_block(bi, block_of(s)), cb))
        tab = pl.BlockSpec((SCAN_BLK, LANES), lambda bi, s: (block_of(s), 0))
        return [blk(P_Q // RET_WIDTH), blk(P_K // RET_WIDTH), blk(P_V // RET_WIDTH), tab, tab]

    def out_spec(block_of):
        return pl.BlockSpec((SCAN_BLK, RET_WIDTH), lambda bi, s: (lay.scan_row_block(bi, block_of(s)), 0))

    const3 = lambda bi, s: (0, 0, 0)
    return pl.pallas_call(
        _ret_kernel,
        out_shape=(jax.ShapeDtypeStruct((n, RET_WIDTH), F32), jax.ShapeDtypeStruct((n, RET_WIDTH), F32)),
        grid=(lay.b, lay.nblk),
        in_specs=specs(lay.fwd_block) + specs(lay.bwd_block)
        + [pl.BlockSpec((2, 1, RET_WIDTH), const3), pl.BlockSpec((2, 1, LANES), const3)],
        out_specs=(out_spec(lay.fwd_block), out_spec(lay.bwd_block)),
        scratch_shapes=[pltpu.VMEM((2, RET_HEADS // HEAD_GROUP, GROUP_W, GROUP_W), F32),
                        pltpu.VMEM((2, CHUNK, RET_WIDTH), F32),
                        pltpu.VMEM((2, CHUNK, RET_WIDTH), F32),
                        pltpu.VMEM((2, 1, RET_WIDTH), F32),
                        pltpu.VMEM((2, CHUNK, RET_HEADS * CHUNK), F32)],
        compiler_params=_cparams(("arbitrary", "arbitrary")),
        name="ret_scan",
    )(p, p, p, cos_t, sin_t, p, p, p, cos_t, sin_t, lgx, lgp)


MERGE_SUB = 2


def _merge_kernel(lay, tm, emit_h2, *refs):
    for k in range(MERGE_SUB):
        _merge_tile(lay, tm, emit_h2, pl.program_id(0) * MERGE_SUB + k, k, *refs)


def _merge_tile(lay, tm, emit_h2, i, k, x_ref, m_ref, gate_ref, z_ref, g_ref, um_ref, up_ref, un_ref,
                sf_ref, sb_ref, rf_ref, rb_ref, sng_ref, band_ref, pw_ref, ps_ref, wb_ref, wo_ref, n2g_ref,
                wrh_ref, wrl_ref, *rest):
    if emit_h2:
        xo_ref, h2_ref, lg_ref = rest
    else:
        (xo_ref,) = rest
    rows = pl.ds(k * tm, tm)
    ctx_tiles = lay.n_ctx_rows // tm
    per_ctx = lay.ctx // tm
    per_lat = lay.seq // tm
    in_ctx = i < ctx_tiles
    t_in_seg = jnp.where(in_ctx, i % per_ctx, (i - ctx_tiles) % per_lat)
    seg_tiles = jnp.where(in_ctx, per_ctx, per_lat)
    seg_len = jnp.where(in_ctx, lay.ctx, lay.seq)
    pos = t_in_seg * tm + lax.broadcasted_iota(jnp.int32, (tm, 1), 0)

    ys = (sf_ref[rows, :] + sb_ref[rows, :]) * _silu(z_ref[rows, :].astype(F32))
    s_br = ys * lax.rsqrt(jnp.mean(ys * ys, axis=-1, keepdims=True) + EPS) * sng_ref[...]

    zb = jnp.zeros((), BF16)
    fill = jnp.zeros((SHIFT_PAD - HALO, POOL_WIDTH), BF16)
    um = um_ref[rows, :]
    before = up_ref[...] if k == 0 else um_ref[k * tm - HALO:k * tm, :]
    after = un_ref[...] if k == MERGE_SUB - 1 else um_ref[(k + 1) * tm:(k + 1) * tm + HALO, :]
    u_ext = jnp.concatenate([fill, jnp.where(t_in_seg == 0, zb, before), um,
                             jnp.where(t_in_seg == seg_tiles - 1, zb, after), fill], axis=0)
    pooled = []
    for gi, w in enumerate(POOL_WINDOWS):
        left = w // 2
        right = w - 1 - left
        ls = slice(gi * POOL_GROUP, (gi + 1) * POOL_GROUP)
        tot = jnp.dot(band_ref[gi], u_ext[:, ls], preferred_element_type=F32)
        cnt = (jnp.minimum(pos + right, seg_len - 1) + 1 - jnp.maximum(pos - left, 0)).astype(F32)
        mixed = tot / cnt - um[:, ls].astype(F32)
        pooled.append(jnp.dot(mixed.astype(BF16), pw_ref[gi], preferred_element_type=F32))
    p_br = jnp.concatenate(pooled, axis=-1) * ps_ref[...]

    yr = rf_ref[rows, :] + rb_ref[rows, :]
    normed = []
    for h in range(RET_HEADS):
        yh = yr[:, h * RET_DIM:(h + 1) * RET_DIM]
        mu = jnp.mean(yh, axis=-1, keepdims=True)
        dv = yh - mu
        var = jnp.mean(dv * dv, axis=-1, keepdims=True)
        normed.append(dv * lax.rsqrt(var + EPS))
    r_br = jnp.concatenate(normed, axis=-1) * _silu(g_ref[rows, :].astype(F32))

    acc = None
    for bi, br in enumerate((s_br, p_br, r_br)):
        gate = _sigmoid(gate_ref[rows, bi * D_MODEL:(bi + 1) * D_MODEL].astype(F32))
        term = gate * jnp.dot(br.astype(BF16), wb_ref[bi], preferred_element_type=F32)
        acc = term if acc is None else acc + term
    mix = jnp.dot(acc.astype(BF16), wo_ref[...], preferred_element_type=F32)
    xn = x_ref[rows, :] + m_ref[2] * mix
    xo_ref[rows, :] = xn
    if emit_h2:
        y = xn * lax.rsqrt(jnp.mean(xn * xn, axis=-1, keepdims=True) + EPS) * n2g_ref[...]
        h2 = y * (1.0 + m_ref[4]) + m_ref[3]
        h2_ref[rows, :] = h2
        hh = h2.astype(BF16)
        hl = (h2 - hh.astype(F32)).astype(BF16)
        wh, wl = wrh_ref[...], wrl_ref[...]
        lg_ref[rows, :] = (jnp.dot(hh, wh, preferred_element_type=F32)
                           + (jnp.dot(hh, wl, preferred_element_type=F32)
                              + jnp.dot(hl, wh, preferred_element_type=F32)))


def _merge(lay, x, mods, p, ssd_f, ssd_b, ret_f, ret_b, ssd_norm_g, pool_w, pool_scale, w_branch, w_out,
           norm2_g, w_router_pad, emit_h2):
    n, d = x.shape
    tm = _row_tile(lay, 256)
    bt = MERGE_SUB * tm
    assert lay.n_ctx_rows % bt == 0 and lay.seq % bt == 0
    n_halo = n // HALO
    per = bt // HALO
    pool_blk = P_POOL // POOL_WIDTH
    wr_hi = w_router_pad.astype(BF16)
    wr_lo = (w_router_pad - wr_hi.astype(F32)).astype(BF16)
    bands = jnp.asarray(_band_consts(tm, [(-(w // 2), w - 1 - w // 2) for w in POOL_WINDOWS]), BF16)
    row = lambda i: (i, 0)
    const2 = lambda i: (0, 0)
    const3 = lambda i: (0, 0, 0)
    in_specs = [
        pl.BlockSpec((bt, d), row),
        pl.BlockSpec((None, 6, 1, d), lambda i: (lay.mod_row(i, bt), 0, 0, 0)),
        pl.BlockSpec((bt, N_BRANCH * d), lambda i: (i, P_GATE // (N_BRANCH * d))),
        pl.BlockSpec((bt, SSD_INNER), lambda i: (i, P_Z // SSD_INNER)),
        pl.BlockSpec((bt, RET_WIDTH), lambda i: (i, P_G // RET_WIDTH)),
        pl.BlockSpec((bt, POOL_WIDTH), lambda i: (i, pool_blk)),
        pl.BlockSpec((HALO, POOL_WIDTH), lambda i: (jnp.maximum(i * per - 1, 0), pool_blk)),
        pl.BlockSpec((HALO, POOL_WIDTH), lambda i: (jnp.minimum(i * per + per, n_halo - 1), pool_blk)),
        pl.BlockSpec((bt, SSD_INNER), row),
        pl.BlockSpec((bt, SSD_INNER), row),
        pl.BlockSpec((bt, RET_WIDTH), row),
        pl.BlockSpec((bt, RET_WIDTH), row),
        pl.BlockSpec((1, SSD_INNER), const2),
        pl.BlockSpec((len(POOL_WINDOWS), tm, tm + 2 * SHIFT_PAD), const3),
        pl.BlockSpec((len(POOL_WINDOWS), POOL_GROUP, POOL_GROUP), const3),
        pl.BlockSpec((1, POOL_WIDTH), const2),
        pl.BlockSpec((N_BRANCH, SSD_INNER, d), const3),
        pl.BlockSpec((d, d), const2),
        pl.BlockSpec((1, d), const2),
        pl.BlockSpec((d, LANES), const2),
        pl.BlockSpec((d, LANES), const2),
    ]
    out_shape = [jax.ShapeDtypeStruct((n, d), F32)]
    out_specs = [pl.BlockSpec((bt, d), row)]
    if emit_h2:
        out_shape += [jax.ShapeDtypeStruct((n, d), F32), jax.ShapeDtypeStruct((n, LANES), F32)]
        out_specs += [pl.BlockSpec((bt, d), row), pl.BlockSpec((bt, LANES), row)]
    return pl.pallas_call(
        functools.partial(_merge_kernel, lay, tm, emit_h2),
        out_shape=tuple(out_shape),
        grid=(n // bt,),
        in_specs=in_specs,
        out_specs=tuple(out_specs),
        compiler_params=_cparams(("arbitrary",)),
        name="merge_h2" if emit_h2 else "merge",
    )(x, mods, p, p, p, p, p, p, ssd_f, ssd_b, ret_f, ret_b, ssd_norm_g.reshape(1, -1),
      bands, pool_w.astype(BF16), pool_scale.reshape(1, -1), w_branch.astype(BF16), w_out.astype(BF16),
      norm2_g.reshape(1, -1), wr_hi, wr_lo)


def _ffn_kernel(x_ref, m_ref, g_ref, wg_ref, wu_ref, w2_ref, o_ref, h_ref, acc_ref):
    j = pl.program_id(1)

    @pl.when(j == 0)
    def _():
        x = x_ref[...]
        y = x * lax.rsqrt(jnp.mean(x * x, axis=-1, keepdims=True) + EPS) * g_ref[...]
        h_ref[...] = (y * (1.0 + m_ref[4]) + m_ref[3]).astype(BF16)
        acc_ref[...] = jnp.zeros_like(acc_ref)

    h = h_ref[...]
    gt = jnp.dot(h, wg_ref[...], preferred_element_type=F32)
    up = jnp.dot(h, wu_ref[...], preferred_element_type=F32)
    acc_ref[...] += jnp.dot((_silu(gt) * up).astype(BF16), w2_ref[...], preferred_element_type=F32)

    @pl.when(j == pl.num_programs(1) - 1)
    def _():
        o_ref[...] = x_ref[...] + m_ref[5] * acc_ref[...]


def _ffn_dense(lay, x, mods, g, w13, w2):
    n, d = x.shape
    ff = w2.shape[0]
    tm = _row_tile(lay, 1024)
    tf = 1408
    nf = ff // tf
    return pl.pallas_call(
        _ffn_kernel,
        out_shape=jax.ShapeDtypeStruct((n, d), F32),
        grid=(n // tm, nf),
        in_specs=[
            pl.BlockSpec((tm, d), lambda i, j: (i, 0)),
            pl.BlockSpec((None, 6, 1, d), lambda i, j: (lay.mod_row(i, tm), 0, 0, 0)),
            pl.BlockSpec((1, d), lambda i, j: (0, 0)),
            pl.BlockSpec((d, tf), lambda i, j: (0, j)),
            pl.BlockSpec((d, tf), lambda i, j: (0, j + nf)),
            pl.BlockSpec((tf, d), lambda i, j: (j, 0)),
        ],
        out_specs=pl.BlockSpec((tm, d), lambda i, j: (i, 0)),
        scratch_shapes=[pltpu.VMEM((tm, d), BF16), pltpu.VMEM((tm, d), F32)],
        compiler_params=_cparams(("arbitrary", "arbitrary")),
        name="ffn_dense",
    )(x, mods, g, w13, w13, w2)


ROUTE_TM = 1024
R_E0, R_E1, R_RANK0, R_RANK1, R_W0, R_W1 = range(6)


def _route_kernel(lg_ref, tri_ref, o_ref, cnt_ref, run_ref):
    i = pl.program_id(0)

    @pl.when(i == 0)
    def _():
        run_ref[...] = jnp.zeros_like(run_ref)

    tm = lg_ref.shape[0]
    lane = lax.broadcasted_iota(jnp.int32, (tm, LANES), 1).astype(F32)
    lg = jnp.where(lane < N_EXPERTS, lg_ref[...], -jnp.inf)
    m1 = jnp.max(lg, axis=-1, keepdims=True)
    i1 = jnp.min(jnp.where(lg == m1, lane, float(LANES)), axis=-1, keepdims=True)
    l2 = jnp.where(lane == i1, -jnp.inf, lg)
    m2 = jnp.max(l2, axis=-1, keepdims=True)
    i2 = jnp.min(jnp.where(l2 == m2, lane, float(LANES)), axis=-1, keepdims=True)
    e21 = jnp.exp(m2 - m1)
    w1 = 1.0 / (1.0 + e21)
    w2 = e21 * w1
    sel1, sel2 = lane == i1, lane == i2
    memb = jnp.where(sel1, 1.0, jnp.where(sel2, 1.0, 0.0))
    before = jnp.dot(tri_ref[...], memb.astype(BF16), preferred_element_type=F32) + run_ref[...]
    r1 = jnp.sum(jnp.where(sel1, before, 0.0), axis=-1, keepdims=True)
    r2 = jnp.sum(jnp.where(sel2, before, 0.0), axis=-1, keepdims=True)
    rec = jnp.zeros((tm, LANES), F32)
    for k, v in ((R_E0, i1), (R_E1, i2), (R_RANK0, r1), (R_RANK1, r2), (R_W0, w1), (R_W1, w2)):
        rec = jnp.where(lane == k, v, rec)
    o_ref[...] = rec
    run_ref[...] += jnp.sum(memb, axis=0, keepdims=True)
    cnt_ref[...] = jnp.broadcast_to(run_ref[...], cnt_ref.shape)


def _route(logits):
    n = logits.shape[0]
    tm = ROUTE_TM
    while n % tm:
        tm //= 2
    i = np.arange(tm)
    tri = jnp.asarray((i[None, :] < i[:, None]).astype(np.float32), BF16)
    return pl.pallas_call(
        _route_kernel,
        out_shape=(jax.ShapeDtypeStruct((n, LANES), F32), jax.ShapeDtypeStruct((8, LANES), F32)),
        grid=(n // tm,),
        in_specs=[pl.BlockSpec((tm, LANES), lambda i: (i, 0)), pl.BlockSpec((tm, tm), lambda i: (0, 0))],
        out_specs=(pl.BlockSpec((tm, LANES), lambda i: (i, 0)), pl.BlockSpec((8, LANES), lambda i: (0, 0))),
        scratch_shapes=[pltpu.VMEM((1, LANES), F32)],
        compiler_params=_cparams(("arbitrary",)),
        name="moe_route",
    )(logits, tri)


MOE_BM = 1024


def _slot_plan(rec, counts):
    n = rec.shape[0]
    cnt = counts[0, :N_EXPERTS].astype(jnp.int32)
    padded = (cnt + MOE_BM - 1) // MOE_BM * MOE_BM
    pad_ends = jnp.cumsum(padded)
    pad_starts = pad_ends - padded
    n_blocks = -(-(n * TOP_K + N_EXPERTS * (MOE_BM - 1)) // MOE_BM)
    n_used = pad_ends[-1] // MOE_BM
    blk = jnp.minimum(jnp.arange(n_blocks, dtype=jnp.int32), n_used - 1)
    block_e = jnp.minimum(jnp.sum((blk[:, None] * MOE_BM >= pad_ends[None, :]).astype(jnp.int32), axis=1),
                          N_EXPERTS - 1)
    e = rec[:, R_E0:R_E1 + 1].astype(jnp.int32)
    rank = rec[:, R_RANK0:R_RANK1 + 1].astype(jnp.int32)
    start = jnp.zeros_like(e)
    for k in range(N_EXPERTS):
        start = jnp.where(e == k, pad_starts[k], start)
    dest = start + rank
    last_blk = jnp.where(cnt > 0, pad_ends // MOE_BM - 1, -1).astype(jnp.int32)
    return dest, block_e.astype(jnp.int32), n_used.astype(jnp.int32).reshape(1), last_blk, n_blocks


DISPATCH_TM = 512


def _dispatch_kernel(last_ref, nb_ref, dest_ref, h_ref, xb_ref, zero_ref, sem, zsem):
    i = pl.program_id(0)
    tm = DISPATCH_TM
    n_blocks = xb_ref.shape[0] // MOE_BM

    def clear_block(blk):
        start = pl.multiple_of(blk * MOE_BM, MOE_BM)
        cp = pltpu.make_async_copy(zero_ref, xb_ref.at[pl.ds(start, MOE_BM), :], zsem)
        cp.start()
        cp.wait()

    @pl.when(i == 0)
    def _():
        zero_ref[...] = jnp.zeros_like(zero_ref)
        for e in range(N_EXPERTS):
            @pl.when(last_ref[e] >= 0)
            def _():
                clear_block(last_ref[e])

            @pl.when(nb_ref[0] + e < n_blocks)
            def _():
                clear_block(nb_ref[0] + e)

    for r in range(tm):
        for k in range(TOP_K):
            pltpu.make_async_copy(h_ref.at[pl.ds(r, 1), :],
                                  xb_ref.at[pl.ds(dest_ref[0, TOP_K * r + k], 1), :], sem).start(priority=k)
    for k in range(TOP_K):
        pltpu.make_async_copy(h_ref, xb_ref.at[pl.ds(0, tm), :], sem).wait()


def _dispatch(h2, dest, last_blk, n_used, n_blocks):
    n, d = h2.shape
    tm = DISPATCH_TM
    cap = n_blocks * MOE_BM
    grid_spec = pltpu.PrefetchScalarGridSpec(
        num_scalar_prefetch=2,
        grid=(n // tm,),
        in_specs=[pl.BlockSpec((None, 1, TOP_K * tm), lambda i, lb, nb: (i, 0, 0), memory_space=pltpu.SMEM),
                  pl.BlockSpec((tm, d), lambda i, lb, nb: (i, 0))],
        out_specs=pl.BlockSpec(memory_space=pl.ANY),
        scratch_shapes=[pltpu.VMEM((MOE_BM, d), F32), pltpu.SemaphoreType.DMA, pltpu.SemaphoreType.DMA],
    )
    return pl.pallas_call(
        _dispatch_kernel,
        out_shape=jax.ShapeDtypeStruct((cap, d), F32),
        grid_spec=grid_spec,
        compiler_params=_cparams(("arbitrary",)),
        name="moe_dispatch",
    )(last_blk, n_used, dest.reshape(n // tm, 1, TOP_K * tm), h2)


MOE_TF = 512


def _moe_kernel(be_ref, nb_ref, x_ref, wg_ref, wu_ref, w2_ref, o_ref, h_ref, acc_ref):
    i, j = pl.program_id(0), pl.program_id(1)

    @pl.when(i < nb_ref[0])
    def _():
        @pl.when(j == 0)
        def _():
            h_ref[...] = x_ref[...].astype(BF16)
            acc_ref[...] = jnp.zeros_like(acc_ref)

        h = h_ref[...]
        gt = jnp.dot(h, wg_ref[...].astype(BF16), preferred_element_type=F32)
        up = jnp.dot(h, wu_ref[...].astype(BF16), preferred_element_type=F32)
        acc_ref[...] += jnp.dot((_silu(gt) * up).astype(BF16), w2_ref[...].astype(BF16),
                                preferred_element_type=F32)

        @pl.when(j == pl.num_programs(1) - 1)
        def _():
            o_ref[...] = acc_ref[...]

    @pl.when(jnp.logical_and(i >= nb_ref[0], j == pl.num_programs(1) - 1))
    def _():
        o_ref[...] = jnp.zeros_like(o_ref)


def _moe_blocks(xb, block_e, n_used, w13, w2, li):
    cap, d = xb.shape
    _, ne, ff, _ = w2.shape
    tf = MOE_TF
    nf = ff // tf
    n_blocks = cap // MOE_BM

    def row(i, j, be, nb):
        return (jnp.minimum(i, nb[0] - 1), 0)

    def jj(i, j, nb):
        return jnp.where(i < nb[0], j, nf - 1)

    grid_spec = pltpu.PrefetchScalarGridSpec(
        num_scalar_prefetch=2,
        grid=(n_blocks, nf),
        in_specs=[
            pl.BlockSpec((MOE_BM, d), row),
            pl.BlockSpec((None, None, d, tf), lambda i, j, be, nb: (li, be[i], 0, jj(i, j, nb))),
            pl.BlockSpec((None, None, d, tf), lambda i, j, be, nb: (li, be[i], 0, jj(i, j, nb) + nf)),
            pl.BlockSpec((None, None, tf, d), lambda i, j, be, nb: (li, be[i], jj(i, j, nb), 0)),
        ],
        out_specs=pl.BlockSpec((MOE_BM, d), lambda i, j, be, nb: (i, 0)),
        scratch_shapes=[pltpu.VMEM((MOE_BM, d), BF16), pltpu.VMEM((MOE_BM, d), F32)],
    )
    return pl.pallas_call(
        _moe_kernel,
        out_shape=jax.ShapeDtypeStruct((cap, d), F32),
        grid_spec=grid_spec,
        compiler_params=_cparams(("arbitrary", "arbitrary")),
        name="moe_experts",
    )(block_e, n_used, xb, w13, w13, w2)


COMBINE_TM = 256


def _combine_kernel(final, dest_ref, dnext_ref, x_ref, m_ref, rec_ref, yb_ref, *rest):
    if final:
        g_ref, o_ref, buf_ref, sem = rest
    else:
        o_ref, buf_ref, sem = rest
    tm = COMBINE_TM
    i, n = pl.program_id(0), pl.num_programs(0)

    def issue(idx_ref, slot):
        for r in range(tm):
            for k in range(TOP_K):
                pltpu.make_async_copy(yb_ref.at[pl.ds(idx_ref[0, TOP_K * r + k], 1), :],
                                      buf_ref.at[slot, k, pl.ds(r, 1), :], sem.at[slot]).start(priority=k)

    def finish(slot):
        for k in range(TOP_K):
            pltpu.make_async_copy(yb_ref.at[pl.ds(0, tm), :], buf_ref.at[slot, k], sem.at[slot]).wait()
        rec = rec_ref[...]
        y = rec[:, R_W0:R_W0 + 1] * buf_ref[slot, 0] + rec[:, R_W1:R_W1 + 1] * buf_ref[slot, 1]
        xn = x_ref[...] + m_ref[5] * y
        if final:
            xn = xn * lax.rsqrt(jnp.mean(xn * xn, axis=-1, keepdims=True) + EPS) * g_ref[...]
        o_ref[...] = xn

    @pl.when(i == 0)
    def _():
        issue(dest_ref, 0)

    for slot in range(2):
        @pl.when(i % 2 == slot)
        def _():
            @pl.when(i + 1 < n)
            def _():
                issue(dnext_ref, 1 - slot)

            finish(slot)


def _combine(lay, x, mods, rec, dest, yb, final_g=None):
    n, d = x.shape
    tm = COMBINE_TM
    final = final_g is not None
    off = lay.n_ctx_rows // tm if final else 0
    n_out = n - off * tm
    row = lambda i: (i + off, 0)
    last = n // tm - 1
    in_specs = [pl.BlockSpec((None, 1, TOP_K * tm), lambda i: (i + off, 0, 0), memory_space=pltpu.SMEM),
                pl.BlockSpec((None, 1, TOP_K * tm), lambda i: (jnp.minimum(i + off + 1, last), 0, 0),
                             memory_space=pltpu.SMEM),
                pl.BlockSpec((tm, d), row),
                pl.BlockSpec((None, 6, 1, d), lambda i: (lay.mod_row(i + off, tm), 0, 0, 0)),
                pl.BlockSpec((tm, LANES), row),
                pl.BlockSpec(memory_space=pl.ANY)]
    dest_tiles = dest.reshape(n // tm, 1, TOP_K * tm)
    args = [dest_tiles, dest_tiles, x, mods, rec, yb]
    if final:
        in_specs.append(pl.BlockSpec((1, d), lambda i: (0, 0)))
        args.append(final_g.reshape(1, -1))
    return pl.pallas_call(
        functools.partial(_combine_kernel, final),
        out_shape=jax.ShapeDtypeStruct((n_out, d), F32),
        grid=(n_out // tm,),
        in_specs=in_specs,
        out_specs=pl.BlockSpec((tm, d), lambda i: (i, 0)),
        scratch_shapes=[pltpu.VMEM((2, TOP_K, tm, d), F32), pltpu.SemaphoreType.DMA((2,))],
        compiler_params=_cparams(("arbitrary",)),
        name="moe_combine_final" if final else "moe_combine",
    )(*args)


def _final_kernel(x_ref, g_ref, o_ref):
    x = x_ref[...]
    o_ref[...] = x * lax.rsqrt(jnp.mean(x * x, axis=-1, keepdims=True) + EPS) * g_ref[...]


def _final_norm(lay, x, g):
    n, d = x.shape
    tm = _row_tile(lay, 1024)
    off = lay.n_ctx_rows // tm
    n_lat = lay.b * lay.seq
    return pl.pallas_call(
        _final_kernel,
        out_shape=jax.ShapeDtypeStruct((n_lat, d), F32),
        grid=(n_lat // tm,),
        in_specs=[pl.BlockSpec((tm, d), lambda i: (i + off, 0)), pl.BlockSpec((1, d), lambda i: (0, 0))],
        out_specs=pl.BlockSpec((tm, d), lambda i: (i, 0)),
        compiler_params=_cparams(("arbitrary",)),
        name="final_norm",
    )(x, g.reshape(1, -1))


def _permute_w_in(w):
    d = w.shape[0]
    parts = [w[:, COL_GATE:COL_GATE + N_BRANCH * D_MODEL], w[:, COL_Q:COL_G_END],
             w[:, COL_POOL:COL_POOL + POOL_WIDTH], w[:, COL_Z:COL_Z + SSD_INNER],
             w[:, COL_XBC:COL_XBC + SSD_XBC]]
    w_dt = jnp.pad(w[:, COL_DT:COL_DT + 2 * SSD_HEADS], ((0, 0), (0, LANES - 2 * SSD_HEADS)))
    return jnp.concatenate(parts, axis=1).astype(BF16), w_dt.astype(BF16)


def kernel(x, c, ctx, c_ctx, ada_w, ada_b, norm1_g, norm2_g, w_in, ssd_conv_w, ssd_conv_b, ssd_dt_bias,
           ssd_a_log, ssd_d, ssd_norm_g, pool_w, pool_scale, ret_decay_logit, w_branch, w_out,
           ffn_w13, ffn_w2, moe_router, moe_w13, moe_w2, final_norm_g):
    b, seq, d = x.shape
    ctx_len = ctx.shape[1]
    depth = w_in.shape[0]
    lay = _Layout(b, ctx_len, seq)

    cvec = jnp.concatenate([c, c_ctx[None, :], jnp.zeros((8 - b - 1, d), F32)], axis=0)
    mods_all = _ada_all(cvec, ada_w, ada_b).reshape(depth, 8, 6, 1, d)
    xa = jnp.concatenate([ctx.reshape(-1, d), x.reshape(-1, d)], axis=0)

    for layer in range(depth):
        mods = mods_all[layer]
        w_main, w_dt = _permute_w_in(w_in[layer])
        p, dt_raw = _in_proj(lay, xa, mods, norm1_g[layer].reshape(1, -1), w_main, w_dt)
        ssd_f, ssd_b = _ssd_scan(lay, p, dt_raw, ssd_conv_w[layer], ssd_conv_b[layer], ssd_dt_bias[layer],
                                 ssd_a_log[layer], ssd_d[layer])
        ret_f, ret_b = _ret_scan(lay, p, ret_decay_logit[layer])
        is_moe = layer % 2 == 1
        if is_moe:
            w_r = jnp.pad(moe_router[layer // 2], ((0, 0), (0, LANES - N_EXPERTS)))
        else:
            w_r = jnp.zeros((d, LANES), F32)
        outs = _merge(lay, xa, mods, p, ssd_f, ssd_b, ret_f, ret_b, ssd_norm_g[layer], pool_w[layer],
                      pool_scale[layer], w_branch[layer], w_out[layer], norm2_g[layer], w_r, is_moe)
        if not is_moe:
            (xa,) = outs
            xa = _ffn_dense(lay, xa, mods, norm2_g[layer].reshape(1, -1),
                            ffn_w13[layer // 2].astype(BF16), ffn_w2[layer // 2].astype(BF16))
        else:
            xa, h2, logits = outs
            rec, counts = _route(logits)
            dest, block_e, n_used, last_blk, n_blocks = _slot_plan(rec, counts)
            xb = _dispatch(h2, dest, last_blk, n_used, n_blocks)
            yb = _moe_blocks(xb, block_e, n_used, moe_w13, moe_w2, layer // 2)
            if layer == depth - 1:
                return _combine(lay, xa, mods, rec, dest, yb, final_norm_g).reshape(b, seq, d)
            xa = _combine(lay, xa, mods, rec, dest, yb)
    return _final_norm(lay, xa, final_norm_g).reshape(b, seq, d)
```

```python
import functools

import numpy as np
import jax
import jax.numpy as jnp
from jax import lax
from jax.experimental import pallas as pl
from jax.experimental.pallas import tpu as pltpu

F32 = jnp.float32
BF16 = jnp.bfloat16
HIGHEST = lax.Precision.HIGHEST

D_MODEL = 1024
GRID_W = 64
EPS = 1e-6
CHUNK = 128
SCAN_SUB = 2
SCAN_BLK = SCAN_SUB * CHUNK
HALO = 16
SSD_HEADS = 8
SSD_HEAD_DIM = 64
SSD_INNER = 512
SSD_STATE = 64
SSD_GROUPS = 2
SSD_CONV = 5
SSD_XBC = 768
POOL_WINDOWS = (2, 4, 8, 16)
POOL_WIDTH = 512
POOL_GROUP = 128
RET_HEADS = 8
RET_DIM = 64
RET_WIDTH = 512
ROPE_BASE = 10000.0
N_BRANCH = 3
N_EXPERTS = 8
TOP_K = 2
LANES = 128
HEAD_GROUP = 4
GROUP_W = HEAD_GROUP * 64

COL_Z = 0
COL_XBC = 512
COL_DT = 1280
COL_POOL = 1296
COL_Q = 1808
COL_G_END = 3856
COL_GATE = 3856
IN_COLS = 6928
P_GATE = 0
P_Q = 3072
P_K = 3584
P_V = 4096
P_G = 4608
P_POOL = 5120
P_Z = 5632
P_XBC = 6144
P_COLS = 6912

VMEM_LIMIT = 56 * 1024 * 1024


def _sigmoid(v):
    return 0.5 * jnp.tanh(0.5 * v) + 0.5


def _silu(v):
    return v * _sigmoid(v)


def _softplus(v):
    return jnp.maximum(v, 0.0) + jnp.log1p(jnp.exp(-jnp.abs(v)))


def _log_sigmoid(v):
    return -_softplus(-v)


def _cparams(sem):
    return pltpu.CompilerParams(dimension_semantics=sem, vmem_limit_bytes=VMEM_LIMIT)


def _split3(x):
    hi = x.astype(BF16)
    r = x - hi.astype(F32)
    mid = r.astype(BF16)
    lo = (r - mid.astype(F32)).astype(BF16)
    return hi, mid, lo


def _dot_sel_right(x, m):
    return sum(jnp.dot(part, m, preferred_element_type=F32) for part in _split3(x))


def _dot_sel_left(m, x):
    return sum(jnp.dot(m, part, preferred_element_type=F32) for part in _split3(x))


SHIFT_PAD = 64


def _band_consts(rows, windows):
    m = np.zeros((len(windows), rows, rows + 2 * SHIFT_PAD), np.float32)
    t = np.arange(rows)
    for i, (lo, hi) in enumerate(windows):
        for o in range(lo, hi + 1):
            m[i, t, SHIFT_PAD + t + o] = 1.0
    return m


def _block_mask(rows, cols, row_blk, col_blk):
    r = lax.broadcasted_iota(jnp.int32, (rows, cols), 0) // row_blk
    c = lax.broadcasted_iota(jnp.int32, (rows, cols), 1) // col_blk
    return r == c


def _ada_kernel(c_ref, w_ref, b_ref, o_ref):
    cv = c_ref[...]
    o_ref[...] = jnp.dot(_silu(cv), w_ref[...], precision=HIGHEST,
                         preferred_element_type=F32) + b_ref[...]


def _ada_all(cvec, ada_w, ada_b):
    depth, d, n6 = ada_w.shape
    tn = 1536
    return pl.pallas_call(
        _ada_kernel,
        out_shape=jax.ShapeDtypeStruct((depth, 8, n6), F32),
        grid=(depth, n6 // tn),
        in_specs=[
            pl.BlockSpec((8, d), lambda l, j: (0, 0)),
            pl.BlockSpec((None, d, tn), lambda l, j: (l, 0, j)),
            pl.BlockSpec((None, 1, tn), lambda l, j: (l, 0, j)),
        ],
        out_specs=pl.BlockSpec((None, 8, tn), lambda l, j: (l, 0, j)),
        compiler_params=_cparams(("arbitrary", "arbitrary")),
        name="ada_mods",
    )(cvec, ada_w, ada_b.reshape(depth, 1, n6))


class _Layout:
    def __init__(self, b, ctx_len, seq):
        self.b, self.ctx, self.seq = b, ctx_len, seq
        self.n_ctx_rows = b * ctx_len
        self.n = b * (ctx_len + seq)
        assert ctx_len % SCAN_BLK == 0 and seq % SCAN_BLK == 0
        self.cblk = ctx_len // SCAN_BLK
        self.lblk = seq // SCAN_BLK
        self.nblk = self.cblk + self.lblk

    def mod_row(self, tile, tm):
        ctx_tiles = self.n_ctx_rows // tm
        per_b = self.seq // tm
        return jnp.where(tile < ctx_tiles, self.b, (tile - ctx_tiles) // per_b)

    def scan_row_block(self, bi, c):
        return jnp.where(c < self.cblk, bi * self.cblk + c,
                         self.b * self.cblk + bi * self.lblk + (c - self.cblk))

    def fwd_block(self, s):
        return s

    def bwd_block(self, s):
        return jnp.where(s < self.cblk, self.cblk - 1 - s, self.nblk - 1 - (s - self.cblk))


def _row_tile(lay, cap):
    tm = cap
    while lay.n_ctx_rows % tm or lay.seq % tm:
        tm //= 2
    return tm


def _in_kernel(x_ref, m_ref, g_ref, w_ref, wdt_ref, o_ref, dt_ref, h_ref):
    @pl.when(pl.program_id(1) == 0)
    def _():
        x = x_ref[...]
        y = x * lax.rsqrt(jnp.mean(x * x, axis=-1, keepdims=True) + EPS) * g_ref[...]
        h_ref[...] = (y * (1.0 + m_ref[1]) + m_ref[0]).astype(BF16)
        dt_ref[...] = jnp.dot(h_ref[...], wdt_ref[...], preferred_element_type=F32)

    o_ref[...] = jnp.dot(h_ref[...], w_ref[...], preferred_element_type=F32).astype(BF16)


def _in_proj(lay, x, mods, g, w, w_dt):
    n, d = x.shape
    tm = _row_tile(lay, 1024)
    tn = 2304
    return pl.pallas_call(
        _in_kernel,
        out_shape=(jax.ShapeDtypeStruct((n, P_COLS), BF16), jax.ShapeDtypeStruct((n, LANES), F32)),
        grid=(n // tm, P_COLS // tn),
        in_specs=[
            pl.BlockSpec((tm, d), lambda i, j: (i, 0)),
            pl.BlockSpec((None, 6, 1, d), lambda i, j: (lay.mod_row(i, tm), 0, 0, 0)),
            pl.BlockSpec((1, d), lambda i, j: (0, 0)),
            pl.BlockSpec((d, tn), lambda i, j: (0, j)),
            pl.BlockSpec((d, LANES), lambda i, j: (0, 0)),
        ],
        out_specs=(pl.BlockSpec((tm, tn), lambda i, j: (i, j)), pl.BlockSpec((tm, LANES), lambda i, j: (i, 0))),
        scratch_shapes=[pltpu.VMEM((tm, d), BF16)],
        compiler_params=_cparams(("arbitrary", "arbitrary")),
        name="in_proj",
    )(x, mods, g, w, w_dt)


def _tri_consts():
    i = np.arange(CHUNK)
    fwd = (i[None, :] <= i[:, None]).astype(np.float32)
    bwd = (i[None, :] >= i[:, None]).astype(np.float32)
    return np.stack([fwd, bwd])


def _expand_consts(heads, width):
    e = np.zeros((2, LANES, heads * width), np.float32)
    for d in range(2):
        for h in range(heads):
            e[d, d * heads + h, h * width:(h + 1) * width] = 1.0
    return e


def _ssd_direction(d, c, lay, xm_ref, xp_ref, xn_ref, dt_ref, tri_ref, exp_ref, cw_ref, cb_ref,
                   dtb_ref, alog_ref, dskip_ref, st_ref, o_ref):
    is_start = jnp.logical_or(c == 0, c == lay.cblk)
    is_end = jnp.logical_or(c == lay.cblk - 1, c == lay.nblk - 1)
    xm = xm_ref[...].astype(F32)
    xw = jnp.concatenate([jnp.where(is_start, 0.0, xp_ref[...].astype(F32)), xm,
                          jnp.where(is_end, 0.0, xn_ref[...].astype(F32))], axis=0)
    mid = SSD_CONV // 2
    acc = cb_ref[...] + cw_ref[mid:mid + 1, :] * xm
    for k in range(SSD_CONV):
        if k != mid:
            shifted = pltpu.roll(xw, (mid - k) % xw.shape[0], 0)[HALO:HALO + SCAN_BLK, :]
            acc = acc + cw_ref[k:k + 1, :] * shifted
    xbc_blk = _silu(acc)
    dt_blk = _softplus(dt_ref[...] + dtb_ref[...])
    for k in (range(SCAN_SUB) if d == 0 else reversed(range(SCAN_SUB))):
        rows = slice(k * CHUNK, (k + 1) * CHUNK)
        _ssd_chunk(d, xbc_blk[rows], dt_blk[rows], tri_ref, exp_ref, alog_ref, dskip_ref, st_ref,
                   o_ref.at[pl.ds(k * CHUNK, CHUNK), :])


def _ssd_chunk(d, xbc, dt_all, tri_ref, exp_ref, alog_ref, dskip_ref, st_ref, o_ref):
    xs = xbc[:, :SSD_INNER]
    bm = xbc[:, SSD_INNER:SSD_INNER + LANES]
    cm = xbc[:, SSD_INNER + LANES:]
    bt = bm.T.astype(BF16)
    top = lax.broadcasted_iota(jnp.int32, (CHUNK, LANES), 0) < SSD_STATE
    zero = jnp.zeros_like(bt)
    bt_bd = jnp.concatenate([jnp.where(top, bt, zero), jnp.where(top, zero, bt)], axis=1)
    cb_all = jnp.dot(cm.astype(BF16), bt_bd, preferred_element_type=F32)

    tri = tri_ref[d]
    acs = _dot_sel_left(tri, dt_all * (-jnp.exp(alog_ref[...])))
    acs_t = acs.T
    dt_x = _dot_sel_right(dt_all, exp_ref[d])
    acs_x = _dot_sel_right(acs, exp_ref[d])
    last = CHUNK - 1 if d == 0 else 0
    tot_x = acs_x[last:last + 1, :]
    u = xs * dt_x
    ud = (u * jnp.exp(tot_x - acs_x)).astype(BF16)
    ub = u.astype(BF16)
    off_x = jnp.exp(acs_x)
    cd_x = jnp.exp(tot_x)

    li = lax.broadcasted_iota(jnp.int32, (CHUNK, CHUNK), 0)
    si = lax.broadcasted_iota(jnp.int32, (CHUNK, CHUNK), 1)
    mask = (si <= li) if d == 0 else (si >= li)
    lane = lax.broadcasted_iota(jnp.int32, (CHUNK, LANES), 1)
    cm_sw = pltpu.roll(cm, SSD_STATE, 1)
    u_mask = _block_mask(HEAD_GROUP * CHUNK, GROUP_W, CHUNK, SSD_HEAD_DIM)
    s_mask = _block_mask(GROUP_W, GROUP_W, SSD_STATE, SSD_HEAD_DIM)
    for g in range(SSD_GROUPS):
        gl = slice(g * GROUP_W, (g + 1) * GROUP_W)
        cb = cb_all[:, g * CHUNK:(g + 1) * CHUNK]
        parts = []
        for hh in range(HEAD_GROUP):
            h = g * HEAD_GROUP + hh
            col = acs_x[:, h * SSD_HEAD_DIM:h * SSD_HEAD_DIM + 1]
            row = acs_t[d * SSD_HEADS + h:d * SSD_HEADS + h + 1, :]
            lm = jnp.exp(jnp.where(mask, col - row, -jnp.inf))
            parts.append((cb * lm).astype(BF16))
        in_g = (lane < SSD_STATE) if g == 0 else (lane >= SSD_STATE)
        c_rep = jnp.where(in_g, cm, cm_sw)
        c_off = jnp.concatenate([c_rep, c_rep], axis=1) * off_x[:, gl]
        parts.append(c_off.astype(BF16))
        lhs = jnp.concatenate(parts, axis=1)
        ub_g = ub[:, gl]
        u_bd = jnp.where(u_mask, jnp.concatenate([ub_g] * HEAD_GROUP, axis=0), jnp.zeros((), BF16))
        st = st_ref[d, g]
        rhs = jnp.concatenate([u_bd, st.astype(BF16)], axis=0)
        y_g = jnp.dot(lhs, rhs, preferred_element_type=F32)
        if d == 0:
            y_g = y_g + dskip_ref[:, gl] * xs[:, gl]
        o_ref[:, gl] = y_g
        bt_g = bt[g * SSD_STATE:(g + 1) * SSD_STATE, :]
        upd = jnp.dot(jnp.concatenate([bt_g] * HEAD_GROUP, axis=0), ud[:, gl], preferred_element_type=F32)
        st_ref[d, g] = st * cd_x[:, gl] + jnp.where(s_mask, upd, 0.0)


def _ssd_kernel(lay, xm_f, xp_f, xn_f, dt_f, xm_b, xp_b, xn_b, dt_b, tri_ref, exp_ref, cw_ref, cb_ref,
                dtb_ref, alog_ref, dskip_ref, of_ref, ob_ref, st_ref):
    s = pl.program_id(1)

    @pl.when(s == 0)
    def _():
        st_ref[...] = jnp.zeros_like(st_ref)

    _ssd_direction(0, lay.fwd_block(s), lay, xm_f, xp_f, xn_f, dt_f, tri_ref, exp_ref, cw_ref, cb_ref,
                   dtb_ref, alog_ref, dskip_ref, st_ref, of_ref)
    _ssd_direction(1, lay.bwd_block(s), lay, xm_b, xp_b, xn_b, dt_b, tri_ref, exp_ref, cw_ref, cb_ref,
                   dtb_ref, alog_ref, dskip_ref, st_ref, ob_ref)


def _halo_specs(lay, width, col_block, block_of):
    n_halo = lay.n // HALO
    per = SCAN_BLK // HALO

    def main(bi, s):
        return (lay.scan_row_block(bi, block_of(s)), col_block)

    def prev(bi, s):
        return (jnp.maximum(lay.scan_row_block(bi, block_of(s)) * per - 1, 0), col_block)

    def nxt(bi, s):
        return (jnp.minimum(lay.scan_row_block(bi, block_of(s)) * per + per, n_halo - 1), col_block)

    return [pl.BlockSpec((SCAN_BLK, width), main), pl.BlockSpec((HALO, width), prev),
            pl.BlockSpec((HALO, width), nxt)]


def _ssd_scan(lay, p, dt_raw, conv_w, conv_b, dt_bias, a_log, d_skip):
    n = lay.n
    pad = LANES - 2 * SSD_HEADS
    dtb = jnp.pad(dt_bias.reshape(1, -1), ((0, 0), (0, pad)))
    alog = jnp.pad(a_log.reshape(1, -1), ((0, 0), (0, pad)))
    dskip = jnp.repeat(d_skip, SSD_HEAD_DIM).reshape(1, SSD_INNER)
    tri = jnp.asarray(_tri_consts(), BF16)
    expand = jnp.asarray(_expand_consts(SSD_HEADS, SSD_HEAD_DIM), BF16)
    xbc_blk = P_XBC // SSD_XBC

    def dt_spec(block_of):
        return pl.BlockSpec((SCAN_BLK, LANES), lambda bi, s: (lay.scan_row_block(bi, block_of(s)), 0))

    def out_spec(block_of):
        return pl.BlockSpec((SCAN_BLK, SSD_INNER), lambda bi, s: (lay.scan_row_block(bi, block_of(s)), 0))

    const2 = lambda bi, s: (0, 0)
    const3 = lambda bi, s: (0, 0, 0)
    in_specs = (
        _halo_specs(lay, SSD_XBC, xbc_blk, lay.fwd_block) + [dt_spec(lay.fwd_block)]
        + _halo_specs(lay, SSD_XBC, xbc_blk, lay.bwd_block) + [dt_spec(lay.bwd_block)]
        + [pl.BlockSpec((2, CHUNK, CHUNK), const3),
           pl.BlockSpec((2, LANES, SSD_INNER), const3),
           pl.BlockSpec((SSD_CONV, SSD_XBC), const2),
           pl.BlockSpec((1, SSD_XBC), const2),
           pl.BlockSpec((1, LANES), const2),
           pl.BlockSpec((1, LANES), const2),
           pl.BlockSpec((1, SSD_INNER), const2)])
    return pl.pallas_call(
        functools.partial(_ssd_kernel, lay),
        out_shape=(jax.ShapeDtypeStruct((n, SSD_INNER), F32), jax.ShapeDtypeStruct((n, SSD_INNER), F32)),
        grid=(lay.b, lay.nblk),
        in_specs=in_specs,
        out_specs=(out_spec(lay.fwd_block), out_spec(lay.bwd_block)),
        scratch_shapes=[pltpu.VMEM((2, SSD_GROUPS, GROUP_W, GROUP_W), F32)],
        compiler_params=_cparams(("arbitrary", "arbitrary")),
        name="ssd_scan",
    )(p, p, p, dt_raw, p, p, p, dt_raw, tri, expand, conv_w, conv_b.reshape(1, -1), dtb, alog, dskip)


def _rope_tables(lay):
    n_axis = RET_DIM // 4
    t = np.arange(lay.seq)
    inv = ROPE_BASE ** (-np.arange(n_axis, dtype=np.float32) / n_axis)
    row = (t // GRID_W).astype(np.float32)
    colp = (t % GRID_W).astype(np.float32)
    ang = jnp.concatenate([jnp.asarray(row)[:, None] * inv, jnp.asarray(colp)[:, None] * inv], axis=-1)
    cos, sin = jnp.cos(ang), jnp.sin(ang)
    cos_l = jnp.concatenate([cos, cos, cos, cos], axis=-1)
    sin_l = jnp.concatenate([-sin, sin, -sin, sin], axis=-1)
    cos_t = jnp.concatenate([jnp.ones((lay.ctx, LANES), F32), cos_l], axis=0)
    sin_t = jnp.concatenate([jnp.zeros((lay.ctx, LANES), F32), sin_l], axis=0)
    return cos_t, sin_t


def _rope(xv, cos, sin):
    lane = lax.broadcasted_iota(jnp.int32, (xv.shape[0], LANES), 1)
    first_half = (lane % RET_DIM) < (RET_DIM // 2)
    out = []
    for j in range(RET_WIDTH // LANES):
        v = xv[:, j * LANES:(j + 1) * LANES]
        swapped = jnp.where(first_half, pltpu.roll(v, LANES - RET_DIM // 2, 1),
                            pltpu.roll(v, RET_DIM // 2, 1))
        out.append(v * cos + swapped * sin)
    return jnp.concatenate(out, axis=-1)


def _ret_tables(lgx_ref, lgp_ref, kdec_ref, qdec_ref, cd_ref, dmat_ref):
    idx = lax.broadcasted_iota(jnp.int32, (CHUNK, 1), 0).astype(F32)
    ii = lax.broadcasted_iota(jnp.int32, (CHUNK, CHUNK), 0)
    mi = lax.broadcasted_iota(jnp.int32, (CHUNK, CHUNK), 1)
    for d in range(2):
        lg_x = _log_sigmoid(lgx_ref[d])
        lg_p = _log_sigmoid(lgp_ref[d])
        if d == 0:
            k_pow, q_pow, diff = (CHUNK - 1) - idx, idx + 1.0, ii - mi
        else:
            k_pow, q_pow, diff = idx, CHUNK - idx, mi - ii
        kdec_ref[d] = jnp.exp(lg_x * k_pow)
        qdec_ref[d] = jnp.exp(lg_x * q_pow)
        cd_ref[d] = jnp.exp(lg_x * float(CHUNK))
        dpos = jnp.maximum(diff, 0).astype(F32)
        for h in range(RET_HEADS):
            dmat_ref[d, :, h * CHUNK:(h + 1) * CHUNK] = jnp.where(
                diff >= 0, jnp.exp(lg_p[:, h:h + 1] * dpos), 0.0)


def _ret_direction(d, q_ref, k_ref, v_ref, cos_ref, sin_ref, kdec_ref, qdec_ref, cd_ref, dmat_ref,
                   st_ref, o_ref):
    cos, sin = cos_ref[...], sin_ref[...]
    q_blk = _rope(q_ref[...].astype(F32), cos, sin)
    k_blk = _rope(k_ref[...].astype(F32), cos, sin) * (RET_DIM ** -0.5)
    v_blk = v_ref[...].astype(F32)
    for c in (range(SCAN_SUB) if d == 0 else reversed(range(SCAN_SUB))):
        rows = slice(c * CHUNK, (c + 1) * CHUNK)
        _ret_chunk(d, q_blk[rows], k_blk[rows], v_blk[rows], kdec_ref, qdec_ref, cd_ref, dmat_ref, st_ref,
                   o_ref.at[pl.ds(c * CHUNK, CHUNK), :])


def _ret_chunk(d, q, k, v, kdec_ref, qdec_ref, cd_ref, dmat_ref, st_ref, o_ref):
    vk = (v * kdec_ref[d]).astype(BF16)
    qd = (q * qdec_ref[d]).astype(BF16)
    cd = cd_ref[d]
    qb, vb = q.astype(BF16), v.astype(BF16)
    kt = k.T.astype(BF16)
    k_mask = _block_mask(GROUP_W, HEAD_GROUP * CHUNK, RET_DIM, CHUNK)
    v_mask = _block_mask(HEAD_GROUP * CHUNK, GROUP_W, CHUNK, RET_DIM)
    s_mask = _block_mask(GROUP_W, GROUP_W, RET_DIM, RET_DIM)
    zero = jnp.zeros((), BF16)
    for g in range(RET_HEADS // HEAD_GROUP):
        gl = slice(g * GROUP_W, (g + 1) * GROUP_W)
        sl = slice(g * HEAD_GROUP * CHUNK, (g + 1) * HEAD_GROUP * CHUNK)
        kt_g = kt[gl, :]
        k_bd = jnp.where(k_mask, jnp.concatenate([kt_g] * HEAD_GROUP, axis=1), zero)
        s_all = jnp.dot(qb[:, gl], k_bd, preferred_element_type=F32)
        inner = (s_all * dmat_ref[d, :, sl]).astype(BF16)
        lhs = jnp.concatenate([inner, qd[:, gl]], axis=1)
        v_bd = jnp.where(v_mask, jnp.concatenate([vb[:, gl]] * HEAD_GROUP, axis=0), zero)
        st = st_ref[d, g]
        rhs = jnp.concatenate([v_bd, st.astype(BF16)], axis=0)
        o_ref[:, gl] = jnp.dot(lhs, rhs, preferred_element_type=F32)
        upd = jnp.dot(kt_g, vk[:, gl], preferred_element_type=F32)
        st_ref[d, g] = st * cd[:, gl] + jnp.where(s_mask, upd, 0.0)


def _ret_kernel(qf, kf, vf, cosf, sinf, qb, kb, vb, cosb, sinb, lgx_ref, lgp_ref, of_ref, ob_ref,
                st_ref, kdec_ref, qdec_ref, cd_ref, dmat_ref):
    @pl.when(jnp.logical_and(pl.program_id(0) == 0, pl.program_id(1) == 0))
    def _():
        _ret_tables(lgx_ref, lgp_ref, kdec_ref, qdec_ref, cd_ref, dmat_ref)

    @pl.when(pl.program_id(1) == 0)
    def _():
        st_ref[...] = jnp.zeros_like(st_ref)

    _ret_direction(0, qf, kf, vf, cosf, sinf, kdec_ref, qdec_ref, cd_ref, dmat_ref, st_ref, of_ref)
    _ret_direction(1, qb, kb, vb, cosb, sinb, kdec_ref, qdec_ref, cd_ref, dmat_ref, st_ref, ob_ref)


def _ret_scan(lay, p, decay_logit):
    n = lay.n
    cos_t, sin_t = _rope_tables(lay)
    lgx = jnp.repeat(decay_logit, RET_DIM, axis=-1).reshape(2, 1, RET_WIDTH)
    lgp = jnp.pad(decay_logit, ((0, 0), (0, LANES - RET_HEADS))).reshape(2, 1, LANES)

    def specs(block_of):
        def blk(cb):
            return pl.BlockSpec((SCAN_BLK, RET_WIDTH), lambda bi, s: (lay.scan_row_block(bi, block_of(s)), cb))
        tab = pl.BlockSpec((SCAN_BLK, LANES), lambda bi, s: (block_of(s), 0))
        return [blk(P_Q // RET_WIDTH), blk(P_K // RET_WIDTH), blk(P_V // RET_WIDTH), tab, tab]

    def out_spec(block_of):
        return pl.BlockSpec((SCAN_BLK, RET_WIDTH), lambda bi, s: (lay.scan_row_block(bi, block_of(s)), 0))

    const3 = lambda bi, s: (0, 0, 0)
    return pl.pallas_call(
        _ret_kernel,
        out_shape=(jax.ShapeDtypeStruct((n, RET_WIDTH), F32), jax.ShapeDtypeStruct((n, RET_WIDTH), F32)),
        grid=(lay.b, lay.nblk),
        in_specs=specs(lay.fwd_block) + specs(lay.bwd_block)
        + [pl.BlockSpec((2, 1, RET_WIDTH), const3), pl.BlockSpec((2, 1, LANES), const3)],
        out_specs=(out_spec(lay.fwd_block), out_spec(lay.bwd_block)),
        scratch_shapes=[pltpu.VMEM((2, RET_HEADS // HEAD_GROUP, GROUP_W, GROUP_W), F32),
                        pltpu.VMEM((2, CHUNK, RET_WIDTH), F32),
                        pltpu.VMEM((2, CHUNK, RET_WIDTH), F32),
                        pltpu.VMEM((2, 1, RET_WIDTH), F32),
                        pltpu.VMEM((2, CHUNK, RET_HEADS * CHUNK), F32)],
        compiler_params=_cparams(("arbitrary", "arbitrary")),
        name="ret_scan",
    )(p, p, p, cos_t, sin_t, p, p, p, cos_t, sin_t, lgx, lgp)


MERGE_SUB = 2


def _merge_kernel(lay, tm, emit_h2, *refs):
    for k in range(MERGE_SUB):
        _merge_tile(lay, tm, emit_h2, pl.program_id(0) * MERGE_SUB + k, k, *refs)


def _merge_tile(lay, tm, emit_h2, i, k, x_ref, m_ref, gate_ref, z_ref, g_ref, um_ref, up_ref, un_ref,
                sf_ref, sb_ref, rf_ref, rb_ref, sng_ref, band_ref, pw_ref, ps_ref, wb_ref, wo_ref, n2g_ref,
                wrh_ref, wrl_ref, *rest):
    if emit_h2:
        xo_ref, h2_ref, lg_ref = rest
    else:
        (xo_ref,) = rest
    rows = pl.ds(k * tm, tm)
    ctx_tiles = lay.n_ctx_rows // tm
    per_ctx = lay.ctx // tm
    per_lat = lay.seq // tm
    in_ctx = i < ctx_tiles
    t_in_seg = jnp.where(in_ctx, i % per_ctx, (i - ctx_tiles) % per_lat)
    seg_tiles = jnp.where(in_ctx, per_ctx, per_lat)
    seg_len = jnp.where(in_ctx, lay.ctx, lay.seq)
    pos = t_in_seg * tm + lax.broadcasted_iota(jnp.int32, (tm, 1), 0)

    ys = (sf_ref[rows, :] + sb_ref[rows, :]) * _silu(z_ref[rows, :].astype(F32))
    s_br = ys * lax.rsqrt(jnp.mean(ys * ys, axis=-1, keepdims=True) + EPS) * sng_ref[...]

    zb = jnp.zeros((), BF16)
    fill = jnp.zeros((SHIFT_PAD - HALO, POOL_WIDTH), BF16)
    um = um_ref[rows, :]
    before = up_ref[...] if k == 0 else um_ref[k * tm - HALO:k * tm, :]
    after = un_ref[...] if k == MERGE_SUB - 1 else um_ref[(k + 1) * tm:(k + 1) * tm + HALO, :]
    u_ext = jnp.concatenate([fill, jnp.where(t_in_seg == 0, zb, before), um,
                             jnp.where(t_in_seg == seg_tiles - 1, zb, after), fill], axis=0)
    pooled = []
    for gi, w in enumerate(POOL_WINDOWS):
        left = w // 2
        right = w - 1 - left
        ls = slice(gi * POOL_GROUP, (gi + 1) * POOL_GROUP)
        tot = jnp.dot(band_ref[gi], u_ext[:, ls], preferred_element_type=F32)
        cnt = (jnp.minimum(pos + right, seg_len - 1) + 1 - jnp.maximum(pos - left, 0)).astype(F32)
        mixed = tot / cnt - um[:, ls].astype(F32)
        pooled.append(jnp.dot(mixed.astype(BF16), pw_ref[gi], preferred_element_type=F32))
    p_br = jnp.concatenate(pooled, axis=-1) * ps_ref[...]

    yr = rf_ref[rows, :] + rb_ref[rows, :]
    normed = []
    for h in range(RET_HEADS):
        yh = yr[:, h * RET_DIM:(h + 1) * RET_DIM]
        mu = jnp.mean(yh, axis=-1, keepdims=True)
        dv = yh - mu
        var = jnp.mean(dv * dv, axis=-1, keepdims=True)
        normed.append(dv * lax.rsqrt(var + EPS))
    r_br = jnp.concatenate(normed, axis=-1) * _silu(g_ref[rows, :].astype(F32))

    acc = None
    for bi, br in enumerate((s_br, p_br, r_br)):
        gate = _sigmoid(gate_ref[rows, bi * D_MODEL:(bi + 1) * D_MODEL].astype(F32))
        term = gate * jnp.dot(br.astype(BF16), wb_ref[bi], preferred_element_type=F32)
        acc = term if acc is None else acc + term
    mix = jnp.dot(acc.astype(BF16), wo_ref[...], preferred_element_type=F32)
    xn = x_ref[rows, :] + m_ref[2] * mix
    xo_ref[rows, :] = xn
    if emit_h2:
        y = xn * lax.rsqrt(jnp.mean(xn * xn, axis=-1, keepdims=True) + EPS) * n2g_ref[...]
        h2 = y * (1.0 + m_ref[4]) + m_ref[3]
        h2_ref[rows, :] = h2
        hh = h2.astype(BF16)
        hl = (h2 - hh.astype(F32)).astype(BF16)
        wh, wl = wrh_ref[...], wrl_ref[...]
        lg_ref[rows, :] = (jnp.dot(hh, wh, preferred_element_type=F32)
                           + (jnp.dot(hh, wl, preferred_element_type=F32)
                              + jnp.dot(hl, wh, preferred_element_type=F32)))


def _merge(lay, x, mods, p, ssd_f, ssd_b, ret_f, ret_b, ssd_norm_g, pool_w, pool_scale, w_branch, w_out,
           norm2_g, w_router_pad, emit_h2):
    n, d = x.shape
    tm = _row_tile(lay, 256)
    bt = MERGE_SUB * tm
    assert lay.n_ctx_rows % bt == 0 and lay.seq % bt == 0
    n_halo = n // HALO
    per = bt // HALO
    pool_blk = P_POOL // POOL_WIDTH
    wr_hi = w_router_pad.astype(BF16)
    wr_lo = (w_router_pad - wr_hi.astype(F32)).astype(BF16)
    bands = jnp.asarray(_band_consts(tm, [(-(w // 2), w - 1 - w // 2) for w in POOL_WINDOWS]), BF16)
    row = lambda i: (i, 0)
    const2 = lambda i: (0, 0)
    const3 = lambda i: (0, 0, 0)
    in_specs = [
        pl.BlockSpec((bt, d), row),
        pl.BlockSpec((None, 6, 1, d), lambda i: (lay.mod_row(i, bt), 0, 0, 0)),
        pl.BlockSpec((bt, N_BRANCH * d), lambda i: (i, P_GATE // (N_BRANCH * d))),
        pl.BlockSpec((bt, SSD_INNER), lambda i: (i, P_Z // SSD_INNER)),
        pl.BlockSpec((bt, RET_WIDTH), lambda i: (i, P_G // RET_WIDTH)),
        pl.BlockSpec((bt, POOL_WIDTH), lambda i: (i, pool_blk)),
        pl.BlockSpec((HALO, POOL_WIDTH), lambda i: (jnp.maximum(i * per - 1, 0), pool_blk)),
        pl.BlockSpec((HALO, POOL_WIDTH), lambda i: (jnp.minimum(i * per + per, n_halo - 1), pool_blk)),
        pl.BlockSpec((bt, SSD_INNER), row),
        pl.BlockSpec((bt, SSD_INNER), row),
        pl.BlockSpec((bt, RET_WIDTH), row),
        pl.BlockSpec((bt, RET_WIDTH), row),
        pl.BlockSpec((1, SSD_INNER), const2),
        pl.BlockSpec((len(POOL_WINDOWS), tm, tm + 2 * SHIFT_PAD), const3),
        pl.BlockSpec((len(POOL_WINDOWS), POOL_GROUP, POOL_GROUP), const3),
        pl.BlockSpec((1, POOL_WIDTH), const2),
        pl.BlockSpec((N_BRANCH, SSD_INNER, d), const3),
        pl.BlockSpec((d, d), const2),
        pl.BlockSpec((1, d), const2),
        pl.BlockSpec((d, LANES), const2),
        pl.BlockSpec((d, LANES), const2),
    ]
    out_shape = [jax.ShapeDtypeStruct((n, d), F32)]
    out_specs = [pl.BlockSpec((bt, d), row)]
    if emit_h2:
        out_shape += [jax.ShapeDtypeStruct((n, d), F32), jax.ShapeDtypeStruct((n, LANES), F32)]
        out_specs += [pl.BlockSpec((bt, d), row), pl.BlockSpec((bt, LANES), row)]
    return pl.pallas_call(
        functools.partial(_merge_kernel, lay, tm, emit_h2),
        out_shape=tuple(out_shape),
        grid=(n // bt,),
        in_specs=in_specs,
        out_specs=tuple(out_specs),
        compiler_params=_cparams(("arbitrary",)),
        name="merge_h2" if emit_h2 else "merge",
    )(x, mods, p, p, p, p, p, p, ssd_f, ssd_b, ret_f, ret_b, ssd_norm_g.reshape(1, -1),
      bands, pool_w.astype(BF16), pool_scale.reshape(1, -1), w_branch.astype(BF16), w_out.astype(BF16),
      norm2_g.reshape(1, -1), wr_hi, wr_lo)


def _ffn_kernel(x_ref, m_ref, g_ref, wg_ref, wu_ref, w2_ref, o_ref, h_ref, acc_ref):
    j = pl.program_id(1)

    @pl.when(j == 0)
    def _():
        x = x_ref[...]
        y = x * lax.rsqrt(jnp.mean(x * x, axis=-1, keepdims=True) + EPS) * g_ref[...]
        h_ref[...] = (y * (1.0 + m_ref[4]) + m_ref[3]).astype(BF16)
        acc_ref[...] = jnp.zeros_like(acc_ref)

    h = h_ref[...]
    gt = jnp.dot(h, wg_ref[...], preferred_element_type=F32)
    up = jnp.dot(h, wu_ref[...], preferred_element_type=F32)
    acc_ref[...] += jnp.dot((_silu(gt) * up).astype(BF16), w2_ref[...], preferred_element_type=F32)

    @pl.when(j == pl.num_programs(1) - 1)
    def _():
        o_ref[...] = x_ref[...] + m_ref[5] * acc_ref[...]


def _ffn_dense(lay, x, mods, g, w13, w2):
    n, d = x.shape
    ff = w2.shape[0]
    tm = _row_tile(lay, 1024)
    tf = 1408
    nf = ff // tf
    return pl.pallas_call(
        _ffn_kernel,
        out_shape=jax.ShapeDtypeStruct((n, d), F32),
        grid=(n // tm, nf),
        in_specs=[
            pl.BlockSpec((tm, d), lambda i, j: (i, 0)),
            pl.BlockSpec((None, 6, 1, d), lambda i, j: (lay.mod_row(i, tm), 0, 0, 0)),
            pl.BlockSpec((1, d), lambda i, j: (0, 0)),
            pl.BlockSpec((d, tf), lambda i, j: (0, j)),
            pl.BlockSpec((d, tf), lambda i, j: (0, j + nf)),
            pl.BlockSpec((tf, d), lambda i, j: (j, 0)),
        ],
        out_specs=pl.BlockSpec((tm, d), lambda i, j: (i, 0)),
        scratch_shapes=[pltpu.VMEM((tm, d), BF16), pltpu.VMEM((tm, d), F32)],
        compiler_params=_cparams(("arbitrary", "arbitrary")),
        name="ffn_dense",
    )(x, mods, g, w13, w13, w2)


ROUTE_TM = 1024
R_E0, R_E1, R_RANK0, R_RANK1, R_W0, R_W1 = range(6)


def _route_kernel(lg_ref, tri_ref, o_ref, cnt_ref, run_ref):
    i = pl.program_id(0)

    @pl.when(i == 0)
    def _():
        run_ref[...] = jnp.zeros_like(run_ref)

    tm = lg_ref.shape[0]
    lane = lax.broadcasted_iota(jnp.int32, (tm, LANES), 1).astype(F32)
    lg = jnp.where(lane < N_EXPERTS, lg_ref[...], -jnp.inf)
    m1 = jnp.max(lg, axis=-1, keepdims=True)
    i1 = jnp.min(jnp.where(lg == m1, lane, float(LANES)), axis=-1, keepdims=True)
    l2 = jnp.where(lane == i1, -jnp.inf, lg)
    m2 = jnp.max(l2, axis=-1, keepdims=True)
    i2 = jnp.min(jnp.where(l2 == m2, lane, float(LANES)), axis=-1, keepdims=True)
    e21 = jnp.exp(m2 - m1)
    w1 = 1.0 / (1.0 + e21)
    w2 = e21 * w1
    sel1, sel2 = lane == i1, lane == i2
    memb = jnp.where(sel1, 1.0, jnp.where(sel2, 1.0, 0.0))
    before = jnp.dot(tri_ref[...], memb.astype(BF16), preferred_element_type=F32) + run_ref[...]
    r1 = jnp.sum(jnp.where(sel1, before, 0.0), axis=-1, keepdims=True)
    r2 = jnp.sum(jnp.where(sel2, before, 0.0), axis=-1, keepdims=True)
    rec = jnp.zeros((tm, LANES), F32)
    for k, v in ((R_E0, i1), (R_E1, i2), (R_RANK0, r1), (R_RANK1, r2), (R_W0, w1), (R_W1, w2)):
        rec = jnp.where(lane == k, v, rec)
    o_ref[...] = rec
    run_ref[...] += jnp.sum(memb, axis=0, keepdims=True)
    cnt_ref[...] = jnp.broadcast_to(run_ref[...], cnt_ref.shape)


def _route(logits):
    n = logits.shape[0]
    tm = ROUTE_TM
    while n % tm:
        tm //= 2
    i = np.arange(tm)
    tri = jnp.asarray((i[None, :] < i[:, None]).astype(np.float32), BF16)
    return pl.pallas_call(
        _route_kernel,
        out_shape=(jax.ShapeDtypeStruct((n, LANES), F32), jax.ShapeDtypeStruct((8, LANES), F32)),
        grid=(n // tm,),
        in_specs=[pl.BlockSpec((tm, LANES), lambda i: (i, 0)), pl.BlockSpec((tm, tm), lambda i: (0, 0))],
        out_specs=(pl.BlockSpec((tm, LANES), lambda i: (i, 0)), pl.BlockSpec((8, LANES), lambda i: (0, 0))),
        scratch_shapes=[pltpu.VMEM((1, LANES), F32)],
        compiler_params=_cparams(("arbitrary",)),
        name="moe_route",
    )(logits, tri)


MOE_BM = 1024


def _slot_plan(rec, counts):
    n = rec.shape[0]
    cnt = counts[0, :N_EXPERTS].astype(jnp.int32)
    padded = (cnt + MOE_BM - 1) // MOE_BM * MOE_BM
    pad_ends = jnp.cumsum(padded)
    pad_starts = pad_ends - padded
    n_blocks = -(-(n * TOP_K + N_EXPERTS * (MOE_BM - 1)) // MOE_BM)
    n_used = pad_ends[-1] // MOE_BM
    blk = jnp.minimum(jnp.arange(n_blocks, dtype=jnp.int32), n_used - 1)
    block_e = jnp.minimum(jnp.sum((blk[:, None] * MOE_BM >= pad_ends[None, :]).astype(jnp.int32), axis=1),
                          N_EXPERTS - 1)
    e = rec[:, R_E0:R_E1 + 1].astype(jnp.int32)
    rank = rec[:, R_RANK0:R_RANK1 + 1].astype(jnp.int32)
    start = jnp.zeros_like(e)
    for k in range(N_EXPERTS):
        start = jnp.where(e == k, pad_starts[k], start)
    dest = start + rank
    last_blk = jnp.where(cnt > 0, pad_ends // MOE_BM - 1, -1).astype(jnp.int32)
    return dest, block_e.astype(jnp.int32), n_used.astype(jnp.int32).reshape(1), last_blk, n_blocks


DISPATCH_TM = 512


def _dispatch_kernel(last_ref, nb_ref, dest_ref, h_ref, xb_ref, zero_ref, sem, zsem):
    i = pl.program_id(0)
    tm = DISPATCH_TM
    n_blocks = xb_ref.shape[0] // MOE_BM

    def clear_block(blk):
        start = pl.multiple_of(blk * MOE_BM, MOE_BM)
        cp = pltpu.make_async_copy(zero_ref, xb_ref.at[pl.ds(start, MOE_BM), :], zsem)
        cp.start()
        cp.wait()

    @pl.when(i == 0)
    def _():
        zero_ref[...] = jnp.zeros_like(zero_ref)
        for e in range(N_EXPERTS):
            @pl.when(last_ref[e] >= 0)
            def _():
                clear_block(last_ref[e])

            @pl.when(nb_ref[0] + e < n_blocks)
            def _():
                clear_block(nb_ref[0] + e)

    for r in range(tm):
        for k in range(TOP_K):
            pltpu.make_async_copy(h_ref.at[pl.ds(r, 1), :],
                                  xb_ref.at[pl.ds(dest_ref[0, TOP_K * r + k], 1), :], sem).start(priority=k)
    for k in range(TOP_K):
        pltpu.make_async_copy(h_ref, xb_ref.at[pl.ds(0, tm), :], sem).wait()


def _dispatch(h2, dest, last_blk, n_used, n_blocks):
    n, d = h2.shape
    tm = DISPATCH_TM
    cap = n_blocks * MOE_BM
    grid_spec = pltpu.PrefetchScalarGridSpec(
        num_scalar_prefetch=2,
        grid=(n // tm,),
        in_specs=[pl.BlockSpec((None, 1, TOP_K * tm), lambda i, lb, nb: (i, 0, 0), memory_space=pltpu.SMEM),
                  pl.BlockSpec((tm, d), lambda i, lb, nb: (i, 0))],
        out_specs=pl.BlockSpec(memory_space=pl.ANY),
        scratch_shapes=[pltpu.VMEM((MOE_BM, d), F32), pltpu.SemaphoreType.DMA, pltpu.SemaphoreType.DMA],
    )
    return pl.pallas_call(
        _dispatch_kernel,
        out_shape=jax.ShapeDtypeStruct((cap, d), F32),
        grid_spec=grid_spec,
        compiler_params=_cparams(("arbitrary",)),
        name="moe_dispatch",
    )(last_blk, n_used, dest.reshape(n // tm, 1, TOP_K * tm), h2)


MOE_TF = 512


def _moe_kernel(be_ref, nb_ref, x_ref, wg_ref, wu_ref, w2_ref, o_ref, h_ref, acc_ref):
    i, j = pl.program_id(0), pl.program_id(1)

    @pl.when(i < nb_ref[0])
    def _():
        @pl.when(j == 0)
        def _():
            h_ref[...] = x_ref[...].astype(BF16)
            acc_ref[...] = jnp.zeros_like(acc_ref)

        h = h_ref[...]
        gt = jnp.dot(h, wg_ref[...].astype(BF16), preferred_element_type=F32)
        up = jnp.dot(h, wu_ref[...].astype(BF16), preferred_element_type=F32)
        acc_ref[...] += jnp.dot((_silu(gt) * up).astype(BF16), w2_ref[...].astype(BF16),
                                preferred_element_type=F32)

        @pl.when(j == pl.num_programs(1) - 1)
        def _():
            o_ref[...] = acc_ref[...]

    @pl.when(jnp.logical_and(i >= nb_ref[0], j == pl.num_programs(1) - 1))
    def _():
        o_ref[...] = jnp.zeros_like(o_ref)


def _moe_blocks(xb, block_e, n_used, w13, w2, li):
    cap, d = xb.shape
    _, ne, ff, _ = w2.shape
    tf = MOE_TF
    nf = ff // tf
    n_blocks = cap // MOE_BM

    def row(i, j, be, nb):
        return (jnp.minimum(i, nb[0] - 1), 0)

    def jj(i, j, nb):
        return jnp.where(i < nb[0], j, nf - 1)

    grid_spec = pltpu.PrefetchScalarGridSpec(
        num_scalar_prefetch=2,
        grid=(n_blocks, nf),
        in_specs=[
            pl.BlockSpec((MOE_BM, d), row),
            pl.BlockSpec((None, None, d, tf), lambda i, j, be, nb: (li, be[i], 0, jj(i, j, nb))),
            pl.BlockSpec((None, None, d, tf), lambda i, j, be, nb: (li, be[i], 0, jj(i, j, nb) + nf)),
            pl.BlockSpec((None, None, tf, d), lambda i, j, be, nb: (li, be[i], jj(i, j, nb), 0)),
        ],
        out_specs=pl.BlockSpec((MOE_BM, d), lambda i, j, be, nb: (i, 0)),
        scratch_shapes=[pltpu.VMEM((MOE_BM, d), BF16), pltpu.VMEM((MOE_BM, d), F32)],
    )
    return pl.pallas_call(
        _moe_kernel,
        out_shape=jax.ShapeDtypeStruct((cap, d), F32),
        grid_spec=grid_spec,
        compiler_params=_cparams(("arbitrary", "arbitrary")),
        name="moe_experts",
    )(block_e, n_used, xb, w13, w13, w2)


COMBINE_TM = 512


def _combine_kernel(final, dest_ref, dnext_ref, x_ref, m_ref, rec_ref, yb_ref, *rest):
    if final:
        g_ref, o_ref, buf_ref, sem = rest
    else:
        o_ref, buf_ref, sem = rest
    tm = COMBINE_TM
    i, n = pl.program_id(0), pl.num_programs(0)

    def issue(idx_ref, slot):
        for r in range(tm):
            for k in range(TOP_K):
                pltpu.make_async_copy(yb_ref.at[pl.ds(idx_ref[0, TOP_K * r + k], 1), :],
                                      buf_ref.at[slot, k, pl.ds(r, 1), :], sem.at[slot]).start(priority=k)

    def finish(slot):
        for k in range(TOP_K):
            pltpu.make_async_copy(yb_ref.at[pl.ds(0, tm), :], buf_ref.at[slot, k], sem.at[slot]).wait()
        rec = rec_ref[...]
        y = rec[:, R_W0:R_W0 + 1] * buf_ref[slot, 0] + rec[:, R_W1:R_W1 + 1] * buf_ref[slot, 1]
        xn = x_ref[...] + m_ref[5] * y
        if final:
            xn = xn * lax.rsqrt(jnp.mean(xn * xn, axis=-1, keepdims=True) + EPS) * g_ref[...]
        o_ref[...] = xn

    @pl.when(i == 0)
    def _():
        issue(dest_ref, 0)

    for slot in range(2):
        @pl.when(i % 2 == slot)
        def _():
            @pl.when(i + 1 < n)
            def _():
                issue(dnext_ref, 1 - slot)

            finish(slot)


def _combine(lay, x, mods, rec, dest, yb, final_g=None):
    n, d = x.shape
    tm = COMBINE_TM
    final = final_g is not None
    off = lay.n_ctx_rows // tm if final else 0
    n_out = n - off * tm
    row = lambda i: (i + off, 0)
    last = n // tm - 1
    in_specs = [pl.BlockSpec((None, 1, TOP_K * tm), lambda i: (i + off, 0, 0), memory_space=pltpu.SMEM),
                pl.BlockSpec((None, 1, TOP_K * tm), lambda i: (jnp.minimum(i + off + 1, last), 0, 0),
                             memory_space=pltpu.SMEM),
                pl.BlockSpec((tm, d), row),
                pl.BlockSpec((None, 6, 1, d), lambda i: (lay.mod_row(i + off, tm), 0, 0, 0)),
                pl.BlockSpec((tm, LANES), row),
                pl.BlockSpec(memory_space=pl.ANY)]
    dest_tiles = dest.reshape(n // tm, 1, TOP_K * tm)
    args = [dest_tiles, dest_tiles, x, mods, rec, yb]
    if final:
        in_specs.append(pl.BlockSpec((1, d), lambda i: (0, 0)))
        args.append(final_g.reshape(1, -1))
    return pl.pallas_call(
        functools.partial(_combine_kernel, final),
        out_shape=jax.ShapeDtypeStruct((n_out, d), F32),
        grid=(n_out // tm,),
        in_specs=in_specs,
        out_specs=pl.BlockSpec((tm, d), lambda i: (i, 0)),
        scratch_shapes=[pltpu.VMEM((2, TOP_K, tm, d), F32), pltpu.SemaphoreType.DMA((2,))],
        compiler_params=_cparams(("arbitrary",)),
        name="moe_combine_final" if final else "moe_combine",
    )(*args)


def _final_kernel(x_ref, g_ref, o_ref):
    x = x_ref[...]
    o_ref[...] = x * lax.rsqrt(jnp.mean(x * x, axis=-1, keepdims=True) + EPS) * g_ref[...]


def _final_norm(lay, x, g):
    n, d = x.shape
    tm = _row_tile(lay, 1024)
    off = lay.n_ctx_rows // tm
    n_lat = lay.b * lay.seq
    return pl.pallas_call(
        _final_kernel,
        out_shape=jax.ShapeDtypeStruct((n_lat, d), F32),
        grid=(n_lat // tm,),
        in_specs=[pl.BlockSpec((tm, d), lambda i: (i + off, 0)), pl.BlockSpec((1, d), lambda i: (0, 0))],
        out_specs=pl.BlockSpec((tm, d), lambda i: (i, 0)),
        compiler_params=_cparams(("arbitrary",)),
        name="final_norm",
    )(x, g.reshape(1, -1))


def _permute_w_in(w):
    d = w.shape[0]
    parts = [w[:, COL_GATE:COL_GATE + N_BRANCH * D_MODEL], w[:, COL_Q:COL_G_END],
             w[:, COL_POOL:COL_POOL + POOL_WIDTH], w[:, COL_Z:COL_Z + SSD_INNER],
             w[:, COL_XBC:COL_XBC + SSD_XBC]]
    w_dt = jnp.pad(w[:, COL_DT:COL_DT + 2 * SSD_HEADS], ((0, 0), (0, LANES - 2 * SSD_HEADS)))
    return jnp.concatenate(parts, axis=1).astype(BF16), w_dt.astype(BF16)


def kernel(x, c, ctx, c_ctx, ada_w, ada_b, norm1_g, norm2_g, w_in, ssd_conv_w, ssd_conv_b, ssd_dt_bias,
           ssd_a_log, ssd_d, ssd_norm_g, pool_w, pool_scale, ret_decay_logit, w_branch, w_out,
           ffn_w13, ffn_w2, moe_router, moe_w13, moe_w2, final_norm_g):
    b, seq, d = x.shape
    ctx_len = ctx.shape[1]
    depth = w_in.shape[0]
    lay = _Layout(b, ctx_len, seq)

    cvec = jnp.concatenate([c, c_ctx[None, :], jnp.zeros((8 - b - 1, d), F32)], axis=0)
    mods_all = _ada_all(cvec, ada_w, ada_b).reshape(depth, 8, 6, 1, d)
    xa = jnp.concatenate([ctx.reshape(-1, d), x.reshape(-1, d)], axis=0)

    for layer in range(depth):
        mods = mods_all[layer]
        w_main, w_dt = _permute_w_in(w_in[layer])
        p, dt_raw = _in_proj(lay, xa, mods, norm1_g[layer].reshape(1, -1), w_main, w_dt)
        ssd_f, ssd_b = _ssd_scan(lay, p, dt_raw, ssd_conv_w[layer], ssd_conv_b[layer], ssd_dt_bias[layer],
                                 ssd_a_log[layer], ssd_d[layer])
        ret_f, ret_b = _ret_scan(lay, p, ret_decay_logit[layer])
        is_moe = layer % 2 == 1
        if is_moe:
            w_r = jnp.pad(moe_router[layer // 2], ((0, 0), (0, LANES - N_EXPERTS)))
        else:
            w_r = jnp.zeros((d, LANES), F32)
        outs = _merge(lay, xa, mods, p, ssd_f, ssd_b, ret_f, ret_b, ssd_norm_g[layer], pool_w[layer],
                      pool_scale[layer], w_branch[layer], w_out[layer], norm2_g[layer], w_r, is_moe)
        if not is_moe:
            (xa,) = outs
            xa = _ffn_dense(lay, xa, mods, norm2_g[layer].reshape(1, -1),
                            ffn_w13[layer // 2].astype(BF16), ffn_w2[layer // 2].astype(BF16))
        else:
            xa, h2, logits = outs
            rec, counts = _route(logits)
            dest, block_e, n_used, last_blk, n_blocks = _slot_plan(rec, counts)
            xb = _dispatch(h2, dest, last_blk, n_used, n_blocks)
            yb = _moe_blocks(xb, block_e, n_used, moe_w13, moe_w2, layer // 2)
            if layer == depth - 1:
                return _combine(lay, xa, mods, rec, dest, yb, final_norm_g).reshape(b, seq, d)
            xa = _combine(lay, xa, mods, rec, dest, yb)
    return _final_norm(lay, xa, final_norm_g).reshape(b, seq, d)
```
